```python
import math
import jax, jax.numpy as jnp
from jax import lax
import numpy as np

D_MODEL = 1024
BATCH = 4
SEQ = 4096
DEPTH = 1
DEC_BATCH = 32
DEC_SEQ = 8
PAST_LEN = 8192
PAGE_SIZE = 128

N_HEADS = 8
HEAD_DIM = 64
ATT_W = N_HEADS * HEAD_DIM
MOBA_BLOCK = 256
MOBA_TOPK = 3
Q_CHUNK = 32
CONV_W = D_MODEL // 2
CONV_K = 3
N_EXPERTS = 32
TOP_K = 4
D_FF = D_MODEL
SWIGLU_LIMIT = 7.0
SWIGLU_ALPHA = 1.702
MOE_BLOCK = 128
NUM_BUCKETS = 32
MAX_DISTANCE = 128
LN_EPS = 1e-5
DEEPNORM_ALPHA = (2 * DEPTH) ** 0.25
DEEPNORM_BETA = (8 * DEPTH) ** -0.25
PROJ_W = 3 * ATT_W + 3 * CONV_W + 2 * D_MODEL
PROJ_SPLITS = [ATT_W, 2 * ATT_W, 3 * ATT_W, 3 * ATT_W + CONV_W, 3 * ATT_W + 2 * CONV_W,
               3 * ATT_W + 3 * CONV_W, 3 * ATT_W + 3 * CONV_W + D_MODEL]

kernel_name = "moba_shortconv_gated_moe_decoder_step"


def layer_norm(x, g, b):
    xf = x.astype(jnp.float32)
    mu = jnp.mean(xf, axis=-1, keepdims=True)
    var = jnp.mean(jnp.square(xf - mu), axis=-1, keepdims=True)
    y = (xf - mu) * lax.rsqrt(var + LN_EPS) * g.astype(jnp.float32) + b.astype(jnp.float32)
    return y.astype(x.dtype)


def rel_bucket(dist):
    n = jnp.maximum(dist, 0)
    exact = NUM_BUCKETS // 2
    nf = jnp.maximum(n, 1).astype(jnp.float32)
    large = exact + (jnp.log(nf / exact) / math.log(MAX_DISTANCE / exact)
                     * (NUM_BUCKETS - exact)).astype(jnp.int32)
    large = jnp.minimum(large, NUM_BUCKETS - 1)
    return jnp.where(n < exact, n, large)


def to_blocks(t):
    b, l = t.shape[0], t.shape[1]
    nb = max(-(-l // MOBA_BLOCK), MOBA_TOPK)
    t = jnp.pad(t, ((0, 0), (0, nb * MOBA_BLOCK - l), (0, 0), (0, 0)))
    return t.reshape(b, nb, MOBA_BLOCK, N_HEADS, HEAD_DIM).transpose(0, 3, 1, 2, 4)


def moba_attention(q, k_blk, v_blk, q_start, rel_bias):
    b, h, tq, hd = q.shape
    nb = k_blk.shape[2]
    scale = HEAD_DIM ** -0.5
    kmean = jnp.mean(k_blk.astype(jnp.float32), axis=3)
    table_hb = rel_bias.T
    c = Q_CHUNK if tq >= Q_CHUNK else tq
    n_chunks = -(-tq // c)
    q_p = jnp.pad(q, ((0, 0), (0, 0), (0, n_chunks * c - tq), (0, 0)))
    q_c = q_p.reshape(b, h, n_chunks, c, hd).transpose(2, 0, 1, 3, 4)
    pos_c = (q_start + jnp.arange(n_chunks * c, dtype=jnp.int32)).reshape(n_chunks, c)
    bi = jnp.arange(b)[:, None, None, None]
    hi = jnp.arange(h)[None, :, None, None]
    blk_ids = jnp.arange(nb)
    offs = jnp.arange(MOBA_BLOCK, dtype=jnp.int32)

    def one_chunk(args):
        qb, pos = args
        own = pos[0] // MOBA_BLOCK
        gate = jnp.einsum('bhqd,bhnd->bhqn', qb.astype(jnp.float32), kmean)
        gate = jnp.where(blk_ids < own, gate, -jnp.inf)
        _, sel = lax.top_k(gate, MOBA_TOPK)
        sel_ok = jnp.arange(MOBA_TOPK) < own
        k_sel = k_blk[bi, hi, sel]
        v_sel = v_blk[bi, hi, sel]
        s_sel = jnp.einsum('bhqd,bhqrkd->bhqrk', qb, k_sel,
                           preferred_element_type=jnp.float32) * scale
        kpos = sel[..., None] * MOBA_BLOCK + offs
        b_sel = table_hb[hi[..., None], rel_bucket(pos[:, None, None] - kpos)]
        s_sel = jnp.where(sel_ok[:, None], s_sel + b_sel.astype(jnp.float32), -jnp.inf)
        k_own = lax.dynamic_index_in_dim(k_blk, own, axis=2, keepdims=False)
        v_own = lax.dynamic_index_in_dim(v_blk, own, axis=2, keepdims=False)
        s_own = jnp.einsum('bhqd,bhkd->bhqk', qb, k_own,
                           preferred_element_type=jnp.float32) * scale
        dist = pos[:, None] - (own * MOBA_BLOCK + offs)[None, :]
        b_own = table_hb[:, rel_bucket(dist)].astype(jnp.float32)
        s_own = jnp.where(dist >= 0, s_own + b_own, -jnp.inf)
        s = jnp.concatenate([s_sel.reshape(b, h, c, MOBA_TOPK * MOBA_BLOCK), s_own], axis=-1)
        p = jax.nn.softmax(s, axis=-1).astype(v_blk.dtype)
        p_sel = p[..., :MOBA_TOPK * MOBA_BLOCK].reshape(b, h, c, MOBA_TOPK, MOBA_BLOCK)
        p_own = p[..., MOBA_TOPK * MOBA_BLOCK:]
        return (jnp.einsum('bhqrk,bhqrkd->bhqd', p_sel, v_sel)
                + jnp.einsum('bhqk,bhkd->bhqd', p_own, v_own))

    out = lax.map(one_chunk, (q_c, pos_c))
    return out.transpose(1, 2, 0, 3, 4).reshape(b, h, n_chunks * c, hd)[:, :, :tq]


def short_conv(u, buf, w):
    t = u.shape[1]
    ext = jnp.concatenate([buf.astype(u.dtype), u], axis=1)
    y = w[0] * ext[:, 0:t] + w[1] * ext[:, 1:t + 1] + w[2] * ext[:, 2:t + 2]
    return y, ext[:, -(CONV_K - 1):]


def moe(x2d, router_w, router_b, w_gate_up, b_gate_up, w_down, b_down):
    t, d = x2d.shape
    tk = t * TOP_K
    logits = (x2d @ router_w + router_b).astype(jnp.float32)
    top_val, top_idx = lax.top_k(logits, TOP_K)
    gates = jax.nn.softmax(top_val, axis=-1)
    flat_e = top_idx.reshape(-1)
    flat_t = jnp.arange(tk, dtype=jnp.int32) // TOP_K
    flat_g = gates.reshape(-1)
    order = jnp.argsort(flat_e)
    sorted_e = flat_e[order]
    counts = jnp.bincount(flat_e, length=N_EXPERTS)
    padded = (counts + MOE_BLOCK - 1) // MOE_BLOCK * MOE_BLOCK
    pad_end = jnp.cumsum(padded)
    pad_start = pad_end - padded
    raw_start = jnp.cumsum(counts) - counts
    dest = pad_start[sorted_e] + (jnp.arange(tk) - raw_start[sorted_e])
    n_blocks = -(-(tk + N_EXPERTS * (MOE_BLOCK - 1)) // MOE_BLOCK)
    rows = n_blocks * MOE_BLOCK
    row_tok = jnp.full((rows,), t, jnp.int32).at[dest].set(flat_t[order])
    row_gate = jnp.zeros((rows,), jnp.float32).at[dest].set(flat_g[order])
    block_e = jnp.minimum(jnp.searchsorted(pad_end, jnp.arange(n_blocks) * MOE_BLOCK, side='right'),
                          N_EXPERTS - 1)
    x_pad = jnp.concatenate([x2d, jnp.zeros((1, d), x2d.dtype)], axis=0)
    xb = x_pad[row_tok].reshape(n_blocks, MOE_BLOCK, d)

    def expert_block(args):
        xblk, e = args
        hgu = xblk @ w_gate_up[e] + b_gate_up[e]
        g = jnp.minimum(hgu[:, 0::2], SWIGLU_LIMIT)
        u = jnp.clip(hgu[:, 1::2], -SWIGLU_LIMIT, SWIGLU_LIMIT)
        a = g * jax.nn.sigmoid(SWIGLU_ALPHA * g) * (u + 1.0)
        return a @ w_down[e] + b_down[e]

    yb = lax.map(expert_block, (xb, block_e)).reshape(rows, d)
    y = jnp.zeros((t + 1, d), x2d.dtype).at[row_tok].add(yb * row_gate[:, None].astype(yb.dtype))
    return y[:t]


def decoder_layer(x, k_past, v_past, conv_buf, rel_bias, w_in, conv_w, w_att_o, w_conv_o, w_o,
                  ln1_g, ln1_b, router_w, router_b, w_gate_up, b_gate_up, w_down, b_down, ln2_g, ln2_b):
    b, t, _ = x.shape
    p_len = k_past.shape[1]
    proj = x @ w_in
    q, k, v, cb, cc, ch, ga, gb = jnp.split(proj, PROJ_SPLITS, axis=-1)
    q = q.reshape(b, t, N_HEADS, HEAD_DIM)
    k = k.reshape(b, t, N_HEADS, HEAD_DIM)
    v = v.reshape(b, t, N_HEADS, HEAD_DIM)
    k_all = jnp.concatenate([k_past.astype(k.dtype), k], axis=1)
    v_all = jnp.concatenate([v_past.astype(v.dtype), v], axis=1)
    attn = moba_attention(q.transpose(0, 2, 1, 3), to_blocks(k_all), to_blocks(v_all), p_len, rel_bias)
    y_att = attn.transpose(0, 2, 1, 3).reshape(b, t, ATT_W) @ w_att_o
    conv_out, buf_new = short_conv(cc * ch, conv_buf, conv_w)
    y_conv = (cb * conv_out) @ w_conv_o
    merged = jax.nn.sigmoid(ga) * y_att + jax.nn.sigmoid(gb) * y_conv
    x1 = layer_norm(DEEPNORM_ALPHA * x + merged @ w_o, ln1_g, ln1_b)
    ffn = moe(x1.reshape(b * t, D_MODEL), router_w, router_b, w_gate_up, b_gate_up,
              w_down, b_down).reshape(b, t, D_MODEL)
    x2 = layer_norm(DEEPNORM_ALPHA * x1 + ffn, ln2_g, ln2_b)
    return x2, k, v, buf_new


def setup_inputs(seed: int = 0) -> dict:
    key = jax.random.key(seed)
    ks = jax.random.split(key, 24)
    f32 = jnp.float32
    n_pages = PAST_LEN // PAGE_SIZE
    n_used = DEC_BATCH * n_pages
    n_pool = n_used + max(n_used // 4, 1)

    def nrm(k, shape, s):
        return jax.random.normal(k, shape, f32) * s

    x_prompt = nrm(ks[0], (BATCH, SEQ, D_MODEL), 1.0)
    x_sample = nrm(ks[1], (DEC_BATCH, DEC_SEQ, D_MODEL), 1.0)
    cache_k = nrm(ks[2], (DEPTH, n_pool, PAGE_SIZE, N_HEADS, HEAD_DIM), 1.0)
    cache_v = nrm(ks[3], (DEPTH, n_pool, PAGE_SIZE, N_HEADS, HEAD_DIM), 1.0)
    state_conv = nrm(ks[4], (DEPTH, DEC_BATCH, CONV_K - 1, CONV_W), 1.0)
    page_table = jax.random.permutation(ks[5], n_pool)[:n_used].reshape(DEC_BATCH, n_pages).astype(jnp.int32)
    rel_bias = nrm(ks[6], (NUM_BUCKETS, N_HEADS), 0.5)
    w_in = nrm(ks[7], (DEPTH, D_MODEL, PROJ_W), D_MODEL ** -0.5)
    conv_w = nrm(ks[8], (DEPTH, CONV_K, CONV_W), CONV_K ** -0.5)
    w_att_o = nrm(ks[9], (DEPTH, ATT_W, D_MODEL), ATT_W ** -0.5)
    w_conv_o = nrm(ks[10], (DEPTH, CONV_W, D_MODEL), CONV_W ** -0.5)
    w_o = nrm(ks[11], (DEPTH, D_MODEL, D_MODEL), D_MODEL ** -0.5 * DEEPNORM_BETA)
    ln1_g = 1.0 + nrm(ks[12], (DEPTH, D_MODEL), 0.02)
    ln1_b = nrm(ks[13], (DEPTH, D_MODEL), 0.02)
    router_w = nrm(ks[14], (DEPTH, D_MODEL, N_EXPERTS), D_MODEL ** -0.5)
    router_b = nrm(ks[15], (DEPTH, N_EXPERTS), 0.01)
    w_gate_up = nrm(ks[16], (DEPTH, N_EXPERTS, D_MODEL, 2 * D_FF), D_MODEL ** -0.5)
    b_gate_up = nrm(ks[17], (DEPTH, N_EXPERTS, 2 * D_FF), 0.01)
    w_down = nrm(ks[18], (DEPTH, N_EXPERTS, D_FF, D_MODEL), D_FF ** -0.5 * DEEPNORM_BETA)
    b_down = nrm(ks[19], (DEPTH, N_EXPERTS, D_MODEL), 0.01)
    ln2_g = 1.0 + nrm(ks[20], (DEPTH, D_MODEL), 0.02)
    ln2_b = nrm(ks[21], (DEPTH, D_MODEL), 0.02)
    return {"x_prompt": x_prompt, "x_sample": x_sample, "cache_k": cache_k, "cache_v": cache_v,
            "state_conv": state_conv, "page_table": page_table, "rel_bias": rel_bias,
            "w_in": w_in, "conv_w": conv_w, "w_att_o": w_att_o, "w_conv_o": w_conv_o, "w_o": w_o,
            "ln1_g": ln1_g, "ln1_b": ln1_b, "router_w": router_w, "router_b": router_b,
            "w_gate_up": w_gate_up, "b_gate_up": b_gate_up, "w_down": w_down, "b_down": b_down,
            "ln2_g": ln2_g, "ln2_b": ln2_b}


def reference(x_prompt, x_sample, cache_k, cache_v, state_conv, page_table, rel_bias,
              w_in, conv_w, w_att_o, w_conv_o, w_o, ln1_g, ln1_b, router_w, router_b,
              w_gate_up, b_gate_up, w_down, b_down, ln2_g, ln2_b):
    past_len = page_table.shape[1] * PAGE_SIZE
    hp, hs = x_prompt, x_sample
    empty = jnp.zeros((x_prompt.shape[0], 0, N_HEADS, HEAD_DIM), x_prompt.dtype)
    zero_buf = jnp.zeros((x_prompt.shape[0], CONV_K - 1, CONV_W), x_prompt.dtype)
    kp_l, vp_l, cp_l, ks_l, vs_l, cs_l = [], [], [], [], [], []
    for l in range(DEPTH):
        lw = (w_in[l], conv_w[l], w_att_o[l], w_conv_o[l], w_o[l], ln1_g[l], ln1_b[l],
              router_w[l], router_b[l], w_gate_up[l], b_gate_up[l], w_down[l], b_down[l],
              ln2_g[l], ln2_b[l])
        hp, kp, vp, cp = decoder_layer(hp, empty, empty, zero_buf, rel_bias, *lw)
        k_past = cache_k[l][page_table].reshape(x_sample.shape[0], past_len, N_HEADS, HEAD_DIM)
        v_past = cache_v[l][page_table].reshape(x_sample.shape[0], past_len, N_HEADS, HEAD_DIM)
        hs, kn, vn, cn = decoder_layer(hs, k_past, v_past, state_conv[l], rel_bias, *lw)
        kp_l.append(kp); vp_l.append(vp); cp_l.append(cp)
        ks_l.append(kn); vs_l.append(vn); cs_l.append(cn)
    k_prompt = jnp.stack(kp_l)
    v_prompt = jnp.stack(vp_l)
    conv_prompt = jnp.stack(cp_l)
    k_sample = jnp.stack(ks_l)
    v_sample = jnp.stack(vs_l)
    conv_sample = jnp.stack(cs_l)
    return (hp, hs, k_prompt, v_prompt, conv_prompt, k_sample, v_sample, conv_sample)
```

```python
import functools
import math

import jax
import jax.numpy as jnp
from jax import lax
from jax.experimental import pallas as pl
from jax.experimental.pallas import tpu as pltpu

F32 = jnp.float32
I32 = jnp.int32
MXU_DTYPE = jnp.bfloat16
HIGHEST = lax.Precision.HIGHEST

N_HEADS = 8
HEAD_DIM = 64
ATT_W = N_HEADS * HEAD_DIM
MOBA_BLOCK = 256
MOBA_TOPK = 3
PAGE_SIZE = 128
PAGES_PER_BLOCK = MOBA_BLOCK // PAGE_SIZE
CONV_K = 3
N_EXPERTS = 32
TOP_K = 4
SWIGLU_LIMIT = 7.0
SWIGLU_ALPHA = 1.702
NUM_BUCKETS = 32
NUM_EXACT = NUM_BUCKETS // 2
MAX_DISTANCE = 128
LN_EPS = 1e-5
SCORE_SCALE = HEAD_DIM ** -0.5
NEG = -1e30

SUBLANES = 8
PROJ_TILE = 256
TOKEN_TILE = 256
EXPERT_TILE = 512
SAMPLE_CHUNK_BLOCKS = 8
VMEM_LIMIT = 56 * 1024 * 1024

assert MOBA_BLOCK >= MAX_DISTANCE


def _nt_dims():
    return (((1,), (1,)), ((), ()))


def _sigmoid(x):
    return 1.0 / (1.0 + jnp.exp(-x))


def _bucket(dist):
    n = jnp.maximum(dist, 0)
    nf = jnp.maximum(n, 1).astype(F32)
    large = NUM_EXACT + (jnp.log(nf / NUM_EXACT) / math.log(MAX_DISTANCE / NUM_EXACT)
                         * (NUM_BUCKETS - NUM_EXACT)).astype(I32)
    large = jnp.minimum(large, NUM_BUCKETS - 1)
    return jnp.where(n < NUM_EXACT, n, large)


def _layer_norm(z, g, b):
    mu = jnp.mean(z, axis=-1, keepdims=True)
    zc = z - mu
    var = jnp.mean(zc * zc, axis=-1, keepdims=True)
    return zc * lax.rsqrt(var + LN_EPS) * g + b


def _proj_kernel(x_ref, w_ref, *out_refs, tm, head_major):
    xb = x_ref[...].astype(MXU_DTYPE)

    def mm(c0, width):
        return jnp.dot(xb, w_ref[:, c0:c0 + width], preferred_element_type=F32)

    if head_major:
        qT_ref, khm_ref, vT_ref, k_ref, v_ref, km_ref, u_ref, cb_ref, ga_ref, gb_ref = out_refs
    else:
        q_ref, k_ref, v_ref, u_ref, cb_ref, ga_ref, gb_ref = out_refs
    d_model = x_ref.shape[1]
    q = mm(0, ATT_W) * SCORE_SCALE
    k = mm(ATT_W, ATT_W)
    v = mm(2 * ATT_W, ATT_W)
    k_ref[...] = k
    v_ref[...] = v
    if head_major:
        qT_ref[...] = q.T
        km_ref[...] = jnp.zeros(km_ref.shape, F32)
        for r in range(tm // MOBA_BLOCK):
            kr = k[r * MOBA_BLOCK:(r + 1) * MOBA_BLOCK]
            km_ref[0, r:r + 1, :] = jnp.sum(kr, axis=0, keepdims=True) * (1.0 / MOBA_BLOCK)
            vT_ref[r] = v[r * MOBA_BLOCK:(r + 1) * MOBA_BLOCK].T.astype(MXU_DTYPE)
            for h in range(N_HEADS):
                khm_ref[h, r] = kr[:, h * HEAD_DIM:(h + 1) * HEAD_DIM].astype(MXU_DTYPE)
    else:
        q_ref[...] = q
    c0 = 3 * ATT_W
    cw = u_ref.shape[1]
    cb_ref[...] = mm(c0, cw).astype(cb_ref.dtype)
    u_ref[...] = mm(c0 + cw, cw) * mm(c0 + 2 * cw, cw)
    ga_ref[...] = _sigmoid(mm(c0 + 3 * cw, d_model)).astype(ga_ref.dtype)
    gb_ref[...] = _sigmoid(mm(c0 + 3 * cw + d_model, d_model)).astype(gb_ref.dtype)


def _project(x2d, w_in_b, *, tm, head_major):
    t, d = x2d.shape
    cw = (w_in_b.shape[1] - 3 * ATT_W - 2 * d) // 3
    nt = t // tm
    row = lambda i: (i, 0)
    f32s = lambda shape: jax.ShapeDtypeStruct(shape, F32)
    mxs = lambda shape: jax.ShapeDtypeStruct(shape, MXU_DTYPE)
    tail_shapes = [f32s((t, cw)), mxs((t, cw)), mxs((t, d)), mxs((t, d))]
    tail_specs = [pl.BlockSpec((tm, cw), row), pl.BlockSpec((tm, cw), row),
                  pl.BlockSpec((tm, d), row), pl.BlockSpec((tm, d), row)]
    if head_major:
        nbt = tm // MOBA_BLOCK
        out_shape = [f32s((ATT_W, t)), mxs((N_HEADS, t // MOBA_BLOCK, MOBA_BLOCK, HEAD_DIM)),
                     mxs((t // MOBA_BLOCK, ATT_W, MOBA_BLOCK)), f32s((t, ATT_W)), f32s((t, ATT_W)),
                     f32s((nt, SUBLANES, ATT_W))] + tail_shapes
        out_specs = [pl.BlockSpec((ATT_W, tm), lambda i: (0, i)),
                     pl.BlockSpec((N_HEADS, nbt, MOBA_BLOCK, HEAD_DIM), lambda i: (0, i, 0, 0)),
                     pl.BlockSpec((nbt, ATT_W, MOBA_BLOCK), lambda i: (i, 0, 0)),
                     pl.BlockSpec((tm, ATT_W), row), pl.BlockSpec((tm, ATT_W), row),
                     pl.BlockSpec((1, SUBLANES, ATT_W), lambda i: (i, 0, 0))] + tail_specs
    else:
        out_shape = [f32s((t, ATT_W))] * 3 + tail_shapes
        out_specs = [pl.BlockSpec((tm, ATT_W), row)] * 3 + tail_specs
    return pl.pallas_call(
        functools.partial(_proj_kernel, tm=tm, head_major=head_major),
        grid=(nt,),
        in_specs=[pl.BlockSpec((tm, d), row), pl.BlockSpec(w_in_b.shape, lambda i: (0, 0))],
        out_specs=out_specs,
        out_shape=out_shape,
        compiler_params=pltpu.CompilerParams(dimension_semantics=("arbitrary",),
                                             vmem_limit_bytes=VMEM_LIMIT),
        name="proj",
    )(x2d, w_in_b)


def _softmax_update(carry, s, vT_blk):
    m, l, acc = carry
    m_new = jnp.maximum(m, jnp.max(s, axis=0, keepdims=True))
    alpha = jnp.exp(m - m_new)
    p = jnp.exp(s - m_new)
    l = alpha * l + jnp.sum(p, axis=0, keepdims=True)
    acc = alpha * acc + jnp.dot(vT_blk, p.astype(MXU_DTYPE), preferred_element_type=F32)
    return m_new, l, acc


def _moba_prompt_kernel(tab_ref, qT_ref, k_ref, vT_ref, km_ref, o_ref, selb_ref, b0_ref, b1_ref):
    h = pl.program_id(1)
    i = pl.program_id(2)
    nb = km_ref.shape[1]
    blk = MOBA_BLOCK
    kk = lax.broadcasted_iota(I32, (blk, blk), 0)
    qq = lax.broadcasted_iota(I32, (blk, blk), 1)

    def bias_tile(dist):
        bucket = _bucket(dist)
        out = jnp.zeros(dist.shape, F32)
        for b in range(NUM_BUCKETS):
            out = jnp.where(bucket == b, tab_ref[b, h], out)
        return out

    @pl.when(i == 0)
    def _():
        b0_ref[...] = bias_tile(qq - kk)
        b1_ref[...] = bias_tile(qq - kk + blk)

    qT = qT_ref[...]
    gT = jnp.dot(km_ref[0], qT, preferred_element_type=F32, precision=HIGHEST)
    nid = lax.broadcasted_iota(I32, (nb, blk), 0)
    own = jnp.full((nb, blk), i, I32)
    cnt = jnp.zeros((nb, blk), I32)
    for m in range(nb):
        row = gT[m:m + 1, :]
        beats = (row > gT) | ((row == gT) & (m < nid))
        cnt = cnt + jnp.where(beats & (m < own), 1, 0)
    sel = (cnt < MOBA_TOPK) & (nid < own)
    selb_ref[...] = jnp.where(sel, 0.0, NEG)

    qb = qT.astype(MXU_DTYPE)
    s = jnp.dot(k_ref[0, i], qb, preferred_element_type=F32) + b0_ref[...]
    s = jnp.where(qq >= kk, s, NEG)
    m0 = jnp.max(s, axis=0, keepdims=True)
    p = jnp.exp(s - m0)
    l0 = jnp.sum(p, axis=0, keepdims=True)
    acc0 = jnp.dot(vT_ref[i], p.astype(MXU_DTYPE), preferred_element_type=F32)

    def prev_block(carry):
        n = i - 1
        s = (jnp.dot(k_ref[0, n], qb, preferred_element_type=F32) + b1_ref[...]
             + selb_ref[pl.ds(n, 1), :])
        return _softmax_update(carry, s, vT_ref[n])

    carry = lax.cond(i >= 1, prev_block, lambda c: c, (m0, l0, acc0))
    c_far = tab_ref[NUM_BUCKETS - 1, h]

    def far_block(n, carry):
        s = jnp.dot(k_ref[0, n], qb, preferred_element_type=F32) + (selb_ref[pl.ds(n, 1), :] + c_far)
        return _softmax_update(carry, s, vT_ref[n])

    _, l, acc = lax.fori_loop(0, i - 1, far_block, carry)
    o_ref[...] = (acc / l).astype(o_ref.dtype)


def _moba_prompt(rel_bias, qT, khm, vT, kmean, *, batch, seq):
    nb = seq // MOBA_BLOCK
    t = batch * seq
    return pl.pallas_call(
        _moba_prompt_kernel,
        grid=(batch, N_HEADS, nb),
        in_specs=[
            pl.BlockSpec(memory_space=pltpu.SMEM),
            pl.BlockSpec((HEAD_DIM, MOBA_BLOCK), lambda b, h, i: (h, b * nb + i)),
            pl.BlockSpec((1, nb, MOBA_BLOCK, HEAD_DIM), lambda b, h, i: (h, b, 0, 0)),
            pl.BlockSpec((nb, HEAD_DIM, MOBA_BLOCK), lambda b, h, i: (b, h, 0)),
            pl.BlockSpec((1, nb, HEAD_DIM), lambda b, h, i: (b * N_HEADS + h, 0, 0)),
        ],
        out_specs=pl.BlockSpec((HEAD_DIM, MOBA_BLOCK), lambda b, h, i: (h, b * nb + i)),
        out_shape=jax.ShapeDtypeStruct((ATT_W, t), MXU_DTYPE),
        scratch_shapes=[pltpu.VMEM((nb, MOBA_BLOCK), F32),
                        pltpu.VMEM((MOBA_BLOCK, MOBA_BLOCK), F32),
                        pltpu.VMEM((MOBA_BLOCK, MOBA_BLOCK), F32)],
        compiler_params=pltpu.CompilerParams(dimension_semantics=("arbitrary",) * 3,
                                             vmem_limit_bytes=VMEM_LIMIT),
        name="moba_prompt",
    )(rel_bias, qT, khm, vT, kmean)


def _moba_sample_kernel(pt_ref, q_ref, kn_ref, vn_ref, tab_ref, *refs, tq, nb_past, cb):
    del pt_ref
    npg = cb * PAGES_PER_BLOCK
    k_pages = refs[:npg]
    v_pages = refs[npg:2 * npg]
    o_ref, km_ref, m_ref, l_ref, acc_ref = refs[2 * npg:]
    c = pl.program_id(1)
    n_chunks = nb_past // cb
    rows = N_HEADS * tq
    width = q_ref.shape[1]
    r_id = lax.broadcasted_iota(I32, (rows, width), 0)
    lane = lax.broadcasted_iota(I32, (rows, width), 1)
    head_mask = (lane // HEAD_DIM) == (r_id // tq)
    q_rows = jnp.concatenate([q_ref[...]] * N_HEADS, axis=0)
    qbd = jnp.where(head_mask, q_rows, 0.0)
    qbd_b = qbd.astype(MXU_DTYPE)
    tab = tab_ref[...]

    def bias_rows(dist):
        bucket = _bucket(dist)
        out = jnp.zeros(dist.shape, F32)
        for b in range(NUM_BUCKETS):
            out = jnp.where(bucket == b, tab[:, b:b + 1], out)
        return out

    c_far = tab[:, NUM_BUCKETS - 1:NUM_BUCKETS]
    for j in range(cb):
        n = c * cb + j
        kb = jnp.concatenate([k_pages[PAGES_PER_BLOCK * j + g][0] for g in range(PAGES_PER_BLOCK)], axis=0)
        vb = jnp.concatenate([v_pages[PAGES_PER_BLOCK * j + g][0] for g in range(PAGES_PER_BLOCK)], axis=0)
        km_ref[pl.ds(n, 1), :] = jnp.sum(kb, axis=0, keepdims=True) * (1.0 / MOBA_BLOCK)
        s = lax.dot_general(qbd_b, kb.astype(MXU_DTYPE), _nt_dims(), preferred_element_type=F32)
        if j == cb - 1:
            qi = lax.broadcasted_iota(I32, (rows, MOBA_BLOCK), 0) % tq
            kj = lax.broadcasted_iota(I32, (rows, MOBA_BLOCK), 1)
            near = bias_rows(MOBA_BLOCK + qi - kj)
            s = s + jnp.where(c == n_chunks - 1, near, c_far)
        else:
            s = s + c_far
        m_n = jnp.max(s, axis=1, keepdims=True)
        p = jnp.exp(s - m_n)
        m_ref[n] = jnp.broadcast_to(m_n, m_ref.shape[1:])
        l_ref[n] = jnp.broadcast_to(jnp.sum(p, axis=1, keepdims=True), l_ref.shape[1:])
        acc_ref[n] = jnp.dot(p.astype(MXU_DTYPE), vb.astype(MXU_DTYPE), preferred_element_type=F32)

    @pl.when(c == n_chunks - 1)
    def _():
        gate = lax.dot_general(qbd, km_ref[...], _nt_dims(), preferred_element_type=F32,
                               precision=HIGHEST)
        nid = lax.broadcasted_iota(I32, (rows, nb_past), 1)
        cnt = jnp.zeros((rows, nb_past), I32)
        for n in range(nb_past):
            col = gate[:, n:n + 1]
            beats = (col > gate) | ((col == gate) & (n < nid))
            cnt = cnt + jnp.where(beats, 1, 0)
        sel = cnt < MOBA_TOPK
        pad = jnp.zeros((PAGE_SIZE - tq, width), F32)
        kn = jnp.concatenate([kn_ref[...], pad], axis=0)
        vn = jnp.concatenate([vn_ref[...], pad], axis=0)
        qi = lax.broadcasted_iota(I32, (rows, PAGE_SIZE), 0) % tq
        kj = lax.broadcasted_iota(I32, (rows, PAGE_SIZE), 1)
        s = lax.dot_general(qbd_b, kn.astype(MXU_DTYPE), _nt_dims(), preferred_element_type=F32)
        s = jnp.where(kj <= qi, s + bias_rows(qi - kj), NEG)
        m_o = jnp.max(s, axis=1, keepdims=True)
        p = jnp.exp(s - m_o)
        l_o = jnp.sum(p, axis=1, keepdims=True)
        acc_o = jnp.dot(p.astype(MXU_DTYPE), vn.astype(MXU_DTYPE), preferred_element_type=F32)
        m_all = m_o
        for n in range(nb_past):
            m_all = jnp.maximum(m_all, jnp.where(sel[:, n:n + 1], m_ref[n][:, 0:1], NEG))
        w_o = jnp.exp(m_o - m_all)
        l_all = w_o * l_o
        acc_all = w_o * acc_o
        for n in range(nb_past):
            w_n = jnp.where(sel[:, n:n + 1], jnp.exp(m_ref[n][:, 0:1] - m_all), 0.0)
            l_all = l_all + w_n * l_ref[n][:, 0:1]
            acc_all = acc_all + w_n * acc_ref[n]
        out_bd = jnp.where(head_mask, acc_all / l_all, 0.0)
        out = out_bd[0:tq]
        for h in range(1, N_HEADS):
            out = out + out_bd[h * tq:(h + 1) * tq]
        o_ref[...] = out


def _moba_sample(page_rows, q, k_new, v_new, tab_rows, cache_k, cache_v, *, nseq, tq, nb_past):
    cb = math.gcd(SAMPLE_CHUNK_BLOCKS, nb_past)
    npg = cb * PAGES_PER_BLOCK
    n_chunks = nb_past // cb
    ppseq = nb_past * PAGES_PER_BLOCK
    rows = N_HEADS * tq
    width = q.shape[1]
    tok = lambda b, c, pt: (b, 0)

    def page_spec(g):
        return pl.BlockSpec((1, PAGE_SIZE, width), lambda b, c, pt: (pt[b * ppseq + c * npg + g], 0, 0))

    grid_spec = pltpu.PrefetchScalarGridSpec(
        num_scalar_prefetch=1,
        grid=(nseq, n_chunks),
        in_specs=[pl.BlockSpec((tq, width), tok)] * 3
        + [pl.BlockSpec(tab_rows.shape, lambda b, c, pt: (0, 0))]
        + [page_spec(g) for g in range(npg)] * 2,
        out_specs=pl.BlockSpec((tq, width), tok),
        scratch_shapes=[pltpu.VMEM((nb_past, width), F32),
                        pltpu.VMEM((nb_past, rows, 128), F32),
                        pltpu.VMEM((nb_past, rows, 128), F32),
                        pltpu.VMEM((nb_past, rows, width), F32)],
    )
    return pl.pallas_call(
        functools.partial(_moba_sample_kernel, tq=tq, nb_past=nb_past, cb=cb),
        grid_spec=grid_spec,
        out_shape=jax.ShapeDtypeStruct((nseq * tq, width), F32),
        compiler_params=pltpu.CompilerParams(dimension_semantics=("arbitrary",) * 2,
                                             vmem_limit_bytes=VMEM_LIMIT),
        name="moba_sample",
    )(page_rows, q, k_new, v_new, tab_rows, *([cache_k] * npg), *([cache_v] * npg))


def _mix_kernel(*refs, tm, seq_len, attn_transposed, has_state, alpha):
    if has_state:
        (x_ref, attn_ref, u_ref, halo_ref, st0_ref, st1_ref, cb_ref, ga_ref, gb_ref, cw_ref,
         wao_ref, wco_ref, wo_ref, g_ref, b_ref, x1_ref, x1b_ref, ubuf) = refs
    else:
        (x_ref, attn_ref, u_ref, halo_ref, cb_ref, ga_ref, gb_ref, cw_ref,
         wao_ref, wco_ref, wo_ref, g_ref, b_ref, x1_ref, x1b_ref, ubuf) = refs
    i = pl.program_id(0)
    if attn_transposed:
        attn = attn_ref[...].astype(F32).T.astype(MXU_DTYPE)
    else:
        attn = attn_ref[...].astype(MXU_DTYPE)
    y_att = jnp.dot(attn, wao_ref[...], preferred_element_type=F32)
    u = u_ref[...]
    ubuf[0:SUBLANES, :] = halo_ref[...]
    ubuf[SUBLANES:SUBLANES + tm, :] = u
    p1 = ubuf[SUBLANES - 1:SUBLANES - 1 + tm, :]
    p2 = ubuf[SUBLANES - 2:SUBLANES - 2 + tm, :]
    row = lax.broadcasted_iota(I32, (tm, 1), 0)
    if seq_len % tm == 0:
        pos = (i * tm) % seq_len + row
    else:
        pos = row % seq_len
    s0 = st0_ref[...] if has_state else 0.0
    s1 = st1_ref[...] if has_state else 0.0
    prev1 = jnp.where(pos >= 1, p1, s1)
    prev2 = jnp.where(pos >= 2, p2, jnp.where(pos == 1, s1, s0))
    conv = cw_ref[0:1, :] * prev2 + cw_ref[1:2, :] * prev1 + cw_ref[2:3, :] * u
    y_conv = jnp.dot((cb_ref[...].astype(F32) * conv).astype(MXU_DTYPE), wco_ref[...],
                     preferred_element_type=F32)
    merged = ga_ref[...].astype(F32) * y_att + gb_ref[...].astype(F32) * y_conv
    z = alpha * x_ref[...] + jnp.dot(merged.astype(MXU_DTYPE), wo_ref[...], preferred_element_type=F32)
    x1 = _layer_norm(z, g_ref[...], b_ref[...])
    x1_ref[...] = x1
    x1b_ref[...] = x1.astype(x1b_ref.dtype)


def _mix(x2d, attn, u, state, cb, ga, gb, conv_w, wao, wco, wo, g, b, *, tm, seq_len, attn_transposed, alpha):
    t, d = x2d.shape
    cw = u.shape[1]
    assert seq_len % tm == 0 or tm % seq_len == 0
    nt = t // tm
    row = lambda i: (i, 0)
    full = lambda a: pl.BlockSpec(a.shape, lambda i: (0,) * a.ndim)
    has_state = state is not None
    hb = tm // SUBLANES
    attn_spec = (pl.BlockSpec((ATT_W, tm), lambda i: (0, i)) if attn_transposed
                 else pl.BlockSpec((tm, ATT_W), row))
    args = [x2d, attn, u, u]
    in_specs = [pl.BlockSpec((tm, d), row), attn_spec, pl.BlockSpec((tm, cw), row),
                pl.BlockSpec((SUBLANES, cw), lambda i: (jnp.maximum(i * hb - 1, 0), 0))]
    if has_state:
        args += list(state)
        in_specs += [pl.BlockSpec((tm, cw), row)] * 2
    args += [cb, ga, gb, conv_w, wao, wco, wo, g, b]
    in_specs += [pl.BlockSpec((tm, cw), row), pl.BlockSpec((tm, d), row), pl.BlockSpec((tm, d), row),
                 full(conv_w), full(wao), full(wco), full(wo), full(g), full(b)]
    return pl.pallas_call(
        functools.partial(_mix_kernel, tm=tm, seq_len=seq_len, attn_transposed=attn_transposed,
                          has_state=has_state, alpha=alpha),
        grid=(nt,),
        in_specs=in_specs,
        out_specs=[pl.BlockSpec((tm, d), row), pl.BlockSpec((tm, d), row)],
        out_shape=[jax.ShapeDtypeStruct((t, d), F32), jax.ShapeDtypeStruct((t, d), MXU_DTYPE)],
        scratch_shapes=[pltpu.VMEM((SUBLANES + tm, cw), F32)],
        compiler_params=pltpu.CompilerParams(dimension_semantics=("arbitrary",),
                                             vmem_limit_bytes=VMEM_LIMIT),
        name="mix",
    )(*args)


def _router_kernel(x_ref, wT_ref, b_ref, idx_ref, gate_ref, rank_ref, cnt_ref, seen_ref):
    i = pl.program_id(0)
    ne, tm = wT_ref.shape[0], x_ref.shape[0]

    @pl.when(i == 0)
    def _():
        seen_ref[...] = jnp.zeros(seen_ref.shape, F32)

    logits = lax.dot_general(wT_ref[...], x_ref[...], _nt_dims(), preferred_element_type=F32,
                             precision=HIGHEST) + b_ref[...]
    eid = lax.broadcasted_iota(I32, (ne, tm), 0)
    cnt = jnp.zeros((ne, tm), I32)
    for e in range(ne):
        row = logits[e:e + 1, :]
        beats = (row > logits) | ((row == logits) & (e < eid))
        cnt = cnt + jnp.where(beats, 1, 0)
    sel = cnt < TOP_K
    lmax = jnp.max(logits, axis=0, keepdims=True)
    ex = jnp.where(sel, jnp.exp(logits - lmax), 0.0)
    gates = ex / jnp.sum(ex, axis=0, keepdims=True)
    self_ = jnp.where(sel, 1.0, 0.0)
    tt = lax.broadcasted_iota(I32, (tm, tm), 0) < lax.broadcasted_iota(I32, (tm, tm), 1)
    earlier = jnp.where(tt, 1.0, 0.0).astype(MXU_DTYPE)
    seen = seen_ref[:, 0:1]
    rank = jnp.dot(self_.astype(MXU_DTYPE), earlier, preferred_element_type=F32) + seen
    seen = seen + jnp.sum(self_, axis=1, keepdims=True)
    seen_ref[...] = jnp.broadcast_to(seen, seen_ref.shape)
    cnt_ref[...] = jnp.broadcast_to(seen, cnt_ref.shape)
    ee = lax.broadcasted_iota(I32, (ne, ne), 1) < lax.broadcasted_iota(I32, (ne, ne), 0)
    below = jnp.dot(jnp.where(ee, 1.0, 0.0).astype(MXU_DTYPE), self_.astype(MXU_DTYPE),
                    preferred_element_type=F32)
    eidf = eid.astype(F32)
    for kk in range(TOP_K):
        hit = sel & (below == float(kk))
        idx_ref[kk:kk + 1, :] = jnp.sum(jnp.where(hit, eidf, 0.0), axis=0, keepdims=True).astype(I32)
        gate_ref[kk:kk + 1, :] = jnp.sum(jnp.where(hit, gates, 0.0), axis=0, keepdims=True)
        rank_ref[kk:kk + 1, :] = jnp.sum(jnp.where(hit, rank, 0.0), axis=0, keepdims=True).astype(I32)


def _route(x1, router_wT, router_b_col, *, tm):
    t, d = x1.shape
    ne = router_wT.shape[0]
    nt = t // tm
    col = lambda i: (0, i)
    return pl.pallas_call(
        _router_kernel,
        grid=(nt,),
        in_specs=[pl.BlockSpec((tm, d), lambda i: (i, 0)),
                  pl.BlockSpec((ne, d), lambda i: (0, 0)),
                  pl.BlockSpec((ne, 1), lambda i: (0, 0))],
        out_specs=[pl.BlockSpec((TOP_K, tm), col), pl.BlockSpec((TOP_K, tm), col),
                   pl.BlockSpec((TOP_K, tm), col), pl.BlockSpec((ne, 128), lambda i: (0, 0))],
        out_shape=[jax.ShapeDtypeStruct((TOP_K, t), I32), jax.ShapeDtypeStruct((TOP_K, t), F32),
                   jax.ShapeDtypeStruct((TOP_K, t), I32), jax.ShapeDtypeStruct((ne, 128), F32)],
        scratch_shapes=[pltpu.VMEM((ne, 128), F32)],
        compiler_params=pltpu.CompilerParams(dimension_semantics=("arbitrary",),
                                             vmem_limit_bytes=VMEM_LIMIT),
        name="router",
    )(x1, router_wT, router_b_col)


def _expert_kernel(te_ref, nu_ref, x_ref, wg_ref, wu_ref, bg_ref, bu_ref, wd_ref, bd_ref, y_ref):
    del te_ref
    i = pl.program_id(0)

    @pl.when(i < nu_ref[0])
    def _():
        x = x_ref[...]
        hg = jnp.dot(x, wg_ref[0], preferred_element_type=F32) + bg_ref[0]
        hu = jnp.dot(x, wu_ref[0], preferred_element_type=F32) + bu_ref[0]
        g = jnp.minimum(hg, SWIGLU_LIMIT)
        u = jnp.clip(hu, -SWIGLU_LIMIT, SWIGLU_LIMIT)
        a = g * _sigmoid(SWIGLU_ALPHA * g) * (u + 1.0)
        y_ref[...] = jnp.dot(a.astype(MXU_DTYPE), wd_ref[0], preferred_element_type=F32) + bd_ref[0]

    @pl.when(i >= nu_ref[0])
    def _():
        y_ref[...] = jnp.zeros(y_ref.shape, F32)


def _experts(tile_e, n_used, xb, wg, wu, bg, bu, wd, bd, *, tm):
    rows, d = xb.shape
    dff = wg.shape[2]
    nt = rows // tm
    ew = lambda i, te, nu: (te[i], 0, 0)
    grid_spec = pltpu.PrefetchScalarGridSpec(
        num_scalar_prefetch=2,
        grid=(nt,),
        in_specs=[pl.BlockSpec((tm, d), lambda i, te, nu: (i, 0)),
                  pl.BlockSpec((1, d, dff), ew), pl.BlockSpec((1, d, dff), ew),
                  pl.BlockSpec((1, 1, dff), ew), pl.BlockSpec((1, 1, dff), ew),
                  pl.BlockSpec((1, dff, d), ew), pl.BlockSpec((1, 1, d), ew)],
        out_specs=pl.BlockSpec((tm, d), lambda i, te, nu: (i, 0)),
    )
    return pl.pallas_call(
        _expert_kernel,
        grid_spec=grid_spec,
        out_shape=jax.ShapeDtypeStruct((rows, d), F32),
        compiler_params=pltpu.CompilerParams(dimension_semantics=("arbitrary",),
                                             vmem_limit_bytes=VMEM_LIMIT),
        name="experts",
    )(tile_e, n_used, xb, wg, wu, bg, bu, wd, bd)


def _norm2_kernel(x1_ref, f_ref, g_ref, b_ref, y_ref, *, alpha):
    y_ref[...] = _layer_norm(alpha * x1_ref[...] + f_ref[...], g_ref[...], b_ref[...])


def _norm2(x1_all, ffn_all, g, b, *, row0, nrows, tm, alpha):
    d = x1_all.shape[1]
    off = row0 // tm
    src = lambda i: (i + off, 0)
    return pl.pallas_call(
        functools.partial(_norm2_kernel, alpha=alpha),
        grid=(nrows // tm,),
        in_specs=[pl.BlockSpec((tm, d), src), pl.BlockSpec((tm, d), src),
                  pl.BlockSpec((1, d), lambda i: (0, 0)), pl.BlockSpec((1, d), lambda i: (0, 0))],
        out_specs=pl.BlockSpec((tm, d), lambda i: (i, 0)),
        out_shape=jax.ShapeDtypeStruct((nrows, d), F32),
        compiler_params=pltpu.CompilerParams(dimension_semantics=("arbitrary",),
                                             vmem_limit_bytes=VMEM_LIMIT),
        name="norm2",
    )(x1_all, ffn_all, g, b)


def _moe(x1_all, x1b_all, router_w, router_b, w_gate_up, b_gate_up, w_down, b_down):
    t, d = x1_all.shape
    ne = router_w.shape[1]
    idx, gate, rank, counts = _route(x1_all, router_w.T, router_b.reshape(ne, 1), tm=TOKEN_TILE)
    tm = EXPERT_TILE
    counts = counts[:, 0].astype(I32)
    padded = (counts + tm - 1) // tm * tm
    pad_end = jnp.cumsum(padded)
    pad_start = pad_end - padded
    n_tiles = (t * TOP_K + ne * (tm - 1)) // tm
    rows = n_tiles * tm
    dest = pad_start[idx] + rank
    tok = jnp.broadcast_to(jnp.arange(t, dtype=I32)[None, :], dest.shape)
    row_tok = jnp.zeros((rows,), I32).at[dest.reshape(-1)].set(tok.reshape(-1), unique_indices=True)
    tile_e = jnp.minimum(jnp.searchsorted(pad_end, jnp.arange(n_tiles, dtype=I32) * tm, side="right"),
                         ne - 1).astype(I32)
    n_used = (pad_end[-1:] // tm).astype(I32)
    xb = x1b_all[row_tok]
    wg = w_gate_up[:, :, 0::2].astype(MXU_DTYPE)
    wu = w_gate_up[:, :, 1::2].astype(MXU_DTYPE)
    bg = b_gate_up[:, None, 0::2]
    bu = b_gate_up[:, None, 1::2]
    yb = _experts(tile_e, n_used, xb, wg, wu, bg, bu, w_down.astype(MXU_DTYPE), b_down[:, None, :], tm=tm)
    ffn = gate[0][:, None] * yb[dest[0]]
    for kk in range(1, TOP_K):
        ffn = ffn + gate[kk][:, None] * yb[dest[kk]]
    return ffn


def kernel(x_prompt, x_sample, cache_k, cache_v, state_conv, page_table, rel_bias, w_in, conv_w, w_att_o,
           w_conv_o, w_o, ln1_g, ln1_b, router_w, router_b, w_gate_up, b_gate_up, w_down, b_down, ln2_g, ln2_b):
    depth = w_in.shape[0]
    batch, seq, d = x_prompt.shape
    nseq, tq, _ = x_sample.shape
    n_pool = cache_k.shape[1]
    ppseq = page_table.shape[1]
    past = ppseq * PAGE_SIZE
    assert seq % MOBA_BLOCK == 0 and past % MOBA_BLOCK == 0 and tq % SUBLANES == 0 and tq <= PAGE_SIZE
    nb_past = past // MOBA_BLOCK
    tp, ts = batch * seq, nseq * tq
    alpha = (2 * depth) ** 0.25
    cw = conv_w.shape[2]
    tab_rows = jnp.repeat(rel_bias.T, tq, axis=0)
    ck = cache_k.reshape(depth * n_pool, PAGE_SIZE, ATT_W)
    cv = cache_v.reshape(depth * n_pool, PAGE_SIZE, ATT_W)
    hp = x_prompt.reshape(tp, d)
    hs = x_sample.reshape(ts, d)
    outs = [[] for _ in range(6)]
    for l in range(depth):
        w_in_b = w_in[l].astype(MXU_DTYPE)
        wao, wco, wo = (w_att_o[l].astype(MXU_DTYPE), w_conv_o[l].astype(MXU_DTYPE), w_o[l].astype(MXU_DTYPE))
        g1, b1, g2, b2 = ln1_g[l][None], ln1_b[l][None], ln2_g[l][None], ln2_b[l][None]
        qT, khm, vT, kp, vp, km, up, cbp, gap, gbp = _project(hp, w_in_b, tm=PROJ_TILE, head_major=True)
        nbt = PROJ_TILE // MOBA_BLOCK
        kmean = km[:, :nbt].reshape(batch, seq // MOBA_BLOCK, N_HEADS, HEAD_DIM).transpose(0, 2, 1, 3)
        kmean = kmean.reshape(batch * N_HEADS, seq // MOBA_BLOCK, HEAD_DIM)
        attn_p = _moba_prompt(rel_bias, qT, khm, vT, kmean, batch=batch, seq=seq)
        x1p, x1bp = _mix(hp, attn_p, up, None, cbp, gap, gbp, conv_w[l], wao, wco, wo, g1, b1,
                         tm=PROJ_TILE, seq_len=seq, attn_transposed=True, alpha=alpha)
        qs, ks, vs, us, cbs, gas, gbs = _project(hs, w_in_b, tm=ts, head_major=False)
        pages = (page_table + l * n_pool).reshape(-1).astype(I32)
        attn_s = _moba_sample(pages, qs, ks, vs, tab_rows, ck, cv, nseq=nseq, tq=tq, nb_past=nb_past)
        state = (jnp.repeat(state_conv[l][:, 0], tq, axis=0), jnp.repeat(state_conv[l][:, 1], tq, axis=0))
        x1s, x1bs = _mix(hs, attn_s, us, state, cbs, gas, gbs, conv_w[l], wao, wco, wo, g1, b1,
                         tm=ts, seq_len=tq, attn_transposed=False, alpha=alpha)
        x1_all = jnp.concatenate([x1p, x1s], axis=0)
        x1b_all = jnp.concatenate([x1bp, x1bs], axis=0)
        ffn = _moe(x1_all, x1b_all, router_w[l], router_b[l], w_gate_up[l], b_gate_up[l], w_down[l], b_down[l])
        hp = _norm2(x1_all, ffn, g2, b2, row0=0, nrows=tp, tm=TOKEN_TILE, alpha=alpha)
        hs = _norm2(x1_all, ffn, g2, b2, row0=tp, nrows=ts, tm=TOKEN_TILE, alpha=alpha)
        outs[0].append(kp.reshape(batch, seq, N_HEADS, HEAD_DIM))
        outs[1].append(vp.reshape(batch, seq, N_HEADS, HEAD_DIM))
        outs[2].append(up.reshape(batch, seq, cw)[:, seq - (CONV_K - 1):])
        outs[3].append(ks.reshape(nseq, tq, N_HEADS, HEAD_DIM))
        outs[4].append(vs.reshape(nseq, tq, N_HEADS, HEAD_DIM))
        outs[5].append(us.reshape(nseq, tq, cw)[:, tq - (CONV_K - 1):])
    stacked = [jnp.stack(o) for o in outs]
    return (hp.reshape(batch, seq, d), hs.reshape(nseq, tq, d), *stacked)
```

```python
import functools
import math

import jax
import jax.numpy as jnp
from jax import lax
from jax.experimental import pallas as pl
from jax.experimental.pallas import tpu as pltpu

F32 = jnp.float32
I32 = jnp.int32
MXU_DTYPE = jnp.bfloat16
HIGHEST = lax.Precision.HIGHEST

N_HEADS = 8
HEAD_DIM = 64
ATT_W = N_HEADS * HEAD_DIM
MOBA_BLOCK = 256
MOBA_TOPK = 3
PAGE_SIZE = 128
PAGES_PER_BLOCK = MOBA_BLOCK // PAGE_SIZE
CONV_K = 3
N_EXPERTS = 32
TOP_K = 4
SWIGLU_LIMIT = 7.0
SWIGLU_ALPHA = 1.702
NUM_BUCKETS = 32
NUM_EXACT = NUM_BUCKETS // 2
MAX_DISTANCE = 128
LN_EPS = 1e-5
SCORE_SCALE = HEAD_DIM ** -0.5
NEG = -1e30

SUBLANES = 8
PROJ_TILE = 256
TOKEN_TILE = 256
EXPERT_TILE = 512
SPLIT_TILE = 512
SAMPLE_CHUNK_BLOCKS = 8
PROMPT_GROUP_BLOCKS = 4
VMEM_LIMIT = 56 * 1024 * 1024

assert MOBA_BLOCK >= MAX_DISTANCE


def _nt_dims():
    return (((1,), (1,)), ((), ()))


def _sigmoid(x):
    return 1.0 / (1.0 + jnp.exp(-x))


def _bucket(dist):
    n = jnp.maximum(dist, 0)
    nf = jnp.maximum(n, 1).astype(F32)
    large = NUM_EXACT + (jnp.log(nf / NUM_EXACT) / math.log(MAX_DISTANCE / NUM_EXACT)
                         * (NUM_BUCKETS - NUM_EXACT)).astype(I32)
    large = jnp.minimum(large, NUM_BUCKETS - 1)
    return jnp.where(n < NUM_EXACT, n, large)


def _layer_norm(z, g, b):
    mu = jnp.mean(z, axis=-1, keepdims=True)
    zc = z - mu
    var = jnp.mean(zc * zc, axis=-1, keepdims=True)
    return zc * lax.rsqrt(var + LN_EPS) * g + b


def _proj_kernel(x_ref, w_ref, *out_refs, tm, head_major):
    xb = x_ref[...].astype(MXU_DTYPE)

    def mm(c0, width):
        return jnp.dot(xb, w_ref[:, c0:c0 + width], preferred_element_type=F32)

    if head_major:
        qT_ref, khm_ref, vT_ref, k_ref, v_ref, km_ref, u_ref, cb_ref, ga_ref, gb_ref = out_refs
    else:
        q_ref, k_ref, v_ref, u_ref, cb_ref, ga_ref, gb_ref = out_refs
    d_model = x_ref.shape[1]
    q = mm(0, ATT_W) * SCORE_SCALE
    k = mm(ATT_W, ATT_W)
    v = mm(2 * ATT_W, ATT_W)
    k_ref[...] = k
    v_ref[...] = v
    if head_major:
        qT_ref[...] = q.T
        km_ref[...] = jnp.zeros(km_ref.shape, F32)
        for r in range(tm // MOBA_BLOCK):
            kr = k[r * MOBA_BLOCK:(r + 1) * MOBA_BLOCK]
            km_ref[0, r:r + 1, :] = jnp.sum(kr, axis=0, keepdims=True) * (1.0 / MOBA_BLOCK)
            vT_ref[r] = v[r * MOBA_BLOCK:(r + 1) * MOBA_BLOCK].T.astype(MXU_DTYPE)
            for h in range(N_HEADS):
                khm_ref[h, r] = kr[:, h * HEAD_DIM:(h + 1) * HEAD_DIM].astype(MXU_DTYPE)
    else:
        q_ref[...] = q
    c0 = 3 * ATT_W
    cw = u_ref.shape[1]
    cb_ref[...] = mm(c0, cw).astype(cb_ref.dtype)
    u_ref[...] = mm(c0 + cw, cw) * mm(c0 + 2 * cw, cw)
    ga_ref[...] = _sigmoid(mm(c0 + 3 * cw, d_model)).astype(ga_ref.dtype)
    gb_ref[...] = _sigmoid(mm(c0 + 3 * cw + d_model, d_model)).astype(gb_ref.dtype)


def _project(x2d, w_in_b, *, tm, head_major):
    t, d = x2d.shape
    cw = (w_in_b.shape[1] - 3 * ATT_W - 2 * d) // 3
    nt = t // tm
    row = lambda i: (i, 0)
    f32s = lambda shape: jax.ShapeDtypeStruct(shape, F32)
    mxs = lambda shape: jax.ShapeDtypeStruct(shape, MXU_DTYPE)
    tail_shapes = [f32s((t, cw)), mxs((t, cw)), mxs((t, d)), mxs((t, d))]
    tail_specs = [pl.BlockSpec((tm, cw), row), pl.BlockSpec((tm, cw), row),
                  pl.BlockSpec((tm, d), row), pl.BlockSpec((tm, d), row)]
    if head_major:
        nbt = tm // MOBA_BLOCK
        out_shape = [f32s((ATT_W, t)), mxs((N_HEADS, t // MOBA_BLOCK, MOBA_BLOCK, HEAD_DIM)),
                     mxs((t // MOBA_BLOCK, ATT_W, MOBA_BLOCK)), f32s((t, ATT_W)), f32s((t, ATT_W)),
                     f32s((nt, SUBLANES, ATT_W))] + tail_shapes
        out_specs = [pl.BlockSpec((ATT_W, tm), lambda i: (0, i)),
                     pl.BlockSpec((N_HEADS, nbt, MOBA_BLOCK, HEAD_DIM), lambda i: (0, i, 0, 0)),
                     pl.BlockSpec((nbt, ATT_W, MOBA_BLOCK), lambda i: (i, 0, 0)),
                     pl.BlockSpec((tm, ATT_W), row), pl.BlockSpec((tm, ATT_W), row),
                     pl.BlockSpec((1, SUBLANES, ATT_W), lambda i: (i, 0, 0))] + tail_specs
    else:
        out_shape = [f32s((t, ATT_W))] * 3 + tail_shapes
        out_specs = [pl.BlockSpec((tm, ATT_W), row)] * 3 + tail_specs
    return pl.pallas_call(
        functools.partial(_proj_kernel, tm=tm, head_major=head_major),
        grid=(nt,),
        in_specs=[pl.BlockSpec((tm, d), row), pl.BlockSpec(w_in_b.shape, lambda i: (0, 0))],
        out_specs=out_specs,
        out_shape=out_shape,
        compiler_params=pltpu.CompilerParams(dimension_semantics=("arbitrary",),
                                             vmem_limit_bytes=VMEM_LIMIT),
        name="proj",
    )(x2d, w_in_b)


def _moba_prompt_kernel(tab_ref, qT_ref, k_ref, vT_ref, km_ref, o_ref, selb_ref, bias_ref):
    h = pl.program_id(1)
    i = pl.program_id(2)
    nb = km_ref.shape[1]
    blk = MOBA_BLOCK
    grp = PROMPT_GROUP_BLOCKS

    @pl.when(i == 0)
    def _():
        kk = lax.broadcasted_iota(I32, (blk, blk), 0)
        qq = lax.broadcasted_iota(I32, (blk, blk), 1)
        for age in range(2):
            dist = qq - kk + age * blk
            bucket = _bucket(dist)
            tile = jnp.zeros((blk, blk), F32)
            for b in range(NUM_BUCKETS):
                tile = jnp.where(bucket == b, tab_ref[b, h], tile)
            bias_ref[age] = jnp.where(dist >= 0, tile, NEG)
        bias_ref[2] = jnp.full((blk, blk), tab_ref[NUM_BUCKETS - 1, h], F32)

    qT = qT_ref[...]
    gT = jnp.dot(km_ref[0], qT, preferred_element_type=F32, precision=HIGHEST)
    nid = lax.broadcasted_iota(I32, (nb, blk), 0)
    own = jnp.full((nb, blk), i, I32)
    cnt = jnp.zeros((nb, blk), I32)
    for m in range(nb):
        row = gT[m:m + 1, :]
        beats = (row > gT) | ((row == gT) & (m < nid))
        cnt = cnt + jnp.where(beats & (m < own), 1, 0)
    keep = ((cnt < MOBA_TOPK) & (nid < own)) | (nid == own)
    selb_ref[...] = jnp.where(keep, 0.0, NEG)

    qb = qT.astype(MXU_DTYPE)

    def group(g, carry):
        m, l, acc = carry
        scores, blocks = [], []
        for j in range(grp):
            n = i - grp * g - j
            nc = jnp.maximum(n, 0)
            age = jnp.minimum(i - nc, 2)
            rowb = selb_ref[pl.ds(nc, 1), :] + jnp.where(n >= 0, 0.0, NEG)
            scores.append(jnp.dot(k_ref[0, nc], qb, preferred_element_type=F32) + bias_ref[age] + rowb)
            blocks.append(nc)
        m_new = m
        for s in scores:
            m_new = jnp.maximum(m_new, jnp.max(s, axis=0, keepdims=True))
        alpha = jnp.exp(m - m_new)
        l = alpha * l
        acc = alpha * acc
        for s, nc in zip(scores, blocks):
            p = jnp.exp(s - m_new)
            l = l + jnp.sum(p, axis=0, keepdims=True)
            acc = acc + jnp.dot(vT_ref[nc], p.astype(MXU_DTYPE), preferred_element_type=F32)
        return m_new, l, acc

    init = (jnp.full((1, blk), NEG, F32), jnp.zeros((1, blk), F32), jnp.zeros((HEAD_DIM, blk), F32))
    _, l, acc = lax.fori_loop(0, (i + grp) // grp, group, init)
    o_ref[...] = (acc / l).astype(o_ref.dtype)


def _moba_prompt(rel_bias, qT, khm, vT, kmean, *, batch, seq):
    nb = seq // MOBA_BLOCK
    t = batch * seq
    return pl.pallas_call(
        _moba_prompt_kernel,
        grid=(batch, N_HEADS, nb),
        in_specs=[
            pl.BlockSpec(memory_space=pltpu.SMEM),
            pl.BlockSpec((HEAD_DIM, MOBA_BLOCK), lambda b, h, i: (h, b * nb + i)),
            pl.BlockSpec((1, nb, MOBA_BLOCK, HEAD_DIM), lambda b, h, i: (h, b, 0, 0)),
            pl.BlockSpec((nb, HEAD_DIM, MOBA_BLOCK), lambda b, h, i: (b, h, 0)),
            pl.BlockSpec((1, nb, HEAD_DIM), lambda b, h, i: (b * N_HEADS + h, 0, 0)),
        ],
        out_specs=pl.BlockSpec((HEAD_DIM, MOBA_BLOCK), lambda b, h, i: (h, b * nb + i)),
        out_shape=jax.ShapeDtypeStruct((ATT_W, t), MXU_DTYPE),
        scratch_shapes=[pltpu.VMEM((nb, MOBA_BLOCK), F32),
                        pltpu.VMEM((3, MOBA_BLOCK, MOBA_BLOCK), F32)],
        compiler_params=pltpu.CompilerParams(dimension_semantics=("arbitrary",) * 3,
                                             vmem_limit_bytes=VMEM_LIMIT),
        name="moba_prompt",
    )(rel_bias, qT, khm, vT, kmean)


def _moba_sample_kernel(pt_ref, q_ref, kn_ref, vn_ref, tab_ref, *refs, tq, nb_past, cb):
    del pt_ref
    npg = cb * PAGES_PER_BLOCK
    k_pages = refs[:npg]
    v_pages = refs[npg:2 * npg]
    o_ref, km_ref, m_ref, l_ref, acc_ref = refs[2 * npg:]
    c = pl.program_id(1)
    n_chunks = nb_past // cb
    rows = N_HEADS * tq
    width = q_ref.shape[1]
    r_id = lax.broadcasted_iota(I32, (rows, width), 0)
    lane = lax.broadcasted_iota(I32, (rows, width), 1)
    head_mask = (lane // HEAD_DIM) == (r_id // tq)
    q_rows = jnp.concatenate([q_ref[...]] * N_HEADS, axis=0)
    qbd = jnp.where(head_mask, q_rows, 0.0)
    qbd_b = qbd.astype(MXU_DTYPE)
    tab = tab_ref[...]

    def bias_rows(dist):
        bucket = _bucket(dist)
        out = jnp.zeros(dist.shape, F32)
        for b in range(NUM_BUCKETS):
            out = jnp.where(bucket == b, tab[:, b:b + 1], out)
        return out

    c_far = tab[:, NUM_BUCKETS - 1:NUM_BUCKETS]
    for j in range(cb):
        n = c * cb + j
        kb = jnp.concatenate([k_pages[PAGES_PER_BLOCK * j + g][0] for g in range(PAGES_PER_BLOCK)], axis=0)
        vb = jnp.concatenate([v_pages[PAGES_PER_BLOCK * j + g][0] for g in range(PAGES_PER_BLOCK)], axis=0)
        km_ref[pl.ds(n, 1), :] = jnp.sum(kb, axis=0, keepdims=True) * (1.0 / MOBA_BLOCK)
        s = lax.dot_general(qbd_b, kb.astype(MXU_DTYPE), _nt_dims(), preferred_element_type=F32)
        if j == cb - 1:
            qi = lax.broadcasted_iota(I32, (rows, MOBA_BLOCK), 0) % tq
            kj = lax.broadcasted_iota(I32, (rows, MOBA_BLOCK), 1)
            near = bias_rows(MOBA_BLOCK + qi - kj)
            s = s + jnp.where(c == n_chunks - 1, near, c_far)
        else:
            s = s + c_far
        m_n = jnp.max(s, axis=1, keepdims=True)
        p = jnp.exp(s - m_n)
        m_ref[n] = jnp.broadcast_to(m_n, m_ref.shape[1:])
        l_ref[n] = jnp.broadcast_to(jnp.sum(p, axis=1, keepdims=True), l_ref.shape[1:])
        acc_ref[n] = jnp.dot(p.astype(MXU_DTYPE), vb.astype(MXU_DTYPE), preferred_element_type=F32)

    @pl.when(c == n_chunks - 1)
    def _():
        gate = lax.dot_general(qbd, km_ref[...], _nt_dims(), preferred_element_type=F32,
                               precision=HIGHEST)
        nid = lax.broadcasted_iota(I32, (rows, nb_past), 1)
        cnt = jnp.zeros((rows, nb_past), I32)
        for n in range(nb_past):
            col = gate[:, n:n + 1]
            beats = (col > gate) | ((col == gate) & (n < nid))
            cnt = cnt + jnp.where(beats, 1, 0)
        sel = cnt < MOBA_TOPK
        pad = jnp.zeros((PAGE_SIZE - tq, width), F32)
        kn = jnp.concatenate([kn_ref[...], pad], axis=0)
        vn = jnp.concatenate([vn_ref[...], pad], axis=0)
        qi = lax.broadcasted_iota(I32, (rows, PAGE_SIZE), 0) % tq
        kj = lax.broadcasted_iota(I32, (rows, PAGE_SIZE), 1)
        s = lax.dot_general(qbd_b, kn.astype(MXU_DTYPE), _nt_dims(), preferred_element_type=F32)
        s = jnp.where(kj <= qi, s + bias_rows(qi - kj), NEG)
        m_o = jnp.max(s, axis=1, keepdims=True)
        p = jnp.exp(s - m_o)
        l_o = jnp.sum(p, axis=1, keepdims=True)
        acc_o = jnp.dot(p.astype(MXU_DTYPE), vn.astype(MXU_DTYPE), preferred_element_type=F32)
        m_all = m_o
        for n in range(nb_past):
            m_all = jnp.maximum(m_all, jnp.where(sel[:, n:n + 1], m_ref[n][:, 0:1], NEG))
        w_o = jnp.exp(m_o - m_all)
        l_all = w_o * l_o
        acc_all = w_o * acc_o
        for n in range(nb_past):
            w_n = jnp.where(sel[:, n:n + 1], jnp.exp(m_ref[n][:, 0:1] - m_all), 0.0)
            l_all = l_all + w_n * l_ref[n][:, 0:1]
            acc_all = acc_all + w_n * acc_ref[n]
        out_bd = jnp.where(head_mask, acc_all / l_all, 0.0)
        out = out_bd[0:tq]
        for h in range(1, N_HEADS):
            out = out + out_bd[h * tq:(h + 1) * tq]
        o_ref[...] = out


def _moba_sample(page_rows, q, k_new, v_new, tab_rows, cache_k, cache_v, *, nseq, tq, nb_past):
    cb = math.gcd(SAMPLE_CHUNK_BLOCKS, nb_past)
    npg = cb * PAGES_PER_BLOCK
    n_chunks = nb_past // cb
    ppseq = nb_past * PAGES_PER_BLOCK
    rows = N_HEADS * tq
    width = q.shape[1]
    tok = lambda b, c, pt: (b, 0)

    def page_spec(g):
        return pl.BlockSpec((1, PAGE_SIZE, width), lambda b, c, pt: (pt[b * ppseq + c * npg + g], 0, 0))

    grid_spec = pltpu.PrefetchScalarGridSpec(
        num_scalar_prefetch=1,
        grid=(nseq, n_chunks),
        in_specs=[pl.BlockSpec((tq, width), tok)] * 3
        + [pl.BlockSpec(tab_rows.shape, lambda b, c, pt: (0, 0))]
        + [page_spec(g) for g in range(npg)] * 2,
        out_specs=pl.BlockSpec((tq, width), tok),
        scratch_shapes=[pltpu.VMEM((nb_past, width), F32),
                        pltpu.VMEM((nb_past, rows, 128), F32),
                        pltpu.VMEM((nb_past, rows, 128), F32),
                        pltpu.VMEM((nb_past, rows, width), F32)],
    )
    return pl.pallas_call(
        functools.partial(_moba_sample_kernel, tq=tq, nb_past=nb_past, cb=cb),
        grid_spec=grid_spec,
        out_shape=jax.ShapeDtypeStruct((nseq * tq, width), F32),
        compiler_params=pltpu.CompilerParams(dimension_semantics=("arbitrary",) * 2,
                                             vmem_limit_bytes=VMEM_LIMIT),
        name="moba_sample",
    )(page_rows, q, k_new, v_new, tab_rows, *([cache_k] * npg), *([cache_v] * npg))


def _mix_kernel(*refs, tm, seq_len, attn_transposed, has_state, alpha):
    if has_state:
        (x_ref, attn_ref, u_ref, halo_ref, st0_ref, st1_ref, cb_ref, ga_ref, gb_ref, cw_ref,
         wao_ref, wco_ref, wo_ref, g_ref, b_ref, x1_ref, x1b_ref, ubuf) = refs
    else:
        (x_ref, attn_ref, u_ref, halo_ref, cb_ref, ga_ref, gb_ref, cw_ref,
         wao_ref, wco_ref, wo_ref, g_ref, b_ref, x1_ref, x1b_ref, ubuf) = refs
    i = pl.program_id(0)
    if attn_transposed:
        attn = attn_ref[...].astype(F32).T.astype(MXU_DTYPE)
    else:
        attn = attn_ref[...].astype(MXU_DTYPE)
    y_att = jnp.dot(attn, wao_ref[...], preferred_element_type=F32)
    u = u_ref[...]
    ubuf[0:SUBLANES, :] = halo_ref[...]
    ubuf[SUBLANES:SUBLANES + tm, :] = u
    p1 = ubuf[SUBLANES - 1:SUBLANES - 1 + tm, :]
    p2 = ubuf[SUBLANES - 2:SUBLANES - 2 + tm, :]
    row = lax.broadcasted_iota(I32, (tm, 1), 0)
    if seq_len % tm == 0:
        pos = (i * tm) % seq_len + row
    else:
        pos = row % seq_len
    s0 = st0_ref[...] if has_state else 0.0
    s1 = st1_ref[...] if has_state else 0.0
    prev1 = jnp.where(pos >= 1, p1, s1)
    prev2 = jnp.where(pos >= 2, p2, jnp.where(pos == 1, s1, s0))
    conv = cw_ref[0:1, :] * prev2 + cw_ref[1:2, :] * prev1 + cw_ref[2:3, :] * u
    y_conv = jnp.dot((cb_ref[...].astype(F32) * conv).astype(MXU_DTYPE), wco_ref[...],
                     preferred_element_type=F32)
    merged = ga_ref[...].astype(F32) * y_att + gb_ref[...].astype(F32) * y_conv
    z = alpha * x_ref[...] + jnp.dot(merged.astype(MXU_DTYPE), wo_ref[...], preferred_element_type=F32)
    x1 = _layer_norm(z, g_ref[...], b_ref[...])
    x1_ref[...] = x1
    x1b_ref[...] = x1.astype(x1b_ref.dtype)


def _mix(x2d, attn, u, state, cb, ga, gb, conv_w, wao, wco, wo, g, b, *, tm, seq_len, attn_transposed, alpha):
    t, d = x2d.shape
    cw = u.shape[1]
    assert seq_len % tm == 0 or tm % seq_len == 0
    nt = t // tm
    row = lambda i: (i, 0)
    full = lambda a: pl.BlockSpec(a.shape, lambda i: (0,) * a.ndim)
    has_state = state is not None
    hb = tm // SUBLANES
    attn_spec = (pl.BlockSpec((ATT_W, tm), lambda i: (0, i)) if attn_transposed
                 else pl.BlockSpec((tm, ATT_W), row))
    args = [x2d, attn, u, u]
    in_specs = [pl.BlockSpec((tm, d), row), attn_spec, pl.BlockSpec((tm, cw), row),
                pl.BlockSpec((SUBLANES, cw), lambda i: (jnp.maximum(i * hb - 1, 0), 0))]
    if has_state:
        args += list(state)
        in_specs += [pl.BlockSpec((tm, cw), row)] * 2
    args += [cb, ga, gb, conv_w, wao, wco, wo, g, b]
    in_specs += [pl.BlockSpec((tm, cw), row), pl.BlockSpec((tm, d), row), pl.BlockSpec((tm, d), row),
                 full(conv_w), full(wao), full(wco), full(wo), full(g), full(b)]
    return pl.pallas_call(
        functools.partial(_mix_kernel, tm=tm, seq_len=seq_len, attn_transposed=attn_transposed,
                          has_state=has_state, alpha=alpha),
        grid=(nt,),
        in_specs=in_specs,
        out_specs=[pl.BlockSpec((tm, d), row), pl.BlockSpec((tm, d), row)],
        out_shape=[jax.ShapeDtypeStruct((t, d), F32), jax.ShapeDtypeStruct((t, d), MXU_DTYPE)],
        scratch_shapes=[pltpu.VMEM((SUBLANES + tm, cw), F32)],
        compiler_params=pltpu.CompilerParams(dimension_semantics=("arbitrary",),
                                             vmem_limit_bytes=VMEM_LIMIT),
        name="mix",
    )(*args)


def _router_kernel(x_ref, wT_ref, b_ref, idx_ref, gate_ref, rank_ref, cnt_ref, seen_ref):
    i = pl.program_id(0)
    ne, tm = wT_ref.shape[0], x_ref.shape[0]

    @pl.when(i == 0)
    def _():
        seen_ref[...] = jnp.zeros(seen_ref.shape, F32)

    logits = lax.dot_general(wT_ref[...], x_ref[...], _nt_dims(), preferred_element_type=F32,
                             precision=HIGHEST) + b_ref[...]
    eid = lax.broadcasted_iota(I32, (ne, tm), 0)
    cnt = jnp.zeros((ne, tm), I32)
    for e in range(ne):
        row = logits[e:e + 1, :]
        beats = (row > logits) | ((row == logits) & (e < eid))
        cnt = cnt + jnp.where(beats, 1, 0)
    sel = cnt < TOP_K
    lmax = jnp.max(logits, axis=0, keepdims=True)
    ex = jnp.where(sel, jnp.exp(logits - lmax), 0.0)
    gates = ex / jnp.sum(ex, axis=0, keepdims=True)
    self_ = jnp.where(sel, 1.0, 0.0)
    tt = lax.broadcasted_iota(I32, (tm, tm), 0) < lax.broadcasted_iota(I32, (tm, tm), 1)
    earlier = jnp.where(tt, 1.0, 0.0).astype(MXU_DTYPE)
    seen = seen_ref[:, 0:1]
    rank = jnp.dot(self_.astype(MXU_DTYPE), earlier, preferred_element_type=F32) + seen
    seen = seen + jnp.sum(self_, axis=1, keepdims=True)
    seen_ref[...] = jnp.broadcast_to(seen, seen_ref.shape)
    cnt_ref[...] = jnp.broadcast_to(seen, cnt_ref.shape)
    ee = lax.broadcasted_iota(I32, (ne, ne), 1) < lax.broadcasted_iota(I32, (ne, ne), 0)
    below = jnp.dot(jnp.where(ee, 1.0, 0.0).astype(MXU_DTYPE), self_.astype(MXU_DTYPE),
                    preferred_element_type=F32)
    eidf = eid.astype(F32)
    for kk in range(TOP_K):
        hit = sel & (below == float(kk))
        idx_ref[kk:kk + 1, :] = jnp.sum(jnp.where(hit, eidf, 0.0), axis=0, keepdims=True).astype(I32)
        gate_ref[kk:kk + 1, :] = jnp.sum(jnp.where(hit, gates, 0.0), axis=0, keepdims=True)
        rank_ref[kk:kk + 1, :] = jnp.sum(jnp.where(hit, rank, 0.0), axis=0, keepdims=True).astype(I32)


def _route(x1, router_wT, router_b_col, *, tm):
    t, d = x1.shape
    ne = router_wT.shape[0]
    nt = t // tm
    col = lambda i: (0, i)
    return pl.pallas_call(
        _router_kernel,
        grid=(nt,),
        in_specs=[pl.BlockSpec((tm, d), lambda i: (i, 0)),
                  pl.BlockSpec((ne, d), lambda i: (0, 0)),
                  pl.BlockSpec((ne, 1), lambda i: (0, 0))],
        out_specs=[pl.BlockSpec((TOP_K, tm), col), pl.BlockSpec((TOP_K, tm), col),
                   pl.BlockSpec((TOP_K, tm), col), pl.BlockSpec((ne, 128), lambda i: (0, 0))],
        out_shape=[jax.ShapeDtypeStruct((TOP_K, t), I32), jax.ShapeDtypeStruct((TOP_K, t), F32),
                   jax.ShapeDtypeStruct((TOP_K, t), I32), jax.ShapeDtypeStruct((ne, 128), F32)],
        scratch_shapes=[pltpu.VMEM((ne, 128), F32)],
        compiler_params=pltpu.CompilerParams(dimension_semantics=("arbitrary",),
                                             vmem_limit_bytes=VMEM_LIMIT),
        name="router",
    )(x1, router_wT, router_b_col)


def _expert_kernel(te_ref, nu_ref, x_ref, wg_ref, wu_ref, bg_ref, bu_ref, wd_ref, bd_ref, y_ref):
    del te_ref
    i = pl.program_id(0)

    @pl.when(i < nu_ref[0])
    def _():
        x = x_ref[...]
        hg = jnp.dot(x, wg_ref[0], preferred_element_type=F32) + bg_ref[0]
        hu = jnp.dot(x, wu_ref[0], preferred_element_type=F32) + bu_ref[0]
        g = jnp.minimum(hg, SWIGLU_LIMIT)
        u = jnp.clip(hu, -SWIGLU_LIMIT, SWIGLU_LIMIT)
        a = g * _sigmoid(SWIGLU_ALPHA * g) * (u + 1.0)
        y_ref[...] = jnp.dot(a.astype(MXU_DTYPE), wd_ref[0], preferred_element_type=F32) + bd_ref[0]

    @pl.when(i >= nu_ref[0])
    def _():
        y_ref[...] = jnp.zeros(y_ref.shape, F32)


def _experts(tile_e, n_used, xb, wg, wu, bg, bu, wd, bd, *, tm):
    rows, d = xb.shape
    dff = wg.shape[2]
    nt = rows // tm
    ew = lambda i, te, nu: (te[i], 0, 0)
    grid_spec = pltpu.PrefetchScalarGridSpec(
        num_scalar_prefetch=2,
        grid=(nt,),
        in_specs=[pl.BlockSpec((tm, d), lambda i, te, nu: (i, 0)),
                  pl.BlockSpec((1, d, dff), ew), pl.BlockSpec((1, d, dff), ew),
                  pl.BlockSpec((1, 1, dff), ew), pl.BlockSpec((1, 1, dff), ew),
                  pl.BlockSpec((1, dff, d), ew), pl.BlockSpec((1, 1, d), ew)],
        out_specs=pl.BlockSpec((tm, d), lambda i, te, nu: (i, 0)),
    )
    return pl.pallas_call(
        _expert_kernel,
        grid_spec=grid_spec,
        out_shape=jax.ShapeDtypeStruct((rows, d), F32),
        compiler_params=pltpu.CompilerParams(dimension_semantics=("arbitrary",),
                                             vmem_limit_bytes=VMEM_LIMIT),
        name="experts",
    )(tile_e, n_used, xb, wg, wu, bg, bu, wd, bd)


def _split_kernel(w_ref, wg_ref, wu_ref):
    half = 128
    c = lax.broadcasted_iota(I32, (2 * half, 2 * half), 0)
    o = lax.broadcasted_iota(I32, (2 * half, 2 * half), 1)
    src = jnp.where(o < half, 2 * o, 2 * (o - half) + 1)
    perm = jnp.where(c == src, 1.0, 0.0).astype(MXU_DTYPE)
    for j in range(w_ref.shape[1] // (2 * half)):
        wb = w_ref[:, 2 * half * j:2 * half * (j + 1)].astype(MXU_DTYPE)
        y = jnp.dot(wb, perm, preferred_element_type=F32).astype(MXU_DTYPE)
        wg_ref[:, half * j:half * (j + 1)] = y[:, :half]
        wu_ref[:, half * j:half * (j + 1)] = y[:, half:]


def _split_gate_up(w_gate_up, *, tr):
    ne, d, f2 = w_gate_up.shape
    rows = ne * d
    w2d = w_gate_up.reshape(rows, f2)
    wg, wu = pl.pallas_call(
        _split_kernel,
        grid=(rows // tr,),
        in_specs=[pl.BlockSpec((tr, f2), lambda i: (i, 0))],
        out_specs=[pl.BlockSpec((tr, f2 // 2), lambda i: (i, 0))] * 2,
        out_shape=[jax.ShapeDtypeStruct((rows, f2 // 2), MXU_DTYPE)] * 2,
        compiler_params=pltpu.CompilerParams(dimension_semantics=("arbitrary",),
                                             vmem_limit_bytes=VMEM_LIMIT),
        name="split_gate_up",
    )(w2d)
    return wg.reshape(ne, d, f2 // 2), wu.reshape(ne, d, f2 // 2)


def _norm2_kernel(x1_ref, f_ref, g_ref, b_ref, y_ref, *, alpha):
    y_ref[...] = _layer_norm(alpha * x1_ref[...] + f_ref[...], g_ref[...], b_ref[...])


def _norm2(x1_all, ffn_all, g, b, *, row0, nrows, tm, alpha):
    d = x1_all.shape[1]
    off = row0 // tm
    src = lambda i: (i + off, 0)
    return pl.pallas_call(
        functools.partial(_norm2_kernel, alpha=alpha),
        grid=(nrows // tm,),
        in_specs=[pl.BlockSpec((tm, d), src), pl.BlockSpec((tm, d), src),
                  pl.BlockSpec((1, d), lambda i: (0, 0)), pl.BlockSpec((1, d), lambda i: (0, 0))],
        out_specs=pl.BlockSpec((tm, d), lambda i: (i, 0)),
        out_shape=jax.ShapeDtypeStruct((nrows, d), F32),
        compiler_params=pltpu.CompilerParams(dimension_semantics=("arbitrary",),
                                             vmem_limit_bytes=VMEM_LIMIT),
        name="norm2",
    )(x1_all, ffn_all, g, b)


def _moe(x1_all, x1b_all, router_w, router_b, w_gate_up, b_gate_up, w_down, b_down):
    t, d = x1_all.shape
    ne = router_w.shape[1]
    idx, gate, rank, counts = _route(x1_all, router_w.T, router_b.reshape(ne, 1), tm=TOKEN_TILE)
    tm = EXPERT_TILE
    counts = counts[:, 0].astype(I32)
    padded = (counts + tm - 1) // tm * tm
    pad_end = jnp.cumsum(padded)
    pad_start = pad_end - padded
    n_tiles = (t * TOP_K + ne * (tm - 1)) // tm
    rows = n_tiles * tm
    dest = pad_start[idx] + rank
    tok = jnp.broadcast_to(jnp.arange(t, dtype=I32)[None, :], dest.shape)
    row_tok = jnp.zeros((rows,), I32).at[dest.reshape(-1)].set(tok.reshape(-1), unique_indices=True)
    tile_e = jnp.minimum(jnp.searchsorted(pad_end, jnp.arange(n_tiles, dtype=I32) * tm, side="right"),
                         ne - 1).astype(I32)
    n_used = (pad_end[-1:] // tm).astype(I32)
    xb = x1b_all[row_tok]
    wg, wu = _split_gate_up(w_gate_up, tr=SPLIT_TILE)
    bg = b_gate_up[:, None, 0::2]
    bu = b_gate_up[:, None, 1::2]
    yb = _experts(tile_e, n_used, xb, wg, wu, bg, bu, w_down.astype(MXU_DTYPE), b_down[:, None, :], tm=tm)
    ffn = gate[0][:, None] * yb[dest[0]]
    for kk in range(1, TOP_K):
        ffn = ffn + gate[kk][:, None] * yb[dest[kk]]
    return ffn


def kernel(x_prompt, x_sample, cache_k, cache_v, state_conv, page_table, rel_bias, w_in, conv_w, w_att_o,
           w_conv_o, w_o, ln1_g, ln1_b, router_w, router_b, w_gate_up, b_gate_up, w_down, b_down, ln2_g, ln2_b):
    depth = w_in.shape[0]
    batch, seq, d = x_prompt.shape
    nseq, tq, _ = x_sample.shape
    n_pool = cache_k.shape[1]
    ppseq = page_table.shape[1]
    past = ppseq * PAGE_SIZE
    assert seq % MOBA_BLOCK == 0 and past % MOBA_BLOCK == 0 and tq % SUBLANES == 0 and tq <= PAGE_SIZE
    nb_past = past // MOBA_BLOCK
    tp, ts = batch * seq, nseq * tq
    alpha = (2 * depth) ** 0.25
    cw = conv_w.shape[2]
    tab_rows = jnp.repeat(rel_bias.T, tq, axis=0)
    ck = cache_k.reshape(depth * n_pool, PAGE_SIZE, ATT_W)
    cv = cache_v.reshape(depth * n_pool, PAGE_SIZE, ATT_W)
    hp = x_prompt.reshape(tp, d)
    hs = x_sample.reshape(ts, d)
    outs = [[] for _ in range(6)]
    for l in range(depth):
        w_in_b = w_in[l].astype(MXU_DTYPE)
        wao, wco, wo = (w_att_o[l].astype(MXU_DTYPE), w_conv_o[l].astype(MXU_DTYPE), w_o[l].astype(MXU_DTYPE))
        g1, b1, g2, b2 = ln1_g[l][None], ln1_b[l][None], ln2_g[l][None], ln2_b[l][None]
        qT, khm, vT, kp, vp, km, up, cbp, gap, gbp = _project(hp, w_in_b, tm=PROJ_TILE, head_major=True)
        nbt = PROJ_TILE // MOBA_BLOCK
        kmean = km[:, :nbt].reshape(batch, seq // MOBA_BLOCK, N_HEADS, HEAD_DIM).transpose(0, 2, 1, 3)
        kmean = kmean.reshape(batch * N_HEADS, seq // MOBA_BLOCK, HEAD_DIM)
        attn_p = _moba_prompt(rel_bias, qT, khm, vT, kmean, batch=batch, seq=seq)
        x1p, x1bp = _mix(hp, attn_p, up, None, cbp, gap, gbp, conv_w[l], wao, wco, wo, g1, b1,
                         tm=PROJ_TILE, seq_len=seq, attn_transposed=True, alpha=alpha)
        qs, ks, vs, us, cbs, gas, gbs = _project(hs, w_in_b, tm=ts, head_major=False)
        pages = (page_table + l * n_pool).reshape(-1).astype(I32)
        attn_s = _moba_sample(pages, qs, ks, vs, tab_rows, ck, cv, nseq=nseq, tq=tq, nb_past=nb_past)
        state = (jnp.repeat(state_conv[l][:, 0], tq, axis=0), jnp.repeat(state_conv[l][:, 1], tq, axis=0))
        x1s, x1bs = _mix(hs, attn_s, us, state, cbs, gas, gbs, conv_w[l], wao, wco, wo, g1, b1,
                         tm=ts, seq_len=tq, attn_transposed=False, alpha=alpha)
        x1_all = jnp.concatenate([x1p, x1s], axis=0)
        x1b_all = jnp.concatenate([x1bp, x1bs], axis=0)
        ffn = _moe(x1_all, x1b_all, router_w[l], router_b[l], w_gate_up[l], b_gate_up[l], w_down[l], b_down[l])
        hp = _norm2(x1_all, ffn, g2, b2, row0=0, nrows=tp, tm=TOKEN_TILE, alpha=alpha)
        hs = _norm2(x1_all, ffn, g2, b2, row0=tp, nrows=ts, tm=TOKEN_TILE, alpha=alpha)
        outs[0].append(kp.reshape(batch, seq, N_HEADS, HEAD_DIM))
        outs[1].append(vp.reshape(batch, seq, N_HEADS, HEAD_DIM))
        outs[2].append(up.reshape(batch, seq, cw)[:, seq - (CONV_K - 1):])
        outs[3].append(ks.reshape(nseq, tq, N_HEADS, HEAD_DIM))
        outs[4].append(vs.reshape(nseq, tq, N_HEADS, HEAD_DIM))
        outs[5].append(us.reshape(nseq, tq, cw)[:, tq - (CONV_K - 1):])
    stacked = [jnp.stack(o) for o in outs]
    return (hp.reshape(batch, seq, d), hs.reshape(nseq, tq, d), *stacked)
```

```python
import functools
import math

import jax
import jax.numpy as jnp
from jax import lax
from jax.experimental import pallas as pl
from jax.experimental.pallas import tpu as pltpu

F32 = jnp.float32
I32 = jnp.int32
MXU_DTYPE = jnp.bfloat16
HIGHEST = lax.Precision.HIGHEST

N_HEADS = 8
HEAD_DIM = 64
ATT_W = N_HEADS * HEAD_DIM
MOBA_BLOCK = 256
MOBA_TOPK = 3
PAGE_SIZE = 128
PAGES_PER_BLOCK = MOBA_BLOCK // PAGE_SIZE
CONV_K = 3
N_EXPERTS = 32
TOP_K = 4
SWIGLU_LIMIT = 7.0
SWIGLU_ALPHA = 1.702
NUM_BUCKETS = 32
NUM_EXACT = NUM_BUCKETS // 2
MAX_DISTANCE = 128
LN_EPS = 1e-5
SCORE_SCALE = HEAD_DIM ** -0.5
NEG = -1e30

SUBLANES = 8
PROJ_TILE = 256
TOKEN_TILE = 256
EXPERT_TILE = 512
SPLIT_TILE = 512
SAMPLE_CHUNK_BLOCKS = 8
PROMPT_GROUP_BLOCKS = 4
VMEM_LIMIT = 56 * 1024 * 1024

assert MOBA_BLOCK >= MAX_DISTANCE


def _nt_dims():
    return (((1,), (1,)), ((), ()))


def _sigmoid(x):
    return 1.0 / (1.0 + jnp.exp(-x))


def _bucket(dist):
    n = jnp.maximum(dist, 0)
    nf = jnp.maximum(n, 1).astype(F32)
    large = NUM_EXACT + (jnp.log(nf / NUM_EXACT) / math.log(MAX_DISTANCE / NUM_EXACT)
                         * (NUM_BUCKETS - NUM_EXACT)).astype(I32)
    large = jnp.minimum(large, NUM_BUCKETS - 1)
    return jnp.where(n < NUM_EXACT, n, large)


def _layer_norm(z, g, b):
    mu = jnp.mean(z, axis=-1, keepdims=True)
    zc = z - mu
    var = jnp.mean(zc * zc, axis=-1, keepdims=True)
    return zc * lax.rsqrt(var + LN_EPS) * g + b


def _proj_kernel(x_ref, w_ref, *out_refs, tm, head_major):
    xb = x_ref[...].astype(MXU_DTYPE)

    def mm(c0, width):
        return jnp.dot(xb, w_ref[:, c0:c0 + width], preferred_element_type=F32)

    if head_major:
        qT_ref, khm_ref, vT_ref, k_ref, v_ref, km_ref, u_ref, cb_ref, ga_ref, gb_ref = out_refs
    else:
        q_ref, k_ref, v_ref, u_ref, cb_ref, ga_ref, gb_ref = out_refs
    d_model = x_ref.shape[1]
    q = mm(0, ATT_W) * SCORE_SCALE
    k = mm(ATT_W, ATT_W)
    v = mm(2 * ATT_W, ATT_W)
    if head_major:
        qT_ref[...] = q.T
        k_ref[0] = k.T
        vt = v.T
        v_ref[0] = vt
        km_ref[...] = jnp.zeros(km_ref.shape, F32)
        for r in range(tm // MOBA_BLOCK):
            kr = k[r * MOBA_BLOCK:(r + 1) * MOBA_BLOCK]
            km_ref[0, r:r + 1, :] = jnp.sum(kr, axis=0, keepdims=True) * (1.0 / MOBA_BLOCK)
            vT_ref[r] = vt[:, r * MOBA_BLOCK:(r + 1) * MOBA_BLOCK].astype(MXU_DTYPE)
            for h in range(N_HEADS):
                khm_ref[h, r] = kr[:, h * HEAD_DIM:(h + 1) * HEAD_DIM].astype(MXU_DTYPE)
    else:
        q_ref[...] = q
        k_ref[...] = k
        v_ref[...] = v
    c0 = 3 * ATT_W
    cw = u_ref.shape[1]
    cb_ref[...] = mm(c0, cw).astype(cb_ref.dtype)
    u_ref[...] = mm(c0 + cw, cw) * mm(c0 + 2 * cw, cw)
    ga_ref[...] = _sigmoid(mm(c0 + 3 * cw, d_model)).astype(ga_ref.dtype)
    gb_ref[...] = _sigmoid(mm(c0 + 3 * cw + d_model, d_model)).astype(gb_ref.dtype)


def _project(x2d, w_in_b, *, tm, head_major, seq=None):
    t, d = x2d.shape
    cw = (w_in_b.shape[1] - 3 * ATT_W - 2 * d) // 3
    nt = t // tm
    row = lambda i: (i, 0)
    f32s = lambda shape: jax.ShapeDtypeStruct(shape, F32)
    mxs = lambda shape: jax.ShapeDtypeStruct(shape, MXU_DTYPE)
    tail_shapes = [f32s((t, cw)), mxs((t, cw)), mxs((t, d)), mxs((t, d))]
    tail_specs = [pl.BlockSpec((tm, cw), row), pl.BlockSpec((tm, cw), row),
                  pl.BlockSpec((tm, d), row), pl.BlockSpec((tm, d), row)]
    if head_major:
        nbt = tm // MOBA_BLOCK
        tps = seq // tm
        seq_t = pl.BlockSpec((1, ATT_W, tm), lambda i: (i // tps, 0, i % tps))
        out_shape = [f32s((ATT_W, t)), mxs((N_HEADS, t // MOBA_BLOCK, MOBA_BLOCK, HEAD_DIM)),
                     mxs((t // MOBA_BLOCK, ATT_W, MOBA_BLOCK)), f32s((t // seq, ATT_W, seq)),
                     f32s((t // seq, ATT_W, seq)), f32s((nt, SUBLANES, ATT_W))] + tail_shapes
        out_specs = [pl.BlockSpec((ATT_W, tm), lambda i: (0, i)),
                     pl.BlockSpec((N_HEADS, nbt, MOBA_BLOCK, HEAD_DIM), lambda i: (0, i, 0, 0)),
                     pl.BlockSpec((nbt, ATT_W, MOBA_BLOCK), lambda i: (i, 0, 0)),
                     seq_t, seq_t,
                     pl.BlockSpec((1, SUBLANES, ATT_W), lambda i: (i, 0, 0))] + tail_specs
    else:
        out_shape = [f32s((t, ATT_W))] * 3 + tail_shapes
        out_specs = [pl.BlockSpec((tm, ATT_W), row)] * 3 + tail_specs
    return pl.pallas_call(
        functools.partial(_proj_kernel, tm=tm, head_major=head_major),
        grid=(nt,),
        in_specs=[pl.BlockSpec((tm, d), row), pl.BlockSpec(w_in_b.shape, lambda i: (0, 0))],
        out_specs=out_specs,
        out_shape=out_shape,
        compiler_params=pltpu.CompilerParams(dimension_semantics=("arbitrary",),
                                             vmem_limit_bytes=VMEM_LIMIT),
        name="proj",
    )(x2d, w_in_b)


def _moba_prompt_kernel(tab_ref, qT_ref, k_ref, vT_ref, km_ref, o_ref, selb_ref, bias_ref):
    h = pl.program_id(1)
    i = pl.program_id(2)
    nb = km_ref.shape[1]
    blk = MOBA_BLOCK
    grp = PROMPT_GROUP_BLOCKS

    @pl.when(i == 0)
    def _():
        kk = lax.broadcasted_iota(I32, (blk, blk), 0)
        qq = lax.broadcasted_iota(I32, (blk, blk), 1)
        for age in range(2):
            dist = qq - kk + age * blk
            bucket = _bucket(dist)
            tile = jnp.zeros((blk, blk), F32)
            for b in range(NUM_BUCKETS):
                tile = jnp.where(bucket == b, tab_ref[b, h], tile)
            bias_ref[age] = jnp.where(dist >= 0, tile, NEG)
        bias_ref[2] = jnp.full((blk, blk), tab_ref[NUM_BUCKETS - 1, h], F32)

    qT = qT_ref[...]
    gT = jnp.dot(km_ref[0], qT, preferred_element_type=F32, precision=HIGHEST)
    nid = lax.broadcasted_iota(I32, (nb, blk), 0)
    own = jnp.full((nb, blk), i, I32)
    cnt = jnp.zeros((nb, blk), I32)
    for m in range(nb):
        row = gT[m:m + 1, :]
        beats = (row > gT) | ((row == gT) & (m < nid))
        cnt = cnt + jnp.where(beats & (m < own), 1, 0)
    keep = ((cnt < MOBA_TOPK) & (nid < own)) | (nid == own)
    selb_ref[...] = jnp.where(keep, 0.0, NEG)

    qb = qT.astype(MXU_DTYPE)

    def group(g, carry):
        m, l, acc = carry
        scores, blocks = [], []
        for j in range(grp):
            n = i - grp * g - j
            nc = jnp.maximum(n, 0)
            age = jnp.minimum(i - nc, 2)
            rowb = selb_ref[pl.ds(nc, 1), :] + jnp.where(n >= 0, 0.0, NEG)
            scores.append(jnp.dot(k_ref[0, nc], qb, preferred_element_type=F32) + bias_ref[age] + rowb)
            blocks.append(nc)
        m_new = m
        for s in scores:
            m_new = jnp.maximum(m_new, jnp.max(s, axis=0, keepdims=True))
        alpha = jnp.exp(m - m_new)
        l = alpha * l
        acc = alpha * acc
        for s, nc in zip(scores, blocks):
            p = jnp.exp(s - m_new)
            l = l + jnp.sum(p, axis=0, keepdims=True)
            acc = acc + jnp.dot(vT_ref[nc], p.astype(MXU_DTYPE), preferred_element_type=F32)
        return m_new, l, acc

    init = (jnp.full((1, blk), NEG, F32), jnp.zeros((1, blk), F32), jnp.zeros((HEAD_DIM, blk), F32))
    _, l, acc = lax.fori_loop(0, (i + grp) // grp, group, init)
    o_ref[...] = (acc / l).astype(o_ref.dtype)


def _moba_prompt(rel_bias, qT, khm, vT, kmean, *, batch, seq):
    nb = seq // MOBA_BLOCK
    t = batch * seq
    return pl.pallas_call(
        _moba_prompt_kernel,
        grid=(batch, N_HEADS, nb),
        in_specs=[
            pl.BlockSpec(memory_space=pltpu.SMEM),
            pl.BlockSpec((HEAD_DIM, MOBA_BLOCK), lambda b, h, i: (h, b * nb + i)),
            pl.BlockSpec((1, nb, MOBA_BLOCK, HEAD_DIM), lambda b, h, i: (h, b, 0, 0)),
            pl.BlockSpec((nb, HEAD_DIM, MOBA_BLOCK), lambda b, h, i: (b, h, 0)),
            pl.BlockSpec((1, nb, HEAD_DIM), lambda b, h, i: (b * N_HEADS + h, 0, 0)),
        ],
        out_specs=pl.BlockSpec((HEAD_DIM, MOBA_BLOCK), lambda b, h, i: (h, b * nb + i)),
        out_shape=jax.ShapeDtypeStruct((ATT_W, t), MXU_DTYPE),
        scratch_shapes=[pltpu.VMEM((nb, MOBA_BLOCK), F32),
                        pltpu.VMEM((3, MOBA_BLOCK, MOBA_BLOCK), F32)],
        compiler_params=pltpu.CompilerParams(dimension_semantics=("arbitrary",) * 3,
                                             vmem_limit_bytes=VMEM_LIMIT),
        name="moba_prompt",
    )(rel_bias, qT, khm, vT, kmean)


def _moba_sample_kernel(pt_ref, q_ref, kn_ref, vn_ref, tab_ref, *refs, tq, nb_past, cb):
    del pt_ref
    npg = cb * PAGES_PER_BLOCK
    k_pages = refs[:npg]
    v_pages = refs[npg:2 * npg]
    o_ref, km_ref, m_ref, l_ref, acc_ref = refs[2 * npg:]
    c = pl.program_id(1)
    n_chunks = nb_past // cb
    rows = N_HEADS * tq
    width = q_ref.shape[1]
    r_id = lax.broadcasted_iota(I32, (rows, width), 0)
    lane = lax.broadcasted_iota(I32, (rows, width), 1)
    head_mask = (lane // HEAD_DIM) == (r_id // tq)
    q_rows = jnp.concatenate([q_ref[...]] * N_HEADS, axis=0)
    qbd = jnp.where(head_mask, q_rows, 0.0)
    qbd_b = qbd.astype(MXU_DTYPE)
    tab = tab_ref[...]

    def bias_rows(dist):
        bucket = _bucket(dist)
        out = jnp.zeros(dist.shape, F32)
        for b in range(NUM_BUCKETS):
            out = jnp.where(bucket == b, tab[:, b:b + 1], out)
        return out

    c_far = tab[:, NUM_BUCKETS - 1:NUM_BUCKETS]

    @pl.when(c == 0)
    def _():
        km_ref[...] = jnp.zeros(km_ref.shape, F32)

    km_lane = lax.broadcasted_iota(I32, km_ref.shape, 1)
    for j in range(cb):
        n = c * cb + j
        kts = [k_pages[PAGES_PER_BLOCK * j + g][0] for g in range(PAGES_PER_BLOCK)]
        vts = [v_pages[PAGES_PER_BLOCK * j + g][0] for g in range(PAGES_PER_BLOCK)]
        ksum = kts[0]
        for kt in kts[1:]:
            ksum = ksum + kt
        kmean = jnp.sum(ksum, axis=1, keepdims=True) * (1.0 / MOBA_BLOCK)
        km_ref[...] = jnp.where(km_lane == n, kmean, km_ref[...])
        s = jnp.concatenate([jnp.dot(qbd_b, kt.astype(MXU_DTYPE), preferred_element_type=F32) for kt in kts],
                            axis=1)
        if j == cb - 1:
            qi = lax.broadcasted_iota(I32, (rows, MOBA_BLOCK), 0) % tq
            kj = lax.broadcasted_iota(I32, (rows, MOBA_BLOCK), 1)
            near = bias_rows(MOBA_BLOCK + qi - kj)
            s = s + jnp.where(c == n_chunks - 1, near, c_far)
        else:
            s = s + c_far
        m_n = jnp.max(s, axis=1, keepdims=True)
        p = jnp.exp(s - m_n)
        m_ref[n] = jnp.broadcast_to(m_n, m_ref.shape[1:])
        l_ref[n] = jnp.broadcast_to(jnp.sum(p, axis=1, keepdims=True), l_ref.shape[1:])
        pb = p.astype(MXU_DTYPE)
        acc = None
        for g, vt in enumerate(vts):
            part = lax.dot_general(pb[:, g * PAGE_SIZE:(g + 1) * PAGE_SIZE], vt.astype(MXU_DTYPE), _nt_dims(),
                                   preferred_element_type=F32)
            acc = part if acc is None else acc + part
        acc_ref[n] = acc

    @pl.when(c == n_chunks - 1)
    def _():
        gate = jnp.dot(qbd, km_ref[...], preferred_element_type=F32, precision=HIGHEST)
        nid = lax.broadcasted_iota(I32, gate.shape, 1)
        cnt = jnp.zeros(gate.shape, I32)
        for n in range(nb_past):
            col = gate[:, n:n + 1]
            beats = (col > gate) | ((col == gate) & (n < nid))
            cnt = cnt + jnp.where(beats, 1, 0)
        sel = cnt < MOBA_TOPK
        pad = jnp.zeros((PAGE_SIZE - tq, width), F32)
        kn = jnp.concatenate([kn_ref[...], pad], axis=0)
        vn = jnp.concatenate([vn_ref[...], pad], axis=0)
        qi = lax.broadcasted_iota(I32, (rows, PAGE_SIZE), 0) % tq
        kj = lax.broadcasted_iota(I32, (rows, PAGE_SIZE), 1)
        s = lax.dot_general(qbd_b, kn.astype(MXU_DTYPE), _nt_dims(), preferred_element_type=F32)
        s = jnp.where(kj <= qi, s + bias_rows(qi - kj), NEG)
        m_o = jnp.max(s, axis=1, keepdims=True)
        p = jnp.exp(s - m_o)
        l_o = jnp.sum(p, axis=1, keepdims=True)
        acc_o = jnp.dot(p.astype(MXU_DTYPE), vn.astype(MXU_DTYPE), preferred_element_type=F32)
        m_all = m_o
        for n in range(nb_past):
            m_all = jnp.maximum(m_all, jnp.where(sel[:, n:n + 1], m_ref[n][:, 0:1], NEG))
        w_o = jnp.exp(m_o - m_all)
        l_all = w_o * l_o
        acc_all = w_o * acc_o
        for n in range(nb_past):
            w_n = jnp.where(sel[:, n:n + 1], jnp.exp(m_ref[n][:, 0:1] - m_all), 0.0)
            l_all = l_all + w_n * l_ref[n][:, 0:1]
            acc_all = acc_all + w_n * acc_ref[n]
        out_bd = jnp.where(head_mask, acc_all / l_all, 0.0)
        out = out_bd[0:tq]
        for h in range(1, N_HEADS):
            out = out + out_bd[h * tq:(h + 1) * tq]
        o_ref[...] = out


def _moba_sample(page_rows, q, k_new, v_new, tab_rows, cache_k, cache_v, *, nseq, tq, nb_past):
    assert nb_past <= 128
    cb = math.gcd(SAMPLE_CHUNK_BLOCKS, nb_past)
    npg = cb * PAGES_PER_BLOCK
    n_chunks = nb_past // cb
    ppseq = nb_past * PAGES_PER_BLOCK
    rows = N_HEADS * tq
    width = q.shape[1]
    tok = lambda b, c, pt: (b, 0)

    def page_spec(g):
        return pl.BlockSpec((1, width, PAGE_SIZE), lambda b, c, pt: (pt[b * ppseq + c * npg + g], 0, 0))

    grid_spec = pltpu.PrefetchScalarGridSpec(
        num_scalar_prefetch=1,
        grid=(nseq, n_chunks),
        in_specs=[pl.BlockSpec((tq, width), tok)] * 3
        + [pl.BlockSpec(tab_rows.shape, lambda b, c, pt: (0, 0))]
        + [page_spec(g) for g in range(npg)] * 2,
        out_specs=pl.BlockSpec((tq, width), tok),
        scratch_shapes=[pltpu.VMEM((width, 128), F32),
                        pltpu.VMEM((nb_past, rows, 128), F32),
                        pltpu.VMEM((nb_past, rows, 128), F32),
                        pltpu.VMEM((nb_past, rows, width), F32)],
    )
    return pl.pallas_call(
        functools.partial(_moba_sample_kernel, tq=tq, nb_past=nb_past, cb=cb),
        grid_spec=grid_spec,
        out_shape=jax.ShapeDtypeStruct((nseq * tq, width), F32),
        compiler_params=pltpu.CompilerParams(dimension_semantics=("arbitrary",) * 2,
                                             vmem_limit_bytes=VMEM_LIMIT),
        name="moba_sample",
    )(page_rows, q, k_new, v_new, tab_rows, *([cache_k] * npg), *([cache_v] * npg))


def _mix_kernel(*refs, tm, seq_len, attn_transposed, has_state, alpha):
    if has_state:
        (x_ref, attn_ref, u_ref, halo_ref, st0_ref, st1_ref, cb_ref, ga_ref, gb_ref, cw_ref,
         wao_ref, wco_ref, wo_ref, g_ref, b_ref, x1_ref, ubuf) = refs
    else:
        (x_ref, attn_ref, u_ref, halo_ref, cb_ref, ga_ref, gb_ref, cw_ref,
         wao_ref, wco_ref, wo_ref, g_ref, b_ref, x1_ref, ubuf) = refs
    i = pl.program_id(0)
    if attn_transposed:
        attn = attn_ref[...].astype(F32).T.astype(MXU_DTYPE)
    else:
        attn = attn_ref[...].astype(MXU_DTYPE)
    y_att = jnp.dot(attn, wao_ref[...], preferred_element_type=F32)
    u = u_ref[...]
    ubuf[0:SUBLANES, :] = halo_ref[...]
    ubuf[SUBLANES:SUBLANES + tm, :] = u
    p1 = ubuf[SUBLANES - 1:SUBLANES - 1 + tm, :]
    p2 = ubuf[SUBLANES - 2:SUBLANES - 2 + tm, :]
    row = lax.broadcasted_iota(I32, (tm, 1), 0)
    if seq_len % tm == 0:
        pos = (i * tm) % seq_len + row
    else:
        pos = row % seq_len
    s0 = st0_ref[...] if has_state else 0.0
    s1 = st1_ref[...] if has_state else 0.0
    prev1 = jnp.where(pos >= 1, p1, s1)
    prev2 = jnp.where(pos >= 2, p2, jnp.where(pos == 1, s1, s0))
    conv = cw_ref[0:1, :] * prev2 + cw_ref[1:2, :] * prev1 + cw_ref[2:3, :] * u
    y_conv = jnp.dot((cb_ref[...].astype(F32) * conv).astype(MXU_DTYPE), wco_ref[...],
                     preferred_element_type=F32)
    merged = ga_ref[...].astype(F32) * y_att + gb_ref[...].astype(F32) * y_conv
    z = alpha * x_ref[...] + jnp.dot(merged.astype(MXU_DTYPE), wo_ref[...], preferred_element_type=F32)
    x1 = _layer_norm(z, g_ref[...], b_ref[...])
    x1_ref[...] = x1


def _mix(x2d, attn, u, state, cb, ga, gb, conv_w, wao, wco, wo, g, b, *, tm, seq_len, attn_transposed, alpha):
    t, d = x2d.shape
    cw = u.shape[1]
    assert seq_len % tm == 0 or tm % seq_len == 0
    nt = t // tm
    row = lambda i: (i, 0)
    full = lambda a: pl.BlockSpec(a.shape, lambda i: (0,) * a.ndim)
    has_state = state is not None
    hb = tm // SUBLANES
    attn_spec = (pl.BlockSpec((ATT_W, tm), lambda i: (0, i)) if attn_transposed
                 else pl.BlockSpec((tm, ATT_W), row))
    args = [x2d, attn, u, u]
    in_specs = [pl.BlockSpec((tm, d), row), attn_spec, pl.BlockSpec((tm, cw), row),
                pl.BlockSpec((SUBLANES, cw), lambda i: (jnp.maximum(i * hb - 1, 0), 0))]
    if has_state:
        args += list(state)
        in_specs += [pl.BlockSpec((tm, cw), row)] * 2
    args += [cb, ga, gb, conv_w, wao, wco, wo, g, b]
    in_specs += [pl.BlockSpec((tm, cw), row), pl.BlockSpec((tm, d), row), pl.BlockSpec((tm, d), row),
                 full(conv_w), full(wao), full(wco), full(wo), full(g), full(b)]
    return pl.pallas_call(
        functools.partial(_mix_kernel, tm=tm, seq_len=seq_len, attn_transposed=attn_transposed,
                          has_state=has_state, alpha=alpha),
        grid=(nt,),
        in_specs=in_specs,
        out_specs=pl.BlockSpec((tm, d), row),
        out_shape=jax.ShapeDtypeStruct((t, d), F32),
        scratch_shapes=[pltpu.VMEM((SUBLANES + tm, cw), F32)],
        compiler_params=pltpu.CompilerParams(dimension_semantics=("arbitrary",),
                                             vmem_limit_bytes=VMEM_LIMIT),
        name="mix",
    )(*args)


def _router_kernel(x_ref, wT_ref, b_ref, idx_ref, gate_ref, rank_ref, cnt_ref, seen_ref):
    i = pl.program_id(0)
    ne, tm = wT_ref.shape[0], x_ref.shape[0]

    @pl.when(i == 0)
    def _():
        seen_ref[...] = jnp.zeros(seen_ref.shape, F32)

    logits = lax.dot_general(wT_ref[...], x_ref[...], _nt_dims(), preferred_element_type=F32,
                             precision=HIGHEST) + b_ref[...]
    eid = lax.broadcasted_iota(I32, (ne, tm), 0)
    cnt = jnp.zeros((ne, tm), I32)
    for e in range(ne):
        row = logits[e:e + 1, :]
        beats = (row > logits) | ((row == logits) & (e < eid))
        cnt = cnt + jnp.where(beats, 1, 0)
    sel = cnt < TOP_K
    lmax = jnp.max(logits, axis=0, keepdims=True)
    ex = jnp.where(sel, jnp.exp(logits - lmax), 0.0)
    gates = ex / jnp.sum(ex, axis=0, keepdims=True)
    self_ = jnp.where(sel, 1.0, 0.0)
    tt = lax.broadcasted_iota(I32, (tm, tm), 0) < lax.broadcasted_iota(I32, (tm, tm), 1)
    earlier = jnp.where(tt, 1.0, 0.0).astype(MXU_DTYPE)
    seen = seen_ref[:, 0:1]
    rank = jnp.dot(self_.astype(MXU_DTYPE), earlier, preferred_element_type=F32) + seen
    seen = seen + jnp.sum(self_, axis=1, keepdims=True)
    seen_ref[...] = jnp.broadcast_to(seen, seen_ref.shape)
    cnt_ref[...] = jnp.broadcast_to(seen, cnt_ref.shape)
    ee = lax.broadcasted_iota(I32, (ne, ne), 1) < lax.broadcasted_iota(I32, (ne, ne), 0)
    below = jnp.dot(jnp.where(ee, 1.0, 0.0).astype(MXU_DTYPE), self_.astype(MXU_DTYPE),
                    preferred_element_type=F32)
    eidf = eid.astype(F32)
    for kk in range(TOP_K):
        hit = sel & (below == float(kk))
        idx_ref[kk:kk + 1, :] = jnp.sum(jnp.where(hit, eidf, 0.0), axis=0, keepdims=True).astype(I32)
        gate_ref[kk:kk + 1, :] = jnp.sum(jnp.where(hit, gates, 0.0), axis=0, keepdims=True)
        rank_ref[kk:kk + 1, :] = jnp.sum(jnp.where(hit, rank, 0.0), axis=0, keepdims=True).astype(I32)


def _route(x1, router_wT, router_b_col, *, tm):
    t, d = x1.shape
    ne = router_wT.shape[0]
    nt = t // tm
    col = lambda i: (0, i)
    return pl.pallas_call(
        _router_kernel,
        grid=(nt,),
        in_specs=[pl.BlockSpec((tm, d), lambda i: (i, 0)),
                  pl.BlockSpec((ne, d), lambda i: (0, 0)),
                  pl.BlockSpec((ne, 1), lambda i: (0, 0))],
        out_specs=[pl.BlockSpec((TOP_K, tm), col), pl.BlockSpec((TOP_K, tm), col),
                   pl.BlockSpec((TOP_K, tm), col), pl.BlockSpec((ne, 128), lambda i: (0, 0))],
        out_shape=[jax.ShapeDtypeStruct((TOP_K, t), I32), jax.ShapeDtypeStruct((TOP_K, t), F32),
                   jax.ShapeDtypeStruct((TOP_K, t), I32), jax.ShapeDtypeStruct((ne, 128), F32)],
        scratch_shapes=[pltpu.VMEM((ne, 128), F32)],
        compiler_params=pltpu.CompilerParams(dimension_semantics=("arbitrary",),
                                             vmem_limit_bytes=VMEM_LIMIT),
        name="router",
    )(x1, router_wT, router_b_col)


def _expert_kernel(te_ref, nu_ref, x_ref, wg_ref, wu_ref, bg_ref, bu_ref, wd_ref, bd_ref, y_ref):
    del te_ref
    i = pl.program_id(0)

    @pl.when(i < nu_ref[0])
    def _():
        x = x_ref[...].astype(MXU_DTYPE)
        hg = jnp.dot(x, wg_ref[0], preferred_element_type=F32) + bg_ref[0]
        hu = jnp.dot(x, wu_ref[0], preferred_element_type=F32) + bu_ref[0]
        g = jnp.minimum(hg, SWIGLU_LIMIT)
        u = jnp.clip(hu, -SWIGLU_LIMIT, SWIGLU_LIMIT)
        a = g * _sigmoid(SWIGLU_ALPHA * g) * (u + 1.0)
        y_ref[...] = jnp.dot(a.astype(MXU_DTYPE), wd_ref[0], preferred_element_type=F32) + bd_ref[0]

    @pl.when(i >= nu_ref[0])
    def _():
        y_ref[...] = jnp.zeros(y_ref.shape, F32)


def _experts(tile_e, n_used, xb, wg, wu, bg, bu, wd, bd, *, tm):
    rows, d = xb.shape
    dff = wg.shape[2]
    nt = rows // tm
    ew = lambda i, te, nu: (te[i], 0, 0)
    grid_spec = pltpu.PrefetchScalarGridSpec(
        num_scalar_prefetch=2,
        grid=(nt,),
        in_specs=[pl.BlockSpec((tm, d), lambda i, te, nu: (i, 0)),
                  pl.BlockSpec((1, d, dff), ew), pl.BlockSpec((1, d, dff), ew),
                  pl.BlockSpec((1, 1, dff), ew), pl.BlockSpec((1, 1, dff), ew),
                  pl.BlockSpec((1, dff, d), ew), pl.BlockSpec((1, 1, d), ew)],
        out_specs=pl.BlockSpec((tm, d), lambda i, te, nu: (i, 0)),
    )
    return pl.pallas_call(
        _expert_kernel,
        grid_spec=grid_spec,
        out_shape=jax.ShapeDtypeStruct((rows, d), F32),
        compiler_params=pltpu.CompilerParams(dimension_semantics=("arbitrary",),
                                             vmem_limit_bytes=VMEM_LIMIT),
        name="experts",
    )(tile_e, n_used, xb, wg, wu, bg, bu, wd, bd)


def _split_kernel(w_ref, wg_ref, wu_ref):
    half = 128
    c = lax.broadcasted_iota(I32, (2 * half, 2 * half), 0)
    o = lax.broadcasted_iota(I32, (2 * half, 2 * half), 1)
    src = jnp.where(o < half, 2 * o, 2 * (o - half) + 1)
    perm = jnp.where(c == src, 1.0, 0.0).astype(MXU_DTYPE)
    for j in range(w_ref.shape[1] // (2 * half)):
        wb = w_ref[:, 2 * half * j:2 * half * (j + 1)].astype(MXU_DTYPE)
        y = jnp.dot(wb, perm, preferred_element_type=F32).astype(MXU_DTYPE)
        wg_ref[:, half * j:half * (j + 1)] = y[:, :half]
        wu_ref[:, half * j:half * (j + 1)] = y[:, half:]


def _split_gate_up(w_gate_up, *, tr):
    ne, d, f2 = w_gate_up.shape
    rows = ne * d
    w2d = w_gate_up.reshape(rows, f2)
    wg, wu = pl.pallas_call(
        _split_kernel,
        grid=(rows // tr,),
        in_specs=[pl.BlockSpec((tr, f2), lambda i: (i, 0))],
        out_specs=[pl.BlockSpec((tr, f2 // 2), lambda i: (i, 0))] * 2,
        out_shape=[jax.ShapeDtypeStruct((rows, f2 // 2), MXU_DTYPE)] * 2,
        compiler_params=pltpu.CompilerParams(dimension_semantics=("arbitrary",),
                                             vmem_limit_bytes=VMEM_LIMIT),
        name="split_gate_up",
    )(w2d)
    return wg.reshape(ne, d, f2 // 2), wu.reshape(ne, d, f2 // 2)


def _norm2_kernel(x1_ref, f_ref, g_ref, b_ref, y_ref, *, alpha):
    y_ref[...] = _layer_norm(alpha * x1_ref[...] + f_ref[...], g_ref[...], b_ref[...])


def _norm2(x1_all, ffn_all, g, b, *, row0, nrows, tm, alpha):
    d = x1_all.shape[1]
    off = row0 // tm
    src = lambda i: (i + off, 0)
    return pl.pallas_call(
        functools.partial(_norm2_kernel, alpha=alpha),
        grid=(nrows // tm,),
        in_specs=[pl.BlockSpec((tm, d), src), pl.BlockSpec((tm, d), src),
                  pl.BlockSpec((1, d), lambda i: (0, 0)), pl.BlockSpec((1, d), lambda i: (0, 0))],
        out_specs=pl.BlockSpec((tm, d), lambda i: (i, 0)),
        out_shape=jax.ShapeDtypeStruct((nrows, d), F32),
        compiler_params=pltpu.CompilerParams(dimension_semantics=("arbitrary",),
                                             vmem_limit_bytes=VMEM_LIMIT),
        name="norm2",
    )(x1_all, ffn_all, g, b)


def _moe(x1_all, router_w, router_b, w_gate_up, b_gate_up, w_down, b_down):
    t, d = x1_all.shape
    ne = router_w.shape[1]
    idx, gate, rank, counts = _route(x1_all, router_w.T, router_b.reshape(ne, 1), tm=TOKEN_TILE)
    tm = EXPERT_TILE
    counts = counts[:, 0].astype(I32)
    padded = (counts + tm - 1) // tm * tm
    pad_end = jnp.cumsum(padded)
    pad_start = pad_end - padded
    n_tiles = (t * TOP_K + ne * (tm - 1)) // tm
    rows = n_tiles * tm
    eids = jnp.arange(ne, dtype=I32)
    start_of = jnp.sum(jnp.where(idx[:, :, None] == eids, pad_start, 0), axis=-1)
    dest = start_of + rank
    tok = jnp.broadcast_to(jnp.arange(t, dtype=I32)[None, :], dest.shape)
    row_tok = jnp.zeros((rows,), I32).at[dest.reshape(-1)].set(tok.reshape(-1), unique_indices=True)
    tile_row = jnp.arange(n_tiles, dtype=I32) * tm
    tile_e = jnp.minimum(jnp.sum((pad_end[None, :] <= tile_row[:, None]).astype(I32), axis=1), ne - 1)
    n_used = (pad_end[-1:] // tm).astype(I32)
    xb = x1_all[row_tok]
    wg, wu = _split_gate_up(w_gate_up, tr=SPLIT_TILE)
    bg = b_gate_up[:, None, 0::2]
    bu = b_gate_up[:, None, 1::2]
    yb = _experts(tile_e, n_used, xb, wg, wu, bg, bu, w_down.astype(MXU_DTYPE), b_down[:, None, :], tm=tm)
    ffn = gate[0][:, None] * yb[dest[0]]
    for kk in range(1, TOP_K):
        ffn = ffn + gate[kk][:, None] * yb[dest[kk]]
    return ffn


def kernel(x_prompt, x_sample, cache_k, cache_v, state_conv, page_table, rel_bias, w_in, conv_w, w_att_o,
           w_conv_o, w_o, ln1_g, ln1_b, router_w, router_b, w_gate_up, b_gate_up, w_down, b_down, ln2_g, ln2_b):
    depth = w_in.shape[0]
    batch, seq, d = x_prompt.shape
    nseq, tq, _ = x_sample.shape
    n_pool = cache_k.shape[1]
    ppseq = page_table.shape[1]
    past = ppseq * PAGE_SIZE
    assert seq % MOBA_BLOCK == 0 and past % MOBA_BLOCK == 0 and tq % SUBLANES == 0 and tq <= PAGE_SIZE
    nb_past = past // MOBA_BLOCK
    tp, ts = batch * seq, nseq * tq
    alpha = (2 * depth) ** 0.25
    cw = conv_w.shape[2]
    tab_rows = jnp.repeat(rel_bias.T, tq, axis=0)
    ck = cache_k.transpose(0, 1, 3, 4, 2).reshape(depth * n_pool, ATT_W, PAGE_SIZE)
    cv = cache_v.transpose(0, 1, 3, 4, 2).reshape(depth * n_pool, ATT_W, PAGE_SIZE)
    hp = x_prompt.reshape(tp, d)
    hs = x_sample.reshape(ts, d)
    outs = [[] for _ in range(6)]
    for l in range(depth):
        w_in_b = w_in[l].astype(MXU_DTYPE)
        wao, wco, wo = (w_att_o[l].astype(MXU_DTYPE), w_conv_o[l].astype(MXU_DTYPE), w_o[l].astype(MXU_DTYPE))
        g1, b1, g2, b2 = ln1_g[l][None], ln1_b[l][None], ln2_g[l][None], ln2_b[l][None]
        qT, khm, vT, kp, vp, km, up, cbp, gap, gbp = _project(hp, w_in_b, tm=PROJ_TILE, head_major=True, seq=seq)
        nbt = PROJ_TILE // MOBA_BLOCK
        kmean = km[:, :nbt].reshape(batch, seq // MOBA_BLOCK, N_HEADS, HEAD_DIM).transpose(0, 2, 1, 3)
        kmean = kmean.reshape(batch * N_HEADS, seq // MOBA_BLOCK, HEAD_DIM)
        attn_p = _moba_prompt(rel_bias, qT, khm, vT, kmean, batch=batch, seq=seq)
        x1p = _mix(hp, attn_p, up, None, cbp, gap, gbp, conv_w[l], wao, wco, wo, g1, b1,
                         tm=PROJ_TILE, seq_len=seq, attn_transposed=True, alpha=alpha)
        qs, ks, vs, us, cbs, gas, gbs = _project(hs, w_in_b, tm=ts, head_major=False)
        pages = (page_table + l * n_pool).reshape(-1).astype(I32)
        attn_s = _moba_sample(pages, qs, ks, vs, tab_rows, ck, cv, nseq=nseq, tq=tq, nb_past=nb_past)
        state = (jnp.repeat(state_conv[l][:, 0], tq, axis=0), jnp.repeat(state_conv[l][:, 1], tq, axis=0))
        x1s = _mix(hs, attn_s, us, state, cbs, gas, gbs, conv_w[l], wao, wco, wo, g1, b1,
                         tm=ts, seq_len=tq, attn_transposed=False, alpha=alpha)
        x1_all = jnp.concatenate([x1p, x1s], axis=0)
        ffn = _moe(x1_all, router_w[l], router_b[l], w_gate_up[l], b_gate_up[l], w_down[l], b_down[l])
        hp = _norm2(x1_all, ffn, g2, b2, row0=0, nrows=tp, tm=TOKEN_TILE, alpha=alpha)
        hs = _norm2(x1_all, ffn, g2, b2, row0=tp, nrows=ts, tm=TOKEN_TILE, alpha=alpha)
        outs[0].append(kp.reshape(batch, N_HEADS, HEAD_DIM, seq).transpose(0, 3, 1, 2))
        outs[1].append(vp.reshape(batch, N_HEADS, HEAD_DIM, seq).transpose(0, 3, 1, 2))
        outs[2].append(up.reshape(batch, seq, cw)[:, seq - (CONV_K - 1):])
        outs[3].append(ks.reshape(nseq, tq, N_HEADS, HEAD_DIM))
        outs[4].append(vs.reshape(nseq, tq, N_HEADS, HEAD_DIM))
        outs[5].append(us.reshape(nseq, tq, cw)[:, tq - (CONV_K - 1):])
    stacked = [jnp.stack(o) for o in outs]
    return (hp.reshape(batch, seq, d), hs.reshape(nseq, tq, d), *stacked)
```

```python
import functools
import math

import jax
import jax.numpy as jnp
from jax import lax
from jax.experimental import pallas as pl
from jax.experimental.pallas import tpu as pltpu

F32 = jnp.float32
I32 = jnp.int32
MXU_DTYPE = jnp.bfloat16
HIGHEST = lax.Precision.HIGHEST

N_HEADS = 8
HEAD_DIM = 64
ATT_W = N_HEADS * HEAD_DIM
MOBA_BLOCK = 256
MOBA_TOPK = 3
PAGE_SIZE = 128
PAGES_PER_BLOCK = MOBA_BLOCK // PAGE_SIZE
CONV_K = 3
N_EXPERTS = 32
TOP_K = 4
SWIGLU_LIMIT = 7.0
SWIGLU_ALPHA = 1.702
NUM_BUCKETS = 32
NUM_EXACT = NUM_BUCKETS // 2
MAX_DISTANCE = 128
LN_EPS = 1e-5
SCORE_SCALE = HEAD_DIM ** -0.5
NEG = -1e30

SUBLANES = 8
LANES = 128
PROJ_TILE = 512
TOKEN_TILE = 256
EXPERT_TILE = 512
DISPATCH_UNROLL = 8
SAMPLE_CHUNK_BLOCKS = 8
VMEM_LIMIT = 56 * 1024 * 1024

assert MOBA_BLOCK >= MAX_DISTANCE


def _nt_dims():
    return (((1,), (1,)), ((), ()))


def _sigmoid(x):
    return 1.0 / (1.0 + jnp.exp(-x))


def _bucket(dist):
    n = jnp.maximum(dist, 0)
    nf = jnp.maximum(n, 1).astype(F32)
    large = NUM_EXACT + (jnp.log(nf / NUM_EXACT) / math.log(MAX_DISTANCE / NUM_EXACT)
                         * (NUM_BUCKETS - NUM_EXACT)).astype(I32)
    large = jnp.minimum(large, NUM_BUCKETS - 1)
    return jnp.where(n < NUM_EXACT, n, large)


def _layer_norm(z, g, b):
    mu = jnp.mean(z, axis=-1, keepdims=True)
    zc = z - mu
    var = jnp.mean(zc * zc, axis=-1, keepdims=True)
    return zc * lax.rsqrt(var + LN_EPS) * g + b


def _proj_kernel(x_ref, w_ref, *out_refs, tm, head_major):
    xb = x_ref[...].astype(MXU_DTYPE)

    def mm(c0, width):
        return jnp.dot(xb, w_ref[:, c0:c0 + width], preferred_element_type=F32)

    if head_major:
        qT_ref, khm_ref, vT_ref, k_ref, v_ref, km_ref, u_ref, cb_ref, ga_ref, gb_ref = out_refs
    else:
        q_ref, k_ref, v_ref, u_ref, cb_ref, ga_ref, gb_ref = out_refs
    d_model = x_ref.shape[1]
    q = mm(0, ATT_W) * SCORE_SCALE
    k = mm(ATT_W, ATT_W)
    v = mm(2 * ATT_W, ATT_W)
    if head_major:
        qT_ref[...] = q.T
        k_ref[0] = k.T
        vt = v.T
        v_ref[0] = vt
        km_ref[...] = jnp.zeros(km_ref.shape, F32)
        for r in range(tm // MOBA_BLOCK):
            kr = k[r * MOBA_BLOCK:(r + 1) * MOBA_BLOCK]
            km_ref[0, r:r + 1, :] = jnp.sum(kr, axis=0, keepdims=True) * (1.0 / MOBA_BLOCK)
            vT_ref[r] = vt[:, r * MOBA_BLOCK:(r + 1) * MOBA_BLOCK].astype(MXU_DTYPE)
            for h in range(N_HEADS):
                khm_ref[h, r] = kr[:, h * HEAD_DIM:(h + 1) * HEAD_DIM].astype(MXU_DTYPE)
    else:
        q_ref[...] = q
        k_ref[...] = k
        v_ref[...] = v
    c0 = 3 * ATT_W
    cw = u_ref.shape[1]
    cb_ref[...] = mm(c0, cw).astype(cb_ref.dtype)
    u_ref[...] = mm(c0 + cw, cw) * mm(c0 + 2 * cw, cw)
    ga_ref[...] = _sigmoid(mm(c0 + 3 * cw, d_model)).astype(ga_ref.dtype)
    gb_ref[...] = _sigmoid(mm(c0 + 3 * cw + d_model, d_model)).astype(gb_ref.dtype)


def _project(x2d, w_in_b, *, tm, head_major, seq=None):
    t, d = x2d.shape
    cw = (w_in_b.shape[1] - 3 * ATT_W - 2 * d) // 3
    nt = t // tm
    row = lambda i: (i, 0)
    f32s = lambda shape: jax.ShapeDtypeStruct(shape, F32)
    mxs = lambda shape: jax.ShapeDtypeStruct(shape, MXU_DTYPE)
    tail_shapes = [f32s((t, cw)), mxs((t, cw)), mxs((t, d)), mxs((t, d))]
    tail_specs = [pl.BlockSpec((tm, cw), row), pl.BlockSpec((tm, cw), row),
                  pl.BlockSpec((tm, d), row), pl.BlockSpec((tm, d), row)]
    if head_major:
        nbt = tm // MOBA_BLOCK
        tps = seq // tm
        seq_t = pl.BlockSpec((1, ATT_W, tm), lambda i: (i // tps, 0, i % tps))
        out_shape = [f32s((ATT_W, t)), mxs((N_HEADS, t // MOBA_BLOCK, MOBA_BLOCK, HEAD_DIM)),
                     mxs((t // MOBA_BLOCK, ATT_W, MOBA_BLOCK)), f32s((t // seq, ATT_W, seq)),
                     f32s((t // seq, ATT_W, seq)), f32s((nt, SUBLANES, ATT_W))] + tail_shapes
        out_specs = [pl.BlockSpec((ATT_W, tm), lambda i: (0, i)),
                     pl.BlockSpec((N_HEADS, nbt, MOBA_BLOCK, HEAD_DIM), lambda i: (0, i, 0, 0)),
                     pl.BlockSpec((nbt, ATT_W, MOBA_BLOCK), lambda i: (i, 0, 0)),
                     seq_t, seq_t,
                     pl.BlockSpec((1, SUBLANES, ATT_W), lambda i: (i, 0, 0))] + tail_specs
    else:
        out_shape = [f32s((t, ATT_W))] * 3 + tail_shapes
        out_specs = [pl.BlockSpec((tm, ATT_W), row)] * 3 + tail_specs
    return pl.pallas_call(
        functools.partial(_proj_kernel, tm=tm, head_major=head_major),
        grid=(nt,),
        in_specs=[pl.BlockSpec((tm, d), row), pl.BlockSpec(w_in_b.shape, lambda i: (0, 0))],
        out_specs=out_specs,
        out_shape=out_shape,
        compiler_params=pltpu.CompilerParams(dimension_semantics=("arbitrary",),
                                             vmem_limit_bytes=VMEM_LIMIT),
        name="proj",
    )(x2d, w_in_b)


def _moba_prompt_kernel(tab_ref, qT_ref, k_ref, vT_ref, km_ref, o_ref, selb_ref, bias_ref, qb_ref, *state):
    h = pl.program_id(0)
    first_sequence = pl.program_id(1) == 0
    nb = km_ref.shape[1]
    blk = MOBA_BLOCK
    n_streams = nb // 2
    m_refs, l_refs, acc_refs = state[:n_streams], state[n_streams:2 * n_streams], state[2 * n_streams:]

    @pl.when(first_sequence)
    def _():
        kk = lax.broadcasted_iota(I32, (blk, blk), 0)
        qq = lax.broadcasted_iota(I32, (blk, blk), 1)
        for age in range(2):
            dist = qq - kk + age * blk
            bucket = _bucket(dist)
            tile = jnp.zeros((blk, blk), F32)
            for b in range(NUM_BUCKETS):
                tile = jnp.where(bucket == b, tab_ref[b, h], tile)
            bias_ref[age] = jnp.where(dist >= 0, tile, NEG)
        bias_ref[2] = jnp.full((blk, blk), tab_ref[NUM_BUCKETS - 1, h], F32)

    km = km_ref[0]
    nid = lax.broadcasted_iota(I32, (nb, blk), 0)
    for i in range(nb):
        qT = qT_ref[:, i * blk:(i + 1) * blk]
        qb_ref[i] = qT.astype(MXU_DTYPE)
        gT = jnp.dot(km, qT, preferred_element_type=F32, precision=HIGHEST)
        cnt = jnp.zeros((nb, blk), I32)
        for m in range(i):
            row = gT[m:m + 1, :]
            beats = (row > gT) | ((row == gT) & (m < nid))
            cnt = cnt + jnp.where(beats, 1, 0)
        keep = ((cnt < MOBA_TOPK) & (nid < i)) | (nid == i)
        selb_ref[i] = jnp.where(keep, 0.0, NEG)

    for a in range(n_streams):
        m_refs[a][...] = jnp.full(m_refs[a].shape, NEG, F32)
        l_refs[a][...] = jnp.zeros(l_refs[a].shape, F32)
        acc_refs[a][...] = jnp.zeros(acc_refs[a].shape, F32)

    def sweep_round(t, carry):
        staged = []
        for a in range(n_streams):
            first = t <= a
            slot = jnp.where(first, 0, 1)
            i = jnp.where(first, a, nb - 1 - a)
            n = jnp.where(first, a - t, nb - t)
            age = jnp.minimum(i - n, 2)
            s = (jnp.dot(k_ref[0, n], qb_ref[i], preferred_element_type=F32) + bias_ref[age]
                 + selb_ref[i, pl.ds(n, 1), :])
            staged.append((slot, n, s))
        updates = []
        for a, (slot, n, s) in enumerate(staged):
            m_old = m_refs[a][slot]
            m_new = jnp.maximum(m_old, jnp.max(s, axis=0, keepdims=True))
            alpha = jnp.exp(m_old - m_new)
            p = jnp.exp(s - m_new)
            l_new = alpha * l_refs[a][slot] + jnp.sum(p, axis=0, keepdims=True)
            acc_new = alpha * acc_refs[a][slot] + jnp.dot(
                vT_ref[n], p.astype(MXU_DTYPE), preferred_element_type=F32)
            updates.append((slot, m_new, l_new, acc_new))
        for a, (slot, m_new, l_new, acc_new) in enumerate(updates):
            m_refs[a][slot] = m_new
            l_refs[a][slot] = l_new
            acc_refs[a][slot] = acc_new
        return carry

    lax.fori_loop(0, nb + 1, sweep_round, 0)
    for a in range(n_streams):
        for slot, i in ((0, a), (1, nb - 1 - a)):
            o_ref[:, i * blk:(i + 1) * blk] = (acc_refs[a][slot] / l_refs[a][slot]).astype(o_ref.dtype)


def _moba_prompt(rel_bias, qT, khm, vT, kmean, *, batch, seq):
    nb = seq // MOBA_BLOCK
    assert nb % 2 == 0
    t = batch * seq
    stream = lambda shape: [pltpu.VMEM((2,) + shape, F32)] * (nb // 2)
    return pl.pallas_call(
        _moba_prompt_kernel,
        grid=(N_HEADS, batch),
        in_specs=[
            pl.BlockSpec(memory_space=pltpu.SMEM),
            pl.BlockSpec((HEAD_DIM, seq), lambda h, b: (h, b)),
            pl.BlockSpec((1, nb, MOBA_BLOCK, HEAD_DIM), lambda h, b: (h, b, 0, 0)),
            pl.BlockSpec((nb, HEAD_DIM, MOBA_BLOCK), lambda h, b: (b, h, 0)),
            pl.BlockSpec((1, nb, HEAD_DIM), lambda h, b: (b * N_HEADS + h, 0, 0)),
        ],
        out_specs=pl.BlockSpec((HEAD_DIM, seq), lambda h, b: (h, b)),
        out_shape=jax.ShapeDtypeStruct((ATT_W, t), MXU_DTYPE),
        scratch_shapes=[pltpu.VMEM((nb, nb, MOBA_BLOCK), F32),
                        pltpu.VMEM((3, MOBA_BLOCK, MOBA_BLOCK), F32),
                        pltpu.VMEM((nb, HEAD_DIM, MOBA_BLOCK), MXU_DTYPE)]
        + stream((1, MOBA_BLOCK)) + stream((1, MOBA_BLOCK)) + stream((HEAD_DIM, MOBA_BLOCK)),
        compiler_params=pltpu.CompilerParams(dimension_semantics=("arbitrary",) * 2,
                                             vmem_limit_bytes=VMEM_LIMIT),
        name="moba_prompt",
    )(rel_bias, qT, khm, vT, kmean)


def _moba_sample_kernel(pt_ref, q_ref, kn_ref, vn_ref, tab_ref, *refs, tq, nb_past, cb):
    del pt_ref
    npg = cb * PAGES_PER_BLOCK
    k_pages = refs[:npg]
    v_pages = refs[npg:2 * npg]
    o_ref, km_ref, m_ref, l_ref, acc_ref = refs[2 * npg:]
    c = pl.program_id(1)
    n_chunks = nb_past // cb
    rows = N_HEADS * tq
    width = q_ref.shape[1]
    r_id = lax.broadcasted_iota(I32, (rows, width), 0)
    lane = lax.broadcasted_iota(I32, (rows, width), 1)
    head_mask = (lane // HEAD_DIM) == (r_id // tq)
    q_rows = jnp.concatenate([q_ref[...]] * N_HEADS, axis=0)
    qbd = jnp.where(head_mask, q_rows, 0.0)
    qbd_b = qbd.astype(MXU_DTYPE)
    tab = tab_ref[...]

    def bias_rows(dist):
        bucket = _bucket(dist)
        out = jnp.zeros(dist.shape, F32)
        for b in range(NUM_BUCKETS):
            out = jnp.where(bucket == b, tab[:, b:b + 1], out)
        return out

    c_far = tab[:, NUM_BUCKETS - 1:NUM_BUCKETS]

    @pl.when(c == 0)
    def _():
        km_ref[...] = jnp.zeros(km_ref.shape, F32)
        m_ref[...] = jnp.full(m_ref.shape, NEG, F32)
        l_ref[...] = jnp.zeros(l_ref.shape, F32)

    km_lane = lax.broadcasted_iota(I32, km_ref.shape, 1)
    st_lane = lax.broadcasted_iota(I32, m_ref.shape, 1)
    for j in range(cb):
        n = c * cb + j
        kts = [k_pages[PAGES_PER_BLOCK * j + g][0] for g in range(PAGES_PER_BLOCK)]
        vts = [v_pages[PAGES_PER_BLOCK * j + g][0] for g in range(PAGES_PER_BLOCK)]
        ksum = kts[0]
        for kt in kts[1:]:
            ksum = ksum + kt
        kmean = jnp.sum(ksum, axis=1, keepdims=True) * (1.0 / MOBA_BLOCK)
        km_ref[...] = jnp.where(km_lane == n, kmean, km_ref[...])
        s = jnp.concatenate([jnp.dot(qbd_b, kt.astype(MXU_DTYPE), preferred_element_type=F32) for kt in kts],
                            axis=1)
        if j == cb - 1:
            qi = lax.broadcasted_iota(I32, (rows, MOBA_BLOCK), 0) % tq
            kj = lax.broadcasted_iota(I32, (rows, MOBA_BLOCK), 1)
            near = bias_rows(MOBA_BLOCK + qi - kj)
            s = s + jnp.where(c == n_chunks - 1, near, c_far)
        else:
            s = s + c_far
        m_n = jnp.max(s, axis=1, keepdims=True)
        p = jnp.exp(s - m_n)
        m_ref[...] = jnp.where(st_lane == n, m_n, m_ref[...])
        l_ref[...] = jnp.where(st_lane == n, jnp.sum(p, axis=1, keepdims=True), l_ref[...])
        pb = p.astype(MXU_DTYPE)
        acc = None
        for g, vt in enumerate(vts):
            part = lax.dot_general(pb[:, g * PAGE_SIZE:(g + 1) * PAGE_SIZE], vt.astype(MXU_DTYPE), _nt_dims(),
                                   preferred_element_type=F32)
            acc = part if acc is None else acc + part
        acc_ref[n] = acc

    @pl.when(c == n_chunks - 1)
    def _():
        lanes = LANES
        nbp = -(-nb_past // SUBLANES) * SUBLANES
        q_pad = jnp.concatenate([qbd, jnp.zeros((lanes - rows, width), F32)], axis=0)
        gT = lax.dot_general(km_ref[...].T, q_pad, _nt_dims(), preferred_element_type=F32,
                             precision=HIGHEST)[:nbp]
        nid = lax.broadcasted_iota(I32, gT.shape, 0)
        cnt = jnp.zeros(gT.shape, I32)
        for m in range(nb_past):
            row = gT[m:m + 1, :]
            beats = (row > gT) | ((row == gT) & (m < nid))
            cnt = cnt + jnp.where(beats, 1, 0)
        selT = jnp.where((cnt < MOBA_TOPK) & (nid < nb_past), 1.0, 0.0)
        selT = jnp.concatenate([selT, jnp.zeros((lanes - nbp, lanes), F32)], axis=0)
        sel = selT.T[:rows] > 0.5
        pad = jnp.zeros((PAGE_SIZE - tq, width), F32)
        kn = jnp.concatenate([kn_ref[...], pad], axis=0)
        vn = jnp.concatenate([vn_ref[...], pad], axis=0)
        qi = lax.broadcasted_iota(I32, (rows, PAGE_SIZE), 0) % tq
        kj = lax.broadcasted_iota(I32, (rows, PAGE_SIZE), 1)
        s = lax.dot_general(qbd_b, kn.astype(MXU_DTYPE), _nt_dims(), preferred_element_type=F32)
        s = jnp.where(kj <= qi, s + bias_rows(qi - kj), NEG)
        m_o = jnp.max(s, axis=1, keepdims=True)
        p = jnp.exp(s - m_o)
        l_o = jnp.sum(p, axis=1, keepdims=True)
        acc_o = jnp.dot(p.astype(MXU_DTYPE), vn.astype(MXU_DTYPE), preferred_element_type=F32)
        m_blk = m_ref[...]
        m_all = jnp.maximum(m_o, jnp.max(jnp.where(sel, m_blk, NEG), axis=1, keepdims=True))
        w_blk = jnp.where(sel, jnp.exp(m_blk - m_all), 0.0)
        w_o = jnp.exp(m_o - m_all)
        l_all = w_o * l_o + jnp.sum(w_blk * l_ref[...], axis=1, keepdims=True)
        acc_all = w_o * acc_o
        for n in range(nb_past):
            acc_all = acc_all + w_blk[:, n:n + 1] * acc_ref[n]
        out_bd = jnp.where(head_mask, acc_all / l_all, 0.0)
        out = out_bd[0:tq]
        for h in range(1, N_HEADS):
            out = out + out_bd[h * tq:(h + 1) * tq]
        o_ref[...] = out


def _moba_sample(page_rows, q, k_new, v_new, tab_rows, cache_k, cache_v, *, nseq, tq, nb_past):
    assert nb_past <= LANES and N_HEADS * tq <= LANES
    cb = math.gcd(SAMPLE_CHUNK_BLOCKS, nb_past)
    npg = cb * PAGES_PER_BLOCK
    n_chunks = nb_past // cb
    ppseq = nb_past * PAGES_PER_BLOCK
    rows = N_HEADS * tq
    width = q.shape[1]
    tok = lambda b, c, pt: (b, 0)

    def page_spec(g):
        return pl.BlockSpec((1, width, PAGE_SIZE), lambda b, c, pt: (pt[b * ppseq + c * npg + g], 0, 0))

    grid_spec = pltpu.PrefetchScalarGridSpec(
        num_scalar_prefetch=1,
        grid=(nseq, n_chunks),
        in_specs=[pl.BlockSpec((tq, width), tok)] * 3
        + [pl.BlockSpec(tab_rows.shape, lambda b, c, pt: (0, 0))]
        + [page_spec(g) for g in range(npg)] * 2,
        out_specs=pl.BlockSpec((tq, width), tok),
        scratch_shapes=[pltpu.VMEM((width, LANES), F32),
                        pltpu.VMEM((rows, LANES), F32),
                        pltpu.VMEM((rows, LANES), F32),
                        pltpu.VMEM((nb_past, rows, width), F32)],
    )
    return pl.pallas_call(
        functools.partial(_moba_sample_kernel, tq=tq, nb_past=nb_past, cb=cb),
        grid_spec=grid_spec,
        out_shape=jax.ShapeDtypeStruct((nseq * tq, width), F32),
        compiler_params=pltpu.CompilerParams(dimension_semantics=("arbitrary",) * 2,
                                             vmem_limit_bytes=VMEM_LIMIT),
        name="moba_sample",
    )(page_rows, q, k_new, v_new, tab_rows, *([cache_k] * npg), *([cache_v] * npg))


def _mix_kernel(*refs, tm, seq_len, attn_transposed, has_state, alpha):
    if has_state:
        (x_ref, attn_ref, u_ref, halo_ref, st0_ref, st1_ref, cb_ref, ga_ref, gb_ref, cw_ref,
         wao_ref, wco_ref, wo_ref, g_ref, b_ref, x1_ref, ubuf) = refs
    else:
        (x_ref, attn_ref, u_ref, halo_ref, cb_ref, ga_ref, gb_ref, cw_ref,
         wao_ref, wco_ref, wo_ref, g_ref, b_ref, x1_ref, ubuf) = refs
    i = pl.program_id(0)
    if attn_transposed:
        attn = attn_ref[...].astype(F32).T.astype(MXU_DTYPE)
    else:
        attn = attn_ref[...].astype(MXU_DTYPE)
    y_att = jnp.dot(attn, wao_ref[...], preferred_element_type=F32)
    u = u_ref[...]
    ubuf[0:SUBLANES, :] = halo_ref[...]
    ubuf[SUBLANES:SUBLANES + tm, :] = u
    p1 = ubuf[SUBLANES - 1:SUBLANES - 1 + tm, :]
    p2 = ubuf[SUBLANES - 2:SUBLANES - 2 + tm, :]
    row = lax.broadcasted_iota(I32, (tm, 1), 0)
    if seq_len % tm == 0:
        pos = (i * tm) % seq_len + row
    else:
        pos = row % seq_len
    s0 = st0_ref[...] if has_state else 0.0
    s1 = st1_ref[...] if has_state else 0.0
    prev1 = jnp.where(pos >= 1, p1, s1)
    prev2 = jnp.where(pos >= 2, p2, jnp.where(pos == 1, s1, s0))
    conv = cw_ref[0:1, :] * prev2 + cw_ref[1:2, :] * prev1 + cw_ref[2:3, :] * u
    y_conv = jnp.dot((cb_ref[...].astype(F32) * conv).astype(MXU_DTYPE), wco_ref[...],
                     preferred_element_type=F32)
    merged = ga_ref[...].astype(F32) * y_att + gb_ref[...].astype(F32) * y_conv
    z = alpha * x_ref[...] + jnp.dot(merged.astype(MXU_DTYPE), wo_ref[...], preferred_element_type=F32)
    x1 = _layer_norm(z, g_ref[...], b_ref[...])
    x1_ref[...] = x1


def _mix(x2d, attn, u, state, cb, ga, gb, conv_w, wao, wco, wo, g, b, *, tm, seq_len, attn_transposed, alpha):
    t, d = x2d.shape
    cw = u.shape[1]
    assert seq_len % tm == 0 or tm % seq_len == 0
    nt = t // tm
    row = lambda i: (i, 0)
    full = lambda a: pl.BlockSpec(a.shape, lambda i: (0,) * a.ndim)
    has_state = state is not None
    hb = tm // SUBLANES
    attn_spec = (pl.BlockSpec((ATT_W, tm), lambda i: (0, i)) if attn_transposed
                 else pl.BlockSpec((tm, ATT_W), row))
    args = [x2d, attn, u, u]
    in_specs = [pl.BlockSpec((tm, d), row), attn_spec, pl.BlockSpec((tm, cw), row),
                pl.BlockSpec((SUBLANES, cw), lambda i: (jnp.maximum(i * hb - 1, 0), 0))]
    if has_state:
        args += list(state)
        in_specs += [pl.BlockSpec((tm, cw), row)] * 2
    args += [cb, ga, gb, conv_w, wao, wco, wo, g, b]
    in_specs += [pl.BlockSpec((tm, cw), row), pl.BlockSpec((tm, d), row), pl.BlockSpec((tm, d), row),
                 full(conv_w), full(wao), full(wco), full(wo), full(g), full(b)]
    return pl.pallas_call(
        functools.partial(_mix_kernel, tm=tm, seq_len=seq_len, attn_transposed=attn_transposed,
                          has_state=has_state, alpha=alpha),
        grid=(nt,),
        in_specs=in_specs,
        out_specs=pl.BlockSpec((tm, d), row),
        out_shape=jax.ShapeDtypeStruct((t, d), F32),
        scratch_shapes=[pltpu.VMEM((SUBLANES + tm, cw), F32)],
        compiler_params=pltpu.CompilerParams(dimension_semantics=("arbitrary",),
                                             vmem_limit_bytes=VMEM_LIMIT),
        name="mix",
    )(*args)


def _router_kernel(x_ref, wT_ref, b_ref, idx_ref, gate_ref, rank_ref, cnt_ref, seen_ref):
    i = pl.program_id(0)
    ne, tm = wT_ref.shape[0], x_ref.shape[0]

    @pl.when(i == 0)
    def _():
        seen_ref[...] = jnp.zeros(seen_ref.shape, F32)

    logits = lax.dot_general(wT_ref[...], x_ref[...], _nt_dims(), preferred_element_type=F32,
                             precision=HIGHEST) + b_ref[...]
    eid = lax.broadcasted_iota(I32, (ne, tm), 0)
    cnt = jnp.zeros((ne, tm), I32)
    for e in range(ne):
        row = logits[e:e + 1, :]
        beats = (row > logits) | ((row == logits) & (e < eid))
        cnt = cnt + jnp.where(beats, 1, 0)
    sel = cnt < TOP_K
    lmax = jnp.max(logits, axis=0, keepdims=True)
    ex = jnp.where(sel, jnp.exp(logits - lmax), 0.0)
    gates = ex / jnp.sum(ex, axis=0, keepdims=True)
    self_ = jnp.where(sel, 1.0, 0.0)
    tt = lax.broadcasted_iota(I32, (tm, tm), 0) < lax.broadcasted_iota(I32, (tm, tm), 1)
    earlier = jnp.where(tt, 1.0, 0.0).astype(MXU_DTYPE)
    seen = seen_ref[:, 0:1]
    rank = jnp.dot(self_.astype(MXU_DTYPE), earlier, preferred_element_type=F32) + seen
    seen = seen + jnp.sum(self_, axis=1, keepdims=True)
    seen_ref[...] = jnp.broadcast_to(seen, seen_ref.shape)
    cnt_ref[...] = jnp.broadcast_to(seen, cnt_ref.shape)
    ee = lax.broadcasted_iota(I32, (ne, ne), 1) < lax.broadcasted_iota(I32, (ne, ne), 0)
    below = jnp.dot(jnp.where(ee, 1.0, 0.0).astype(MXU_DTYPE), self_.astype(MXU_DTYPE),
                    preferred_element_type=F32)
    eidf = eid.astype(F32)
    for kk in range(TOP_K):
        hit = sel & (below == float(kk))
        idx_ref[kk:kk + 1, :] = jnp.sum(jnp.where(hit, eidf, 0.0), axis=0, keepdims=True).astype(I32)
        gate_ref[kk:kk + 1, :] = jnp.sum(jnp.where(hit, gates, 0.0), axis=0, keepdims=True)
        rank_ref[kk:kk + 1, :] = jnp.sum(jnp.where(hit, rank, 0.0), axis=0, keepdims=True).astype(I32)


def _route(x1, router_wT, router_b_col, *, tm):
    t, d = x1.shape
    ne = router_wT.shape[0]
    nt = t // tm
    col = lambda i: (0, i)
    return pl.pallas_call(
        _router_kernel,
        grid=(nt,),
        in_specs=[pl.BlockSpec((tm, d), lambda i: (i, 0)),
                  pl.BlockSpec((ne, d), lambda i: (0, 0)),
                  pl.BlockSpec((ne, 1), lambda i: (0, 0))],
        out_specs=[pl.BlockSpec((TOP_K, tm), col), pl.BlockSpec((TOP_K, tm), col),
                   pl.BlockSpec((TOP_K, tm), col), pl.BlockSpec((ne, 128), lambda i: (0, 0))],
        out_shape=[jax.ShapeDtypeStruct((TOP_K, t), I32), jax.ShapeDtypeStruct((TOP_K, t), F32),
                   jax.ShapeDtypeStruct((TOP_K, t), I32), jax.ShapeDtypeStruct((ne, 128), F32)],
        scratch_shapes=[pltpu.VMEM((ne, 128), F32)],
        compiler_params=pltpu.CompilerParams(dimension_semantics=("arbitrary",),
                                             vmem_limit_bytes=VMEM_LIMIT),
        name="router",
    )(x1, router_wT, router_b_col)


def _dispatch_kernel(dest_ref, pend_ref, x_ref, xb_ref, zeros_ref, sem, zsem, *, t, tm, tile, ne):
    i = pl.program_id(0)

    def row_copy(r, dst):
        return pltpu.make_async_copy(x_ref.at[pl.ds(r, 1)], xb_ref.at[pl.ds(dst, 1)], sem)

    def clear_copy(e):
        start = pl.multiple_of(pend_ref[e] - tile, tile)
        return pltpu.make_async_copy(zeros_ref, xb_ref.at[pl.ds(start, tile)], zsem)

    @pl.when(i == 0)
    def _():
        zeros_ref[...] = jnp.zeros(zeros_ref.shape, F32)
        for phase in ("start", "wait"):
            for e in range(ne):
                has_rows = pend_ref[e] > (pend_ref[e - 1] if e else 0)

                @pl.when(has_rows)
                def _():
                    if phase == "start":
                        clear_copy(e).start()
                    else:
                        clear_copy(e).wait()

        def tail_copy(j):
            return pltpu.make_async_copy(zeros_ref, xb_ref.at[pl.ds(pl.multiple_of(j * tile, tile), tile)], zsem)

        first_unused = pend_ref[ne - 1] // tile
        n_tiles = xb_ref.shape[0] // tile
        lax.fori_loop(first_unused, n_tiles, lambda j, c: (tail_copy(j).start(), c)[1], 0)
        lax.fori_loop(first_unused, n_tiles, lambda j, c: (tail_copy(j).wait(), c)[1], 0)

    def issue(r, carry):
        for kk in range(TOP_K):
            row_copy(r, dest_ref[kk * t + i * tm + r]).start(priority=kk % 2)
        return carry

    lax.fori_loop(0, tm, issue, 0, unroll=DISPATCH_UNROLL)
    for kk in range(TOP_K):
        pltpu.make_async_copy(x_ref, xb_ref.at[pl.ds(0, tm)], sem).wait()


def _dispatch(dest_flat, pad_end, x1, *, rows, tm, tile):
    t, d = x1.shape
    ne = pad_end.shape[0]
    grid_spec = pltpu.PrefetchScalarGridSpec(
        num_scalar_prefetch=2,
        grid=(t // tm,),
        in_specs=[pl.BlockSpec((tm, d), lambda i, dest, pend: (i, 0))],
        out_specs=pl.BlockSpec(memory_space=pl.ANY),
        scratch_shapes=[pltpu.VMEM((tile, d), F32), pltpu.SemaphoreType.DMA, pltpu.SemaphoreType.DMA],
    )
    return pl.pallas_call(
        functools.partial(_dispatch_kernel, t=t, tm=tm, tile=tile, ne=ne),
        grid_spec=grid_spec,
        out_shape=jax.ShapeDtypeStruct((rows, d), F32),
        compiler_params=pltpu.CompilerParams(dimension_semantics=("arbitrary",),
                                             vmem_limit_bytes=VMEM_LIMIT),
        name="dispatch",
    )(dest_flat, pad_end, x1)


def _expert_kernel(te_ref, nu_ref, x_ref, wgu_ref, bg_ref, bu_ref, wd_ref, bd_ref, y_ref, wg_s, wu_s, wd_s):
    i = pl.program_id(0)
    new_expert = jnp.logical_or(i == 0, te_ref[i] != te_ref[jnp.maximum(i - 1, 0)])

    @pl.when(jnp.logical_and(new_expert, i < nu_ref[0]))
    def _():
        half = LANES
        c = lax.broadcasted_iota(I32, (2 * half, 2 * half), 0)
        o = lax.broadcasted_iota(I32, (2 * half, 2 * half), 1)
        src = jnp.where(o < half, 2 * o, 2 * (o - half) + 1)
        perm = jnp.where(c == src, 1.0, 0.0).astype(MXU_DTYPE)
        for j in range(wgu_ref.shape[2] // (2 * half)):
            wb = wgu_ref[0, :, 2 * half * j:2 * half * (j + 1)].astype(MXU_DTYPE)
            y = jnp.dot(wb, perm, preferred_element_type=F32).astype(MXU_DTYPE)
            wg_s[:, half * j:half * (j + 1)] = y[:, :half]
            wu_s[:, half * j:half * (j + 1)] = y[:, half:]
        wd_s[...] = wd_ref[0].astype(MXU_DTYPE)

    @pl.when(i < nu_ref[0])
    def _():
        x = x_ref[...].astype(MXU_DTYPE)
        hg = jnp.dot(x, wg_s[...], preferred_element_type=F32) + bg_ref[0]
        hu = jnp.dot(x, wu_s[...], preferred_element_type=F32) + bu_ref[0]
        g = jnp.minimum(hg, SWIGLU_LIMIT)
        u = jnp.clip(hu, -SWIGLU_LIMIT, SWIGLU_LIMIT)
        a = g * _sigmoid(SWIGLU_ALPHA * g) * (u + 1.0)
        y_ref[...] = jnp.dot(a.astype(MXU_DTYPE), wd_s[...], preferred_element_type=F32) + bd_ref[0]

    @pl.when(i >= nu_ref[0])
    def _():
        y_ref[...] = jnp.zeros(y_ref.shape, F32)


def _experts(tile_e, n_used, xb, w_gate_up, bg, bu, w_down, bd, *, tm):
    rows, d = xb.shape
    dff = w_down.shape[1]
    nt = rows // tm
    ew = lambda i, te, nu: (te[i], 0, 0)
    grid_spec = pltpu.PrefetchScalarGridSpec(
        num_scalar_prefetch=2,
        grid=(nt,),
        in_specs=[pl.BlockSpec((tm, d), lambda i, te, nu: (jnp.minimum(i, nu[0] - 1), 0)),
                  pl.BlockSpec((1, d, 2 * dff), ew),
                  pl.BlockSpec((1, 1, dff), ew), pl.BlockSpec((1, 1, dff), ew),
                  pl.BlockSpec((1, dff, d), ew), pl.BlockSpec((1, 1, d), ew)],
        out_specs=pl.BlockSpec((tm, d), lambda i, te, nu: (i, 0)),
        scratch_shapes=[pltpu.VMEM((d, dff), MXU_DTYPE), pltpu.VMEM((d, dff), MXU_DTYPE),
                        pltpu.VMEM((dff, d), MXU_DTYPE)],
    )
    return pl.pallas_call(
        _expert_kernel,
        grid_spec=grid_spec,
        out_shape=jax.ShapeDtypeStruct((rows, d), F32),
        compiler_params=pltpu.CompilerParams(dimension_semantics=("arbitrary",),
                                             vmem_limit_bytes=VMEM_LIMIT),
        name="experts",
    )(tile_e, n_used, xb, w_gate_up, bg, bu, w_down, bd)


def _norm2_kernel(x1_ref, gate_ref, *refs, alpha):
    y_refs, (g_ref, b_ref, o_ref) = refs[:TOP_K], refs[TOP_K:]
    gate = gate_ref[...]
    ffn = gate[:, 0:1] * y_refs[0][...]
    for kk in range(1, TOP_K):
        ffn = ffn + gate[:, kk:kk + 1] * y_refs[kk][...]
    o_ref[...] = _layer_norm(alpha * x1_ref[...] + ffn, g_ref[...], b_ref[...])


def _norm2(x1_all, gate_t, y_pairs, g, b, *, row0, nrows, tm, alpha):
    d = x1_all.shape[1]
    off = row0 // tm
    src = lambda i: (i + off, 0)
    return pl.pallas_call(
        functools.partial(_norm2_kernel, alpha=alpha),
        grid=(nrows // tm,),
        in_specs=[pl.BlockSpec((tm, d), src), pl.BlockSpec((tm, TOP_K), src)]
        + [pl.BlockSpec((tm, d), src)] * TOP_K
        + [pl.BlockSpec((1, d), lambda i: (0, 0)), pl.BlockSpec((1, d), lambda i: (0, 0))],
        out_specs=pl.BlockSpec((tm, d), lambda i: (i, 0)),
        out_shape=jax.ShapeDtypeStruct((nrows, d), F32),
        compiler_params=pltpu.CompilerParams(dimension_semantics=("arbitrary",),
                                             vmem_limit_bytes=VMEM_LIMIT),
        name="norm2",
    )(x1_all, gate_t, *y_pairs, g, b)


def _moe(x1_all, router_w, router_b, w_gate_up, b_gate_up, w_down, b_down):
    t, d = x1_all.shape
    ne = router_w.shape[1]
    idx, gate, rank, counts = _route(x1_all, router_w.T, router_b.reshape(ne, 1), tm=TOKEN_TILE)
    tm = EXPERT_TILE
    counts = counts[:, 0].astype(I32)
    padded = (counts + tm - 1) // tm * tm
    pad_end = jnp.cumsum(padded)
    pad_start = pad_end - padded
    n_tiles = (t * TOP_K + ne * (tm - 1)) // tm
    rows = n_tiles * tm
    eids = jnp.arange(ne, dtype=I32)
    start_of = jnp.sum(jnp.where(idx[:, :, None] == eids, pad_start, 0), axis=-1)
    dest = start_of + rank
    tile_row = jnp.arange(n_tiles, dtype=I32) * tm
    tile_e = jnp.minimum(jnp.sum((pad_end[None, :] <= tile_row[:, None]).astype(I32), axis=1), ne - 1)
    n_used = (pad_end[-1:] // tm).astype(I32)
    xb = _dispatch(dest.reshape(-1), pad_end.astype(I32), x1_all, rows=rows, tm=TOKEN_TILE, tile=tm)
    bg = b_gate_up[:, None, 0::2]
    bu = b_gate_up[:, None, 1::2]
    yb = _experts(tile_e, n_used, xb, w_gate_up, bg, bu, w_down, b_down[:, None, :], tm=tm)
    return gate.T, [yb[dest[kk]] for kk in range(TOP_K)]


def kernel(x_prompt, x_sample, cache_k, cache_v, state_conv, page_table, rel_bias, w_in, conv_w, w_att_o,
           w_conv_o, w_o, ln1_g, ln1_b, router_w, router_b, w_gate_up, b_gate_up, w_down, b_down, ln2_g, ln2_b):
    depth = w_in.shape[0]
    batch, seq, d = x_prompt.shape
    nseq, tq, _ = x_sample.shape
    n_pool = cache_k.shape[1]
    ppseq = page_table.shape[1]
    past = ppseq * PAGE_SIZE
    assert seq % MOBA_BLOCK == 0 and past % MOBA_BLOCK == 0 and tq % SUBLANES == 0 and tq <= PAGE_SIZE
    nb_past = past // MOBA_BLOCK
    tp, ts = batch * seq, nseq * tq
    alpha = (2 * depth) ** 0.25
    cw = conv_w.shape[2]
    tab_rows = jnp.repeat(rel_bias.T, tq, axis=0)
    ck = cache_k.transpose(0, 1, 3, 4, 2).reshape(depth * n_pool, ATT_W, PAGE_SIZE)
    cv = cache_v.transpose(0, 1, 3, 4, 2).reshape(depth * n_pool, ATT_W, PAGE_SIZE)
    hp = x_prompt.reshape(tp, d)
    hs = x_sample.reshape(ts, d)
    outs = [[] for _ in range(6)]
    for l in range(depth):
        w_in_b = w_in[l].astype(MXU_DTYPE)
        wao, wco, wo = (w_att_o[l].astype(MXU_DTYPE), w_conv_o[l].astype(MXU_DTYPE), w_o[l].astype(MXU_DTYPE))
        g1, b1, g2, b2 = ln1_g[l][None], ln1_b[l][None], ln2_g[l][None], ln2_b[l][None]
        qT, khm, vT, kp, vp, km, up, cbp, gap, gbp = _project(hp, w_in_b, tm=PROJ_TILE, head_major=True, seq=seq)
        nbt = PROJ_TILE // MOBA_BLOCK
        kmean = km[:, :nbt].reshape(batch, seq // MOBA_BLOCK, N_HEADS, HEAD_DIM).transpose(0, 2, 1, 3)
        kmean = kmean.reshape(batch * N_HEADS, seq // MOBA_BLOCK, HEAD_DIM)
        attn_p = _moba_prompt(rel_bias, qT, khm, vT, kmean, batch=batch, seq=seq)
        x1p = _mix(hp, attn_p, up, None, cbp, gap, gbp, conv_w[l], wao, wco, wo, g1, b1,
                         tm=PROJ_TILE, seq_len=seq, attn_transposed=True, alpha=alpha)
        qs, ks, vs, us, cbs, gas, gbs = _project(hs, w_in_b, tm=ts, head_major=False)
        pages = (page_table + l * n_pool).reshape(-1).astype(I32)
        attn_s = _moba_sample(pages, qs, ks, vs, tab_rows, ck, cv, nseq=nseq, tq=tq, nb_past=nb_past)
        state = (jnp.repeat(state_conv[l][:, 0], tq, axis=0), jnp.repeat(state_conv[l][:, 1], tq, axis=0))
        x1s = _mix(hs, attn_s, us, state, cbs, gas, gbs, conv_w[l], wao, wco, wo, g1, b1,
                         tm=ts, seq_len=tq, attn_transposed=False, alpha=alpha)
        x1_all = jnp.concatenate([x1p, x1s], axis=0)
        gate_t, y_pairs = _moe(x1_all, router_w[l], router_b[l], w_gate_up[l], b_gate_up[l], w_down[l], b_down[l])
        hp = _norm2(x1_all, gate_t, y_pairs, g2, b2, row0=0, nrows=tp, tm=TOKEN_TILE, alpha=alpha)
        hs = _norm2(x1_all, gate_t, y_pairs, g2, b2, row0=tp, nrows=ts, tm=TOKEN_TILE, alpha=alpha)
        outs[0].append(kp.reshape(batch, N_HEADS, HEAD_DIM, seq).transpose(0, 3, 1, 2))
        outs[1].append(vp.reshape(batch, N_HEADS, HEAD_DIM, seq).transpose(0, 3, 1, 2))
        outs[2].append(up.reshape(batch, seq, cw)[:, seq - (CONV_K - 1):])
        outs[3].append(ks.reshape(nseq, tq, N_HEADS, HEAD_DIM))
        outs[4].append(vs.reshape(nseq, tq, N_HEADS, HEAD_DIM))
        outs[5].append(us.reshape(nseq, tq, cw)[:, tq - (CONV_K - 1):])
    stacked = [jnp.stack(o) for o in outs]
    return (hp.reshape(batch, seq, d), hs.reshape(nseq, tq, d), *stacked)
```

```python
import functools
import math

import jax
import jax.numpy as jnp
from jax import lax
from jax.experimental import pallas as pl
from jax.experimental.pallas import tpu as pltpu

F32 = jnp.float32
I32 = jnp.int32
MXU_DTYPE = jnp.bfloat16
HIGHEST = lax.Precision.HIGHEST

N_HEADS = 8
HEAD_DIM = 64
ATT_W = N_HEADS * HEAD_DIM
MOBA_BLOCK = 256
MOBA_TOPK = 3
PAGE_SIZE = 128
PAGES_PER_BLOCK = MOBA_BLOCK // PAGE_SIZE
CONV_K = 3
N_EXPERTS = 32
TOP_K = 4
SWIGLU_LIMIT = 7.0
SWIGLU_ALPHA = 1.702
NUM_BUCKETS = 32
NUM_EXACT = NUM_BUCKETS // 2
MAX_DISTANCE = 128
LN_EPS = 1e-5
SCORE_SCALE = HEAD_DIM ** -0.5
NEG = -1e30
V_ROWS = HEAD_DIM + 16

SUBLANES = 8
LANES = 128
PROJ_TILE = 512
TOKEN_TILE = 256
EXPERT_TILE = 512
DISPATCH_UNROLL = 8
SAMPLE_CHUNK_BLOCKS = 8
VMEM_LIMIT = 56 * 1024 * 1024

assert MOBA_BLOCK >= MAX_DISTANCE


def _nt_dims():
    return (((1,), (1,)), ((), ()))


def _sigmoid(x):
    return 1.0 / (1.0 + jnp.exp(-x))


def _bucket(dist):
    n = jnp.maximum(dist, 0)
    nf = jnp.maximum(n, 1).astype(F32)
    large = NUM_EXACT + (jnp.log(nf / NUM_EXACT) / math.log(MAX_DISTANCE / NUM_EXACT)
                         * (NUM_BUCKETS - NUM_EXACT)).astype(I32)
    large = jnp.minimum(large, NUM_BUCKETS - 1)
    return jnp.where(n < NUM_EXACT, n, large)


def _layer_norm(z, g, b):
    mu = jnp.mean(z, axis=-1, keepdims=True)
    zc = z - mu
    var = jnp.mean(zc * zc, axis=-1, keepdims=True)
    return zc * lax.rsqrt(var + LN_EPS) * g + b


def _proj_kernel(x_ref, w_ref, *refs, tm, head_major, blocks_per_seq):
    xb = x_ref[...].astype(MXU_DTYPE)

    def mm(c0, width):
        return jnp.dot(xb, w_ref[:, c0:c0 + width], preferred_element_type=F32)

    if head_major:
        qT_ref, khm_ref, vT_ref, k_ref, v_ref, km_ref, u_ref, cb_ref, ga_ref, gb_ref = refs
        q = mm(0, ATT_W) * SCORE_SCALE
    else:
        wq_ref, q_ref, k_ref, v_ref, u_ref, cb_ref, ga_ref, gb_ref = refs
        q = jnp.dot(x_ref[...], wq_ref[...], preferred_element_type=F32, precision=HIGHEST) * SCORE_SCALE
    d_model = x_ref.shape[1]
    k = mm(ATT_W, ATT_W)
    v = mm(2 * ATT_W, ATT_W)
    if head_major:
        qT_ref[...] = q.T
        k_ref[0] = k.T
        vt = v.T
        v_ref[0] = vt
        km_ref[...] = jnp.zeros(km_ref.shape, F32)
        lane = lax.broadcasted_iota(I32, (MOBA_BLOCK, 2 * HEAD_DIM), 1)
        ones_rows = jnp.ones((V_ROWS - HEAD_DIM, MOBA_BLOCK), MXU_DTYPE)
        for r in range(tm // MOBA_BLOCK):
            kr = k[r * MOBA_BLOCK:(r + 1) * MOBA_BLOCK]
            km_ref[0, r:r + 1, :] = jnp.sum(kr, axis=0, keepdims=True) * (1.0 / MOBA_BLOCK)
            n_blk = (pl.program_id(0) * (tm // MOBA_BLOCK) + r) % blocks_per_seq
            tag = jnp.where(lane - HEAD_DIM == n_blk, 1.0, 0.0)
            vtr = vt[:, r * MOBA_BLOCK:(r + 1) * MOBA_BLOCK]
            for h in range(N_HEADS):
                pair = kr[:, (h // 2) * 2 * HEAD_DIM:(h // 2 + 1) * 2 * HEAD_DIM]
                if h % 2:
                    pair = pltpu.roll(pair, HEAD_DIM, axis=1)
                khm_ref[h, r] = jnp.where(lane < HEAD_DIM, pair, tag).astype(MXU_DTYPE)
                vT_ref[r, h * V_ROWS:h * V_ROWS + HEAD_DIM] = vtr[h * HEAD_DIM:(h + 1) * HEAD_DIM].astype(MXU_DTYPE)
                vT_ref[r, h * V_ROWS + HEAD_DIM:(h + 1) * V_ROWS] = ones_rows
    else:
        q_ref[...] = q
        k_ref[...] = k
        v_ref[...] = v
    c0 = 3 * ATT_W
    cw = u_ref.shape[1]
    cb_ref[...] = mm(c0, cw).astype(cb_ref.dtype)
    u_ref[...] = mm(c0 + cw, cw) * mm(c0 + 2 * cw, cw)
    ga_ref[...] = _sigmoid(mm(c0 + 3 * cw, d_model)).astype(ga_ref.dtype)
    gb_ref[...] = _sigmoid(mm(c0 + 3 * cw + d_model, d_model)).astype(gb_ref.dtype)


def _project(x2d, w_in_b, *, tm, head_major, seq=None, wq=None):
    t, d = x2d.shape
    cw = (w_in_b.shape[1] - 3 * ATT_W - 2 * d) // 3
    nt = t // tm
    row = lambda i: (i, 0)
    f32s = lambda shape: jax.ShapeDtypeStruct(shape, F32)
    mxs = lambda shape: jax.ShapeDtypeStruct(shape, MXU_DTYPE)
    tail_shapes = [f32s((t, cw)), mxs((t, cw)), mxs((t, d)), mxs((t, d))]
    tail_specs = [pl.BlockSpec((tm, cw), row), pl.BlockSpec((tm, cw), row),
                  pl.BlockSpec((tm, d), row), pl.BlockSpec((tm, d), row)]
    if head_major:
        nbt = tm // MOBA_BLOCK
        tps = seq // tm
        seq_t = pl.BlockSpec((1, ATT_W, tm), lambda i: (i // tps, 0, i % tps))
        out_shape = [f32s((ATT_W, t)), mxs((N_HEADS, t // MOBA_BLOCK, MOBA_BLOCK, 2 * HEAD_DIM)),
                     mxs((t // MOBA_BLOCK, N_HEADS * V_ROWS, MOBA_BLOCK)), f32s((t // seq, ATT_W, seq)),
                     f32s((t // seq, ATT_W, seq)), f32s((nt, SUBLANES, ATT_W))] + tail_shapes
        out_specs = [pl.BlockSpec((ATT_W, tm), lambda i: (0, i)),
                     pl.BlockSpec((N_HEADS, nbt, MOBA_BLOCK, 2 * HEAD_DIM), lambda i: (0, i, 0, 0)),
                     pl.BlockSpec((nbt, N_HEADS * V_ROWS, MOBA_BLOCK), lambda i: (i, 0, 0)),
                     seq_t, seq_t,
                     pl.BlockSpec((1, SUBLANES, ATT_W), lambda i: (i, 0, 0))] + tail_specs
        args, extra_specs = (x2d, w_in_b), []
    else:
        out_shape = [f32s((t, ATT_W))] * 3 + tail_shapes
        out_specs = [pl.BlockSpec((tm, ATT_W), row)] * 3 + tail_specs
        args, extra_specs = (x2d, w_in_b, wq), [pl.BlockSpec(wq.shape, lambda i: (0, 0))]
    return pl.pallas_call(
        functools.partial(_proj_kernel, tm=tm, head_major=head_major,
                          blocks_per_seq=seq // MOBA_BLOCK if head_major else None),
        grid=(nt,),
        in_specs=[pl.BlockSpec((tm, d), row), pl.BlockSpec(w_in_b.shape, lambda i: (0, 0))] + extra_specs,
        out_specs=out_specs,
        out_shape=out_shape,
        compiler_params=pltpu.CompilerParams(dimension_semantics=("arbitrary",),
                                             vmem_limit_bytes=VMEM_LIMIT),
        name="proj",
    )(*args)


def _moba_prompt_kernel(tab_ref, qT_ref, k_ref, vT_ref, km_ref, o_ref, bias_ref, qb_ref, *state):
    h = pl.program_id(0)
    first_sequence = pl.program_id(1) == 0
    nb = km_ref.shape[1]
    blk = MOBA_BLOCK
    n_streams = nb // 2
    m_refs, acc_refs = state[:n_streams], state[n_streams:]

    @pl.when(first_sequence)
    def _():
        kk = lax.broadcasted_iota(I32, (blk, blk), 0)
        qq = lax.broadcasted_iota(I32, (blk, blk), 1)
        for age in range(2):
            dist = qq - kk + age * blk
            bucket = _bucket(dist)
            tile = jnp.zeros((blk, blk), F32)
            for b in range(NUM_BUCKETS):
                tile = jnp.where(bucket == b, tab_ref[b, h], tile)
            bias_ref[age] = jnp.where(dist >= 0, tile, NEG)
        bias_ref[2] = jnp.full((blk, blk), tab_ref[NUM_BUCKETS - 1, h], F32)

    km = km_ref[0]
    nid = lax.broadcasted_iota(I32, (nb, blk), 0)
    spare = jnp.zeros((HEAD_DIM - nb, blk), F32)
    for i in range(nb):
        qT = qT_ref[:, i * blk:(i + 1) * blk]
        gT = jnp.dot(km, qT, preferred_element_type=F32, precision=HIGHEST)
        cnt = jnp.zeros((nb, blk), I32)
        for m in range(i):
            row = gT[m:m + 1, :]
            beats = (row > gT) | ((row == gT) & (m < nid))
            cnt = cnt + jnp.where(beats, 1, 0)
        keep = ((cnt < MOBA_TOPK) & (nid < i)) | (nid == i)
        qb_ref[i] = jnp.concatenate([qT, jnp.where(keep, 0.0, NEG), spare], axis=0).astype(MXU_DTYPE)

    for a in range(n_streams):
        m_refs[a][...] = jnp.full(m_refs[a].shape, NEG, F32)
        acc_refs[a][...] = jnp.zeros(acc_refs[a].shape, F32)

    def sweep_round(t, carry):
        staged = []
        for a in range(n_streams):
            first = t <= a
            slot = jnp.where(first, 0, 1)
            i = jnp.where(first, a, nb - 1 - a)
            n = jnp.where(first, a - t, nb - t)
            age = jnp.minimum(i - n, 2)
            s = jnp.dot(k_ref[0, n], qb_ref[i], preferred_element_type=F32) + bias_ref[age]
            staged.append((slot, n, s))
        updates = []
        for a, (slot, n, s) in enumerate(staged):
            m_old = m_refs[a][slot]
            m_new = jnp.maximum(m_old, jnp.max(s, axis=0, keepdims=True))
            alpha = jnp.exp(m_old - m_new)
            p = jnp.exp(s - m_new)
            acc_new = alpha * acc_refs[a][slot] + jnp.dot(
                vT_ref[n], p.astype(MXU_DTYPE), preferred_element_type=F32)
            updates.append((slot, m_new, acc_new))
        for a, (slot, m_new, acc_new) in enumerate(updates):
            m_refs[a][slot] = m_new
            acc_refs[a][slot] = acc_new
        return carry

    lax.fori_loop(0, nb + 1, sweep_round, 0)
    for a in range(n_streams):
        for slot, i in ((0, a), (1, nb - 1 - a)):
            acc = acc_refs[a][slot]
            o_ref[:, i * blk:(i + 1) * blk] = (acc[:HEAD_DIM] / acc[HEAD_DIM:HEAD_DIM + 1]).astype(o_ref.dtype)


def _moba_prompt(rel_bias, qT, khm, vT, kmean, *, batch, seq):
    nb = seq // MOBA_BLOCK
    assert nb % 2 == 0 and nb <= HEAD_DIM
    t = batch * seq
    stream = lambda shape: [pltpu.VMEM((2,) + shape, F32)] * (nb // 2)
    return pl.pallas_call(
        _moba_prompt_kernel,
        grid=(N_HEADS, batch),
        in_specs=[
            pl.BlockSpec(memory_space=pltpu.SMEM),
            pl.BlockSpec((HEAD_DIM, seq), lambda h, b: (h, b)),
            pl.BlockSpec((1, nb, MOBA_BLOCK, 2 * HEAD_DIM), lambda h, b: (h, b, 0, 0)),
            pl.BlockSpec((nb, V_ROWS, MOBA_BLOCK), lambda h, b: (b, h, 0)),
            pl.BlockSpec((1, nb, HEAD_DIM), lambda h, b: (b * N_HEADS + h, 0, 0)),
        ],
        out_specs=pl.BlockSpec((HEAD_DIM, seq), lambda h, b: (h, b)),
        out_shape=jax.ShapeDtypeStruct((ATT_W, t), MXU_DTYPE),
        scratch_shapes=[pltpu.VMEM((3, MOBA_BLOCK, MOBA_BLOCK), F32),
                        pltpu.VMEM((nb, 2 * HEAD_DIM, MOBA_BLOCK), MXU_DTYPE)]
        + stream((1, MOBA_BLOCK)) + stream((V_ROWS, MOBA_BLOCK)),
        compiler_params=pltpu.CompilerParams(dimension_semantics=("arbitrary",) * 2,
                                             vmem_limit_bytes=VMEM_LIMIT),
        name="moba_prompt",
    )(rel_bias, qT, khm, vT, kmean)


def _moba_sample_kernel(pt_ref, q_ref, kn_ref, vn_ref, tab_ref, *refs, tq, nb_past, cb):
    del pt_ref
    npg = cb * PAGES_PER_BLOCK
    k_pages = refs[:npg]
    v_pages = refs[npg:2 * npg]
    o_ref, km_ref, m_ref, l_ref, acc_ref = refs[2 * npg:]
    c = pl.program_id(1)
    n_chunks = nb_past // cb
    rows = N_HEADS * tq
    width = q_ref.shape[1]
    r_id = lax.broadcasted_iota(I32, (rows, width), 0)
    lane = lax.broadcasted_iota(I32, (rows, width), 1)
    head_mask = (lane // HEAD_DIM) == (r_id // tq)
    q_rows = jnp.concatenate([q_ref[...]] * N_HEADS, axis=0)
    qbd = jnp.where(head_mask, q_rows, 0.0)
    qbd_b = qbd.astype(MXU_DTYPE)
    tab = tab_ref[...]

    def bias_rows(dist):
        bucket = _bucket(dist)
        out = jnp.zeros(dist.shape, F32)
        for b in range(NUM_BUCKETS):
            out = jnp.where(bucket == b, tab[:, b:b + 1], out)
        return out

    c_far = tab[:, NUM_BUCKETS - 1:NUM_BUCKETS]

    @pl.when(c == 0)
    def _():
        km_ref[...] = jnp.zeros(km_ref.shape, F32)
        m_ref[...] = jnp.full(m_ref.shape, NEG, F32)
        l_ref[...] = jnp.zeros(l_ref.shape, F32)

    km_lane = lax.broadcasted_iota(I32, km_ref.shape, 1)
    st_lane = lax.broadcasted_iota(I32, m_ref.shape, 1)
    for j in range(cb):
        n = c * cb + j
        kts = [k_pages[PAGES_PER_BLOCK * j + g][0] for g in range(PAGES_PER_BLOCK)]
        vts = [v_pages[PAGES_PER_BLOCK * j + g][0] for g in range(PAGES_PER_BLOCK)]
        ksum = kts[0]
        for kt in kts[1:]:
            ksum = ksum + kt
        kmean = jnp.sum(ksum, axis=1, keepdims=True) * (1.0 / MOBA_BLOCK)
        km_ref[...] = jnp.where(km_lane == n, kmean, km_ref[...])
        s = jnp.concatenate([jnp.dot(qbd_b, kt.astype(MXU_DTYPE), preferred_element_type=F32) for kt in kts],
                            axis=1)
        if j == cb - 1:
            qi = lax.broadcasted_iota(I32, (rows, MOBA_BLOCK), 0) % tq
            kj = lax.broadcasted_iota(I32, (rows, MOBA_BLOCK), 1)
            near = bias_rows(MOBA_BLOCK + qi - kj)
            s = s + jnp.where(c == n_chunks - 1, near, c_far)
        else:
            s = s + c_far
        m_n = jnp.max(s, axis=1, keepdims=True)
        p = jnp.exp(s - m_n)
        m_ref[...] = jnp.where(st_lane == n, m_n, m_ref[...])
        l_ref[...] = jnp.where(st_lane == n, jnp.sum(p, axis=1, keepdims=True), l_ref[...])
        pb = p.astype(MXU_DTYPE)
        acc = None
        for g, vt in enumerate(vts):
            part = lax.dot_general(pb[:, g * PAGE_SIZE:(g + 1) * PAGE_SIZE], vt.astype(MXU_DTYPE), _nt_dims(),
                                   preferred_element_type=F32)
            acc = part if acc is None else acc + part
        acc_ref[n] = acc

    @pl.when(c == n_chunks - 1)
    def _():
        lanes = LANES
        nbp = -(-nb_past // SUBLANES) * SUBLANES
        q_pad = jnp.concatenate([qbd, jnp.zeros((lanes - rows, width), F32)], axis=0)
        gT = lax.dot_general(km_ref[...].T, q_pad, _nt_dims(), preferred_element_type=F32,
                             precision=HIGHEST)[:nbp]
        nid = lax.broadcasted_iota(I32, gT.shape, 0)
        cnt = jnp.zeros(gT.shape, I32)
        for m in range(nb_past):
            row = gT[m:m + 1, :]
            beats = (row > gT) | ((row == gT) & (m < nid))
            cnt = cnt + jnp.where(beats, 1, 0)
        selT = jnp.where((cnt < MOBA_TOPK) & (nid < nb_past), 1.0, 0.0)
        selT = jnp.concatenate([selT, jnp.zeros((lanes - nbp, lanes), F32)], axis=0)
        sel = selT.T[:rows] > 0.5
        pad = jnp.zeros((PAGE_SIZE - tq, width), F32)
        kn = jnp.concatenate([kn_ref[...], pad], axis=0)
        vn = jnp.concatenate([vn_ref[...], pad], axis=0)
        qi = lax.broadcasted_iota(I32, (rows, PAGE_SIZE), 0) % tq
        kj = lax.broadcasted_iota(I32, (rows, PAGE_SIZE), 1)
        s = lax.dot_general(qbd_b, kn.astype(MXU_DTYPE), _nt_dims(), preferred_element_type=F32)
        s = jnp.where(kj <= qi, s + bias_rows(qi - kj), NEG)
        m_o = jnp.max(s, axis=1, keepdims=True)
        p = jnp.exp(s - m_o)
        l_o = jnp.sum(p, axis=1, keepdims=True)
        acc_o = jnp.dot(p.astype(MXU_DTYPE), vn.astype(MXU_DTYPE), preferred_element_type=F32)
        m_blk = m_ref[...]
        m_all = jnp.maximum(m_o, jnp.max(jnp.where(sel, m_blk, NEG), axis=1, keepdims=True))
        w_blk = jnp.where(sel, jnp.exp(m_blk - m_all), 0.0)
        w_o = jnp.exp(m_o - m_all)
        l_all = w_o * l_o + jnp.sum(w_blk * l_ref[...], axis=1, keepdims=True)
        acc_all = w_o * acc_o
        for n in range(nb_past):
            acc_all = acc_all + w_blk[:, n:n + 1] * acc_ref[n]
        out_bd = jnp.where(head_mask, acc_all / l_all, 0.0)
        out = out_bd[0:tq]
        for h in range(1, N_HEADS):
            out = out + out_bd[h * tq:(h + 1) * tq]
        o_ref[...] = out


def _moba_sample(page_rows, q, k_new, v_new, tab_rows, cache_k, cache_v, *, nseq, tq, nb_past):
    assert nb_past <= LANES and N_HEADS * tq <= LANES
    cb = math.gcd(SAMPLE_CHUNK_BLOCKS, nb_past)
    npg = cb * PAGES_PER_BLOCK
    n_chunks = nb_past // cb
    ppseq = nb_past * PAGES_PER_BLOCK
    rows = N_HEADS * tq
    width = q.shape[1]
    tok = lambda b, c, pt: (b, 0)

    def page_spec(g):
        return pl.BlockSpec((1, width, PAGE_SIZE), lambda b, c, pt: (pt[b * ppseq + c * npg + g], 0, 0))

    grid_spec = pltpu.PrefetchScalarGridSpec(
        num_scalar_prefetch=1,
        grid=(nseq, n_chunks),
        in_specs=[pl.BlockSpec((tq, width), tok)] * 3
        + [pl.BlockSpec(tab_rows.shape, lambda b, c, pt: (0, 0))]
        + [page_spec(g) for g in range(npg)] * 2,
        out_specs=pl.BlockSpec((tq, width), tok),
        scratch_shapes=[pltpu.VMEM((width, LANES), F32),
                        pltpu.VMEM((rows, LANES), F32),
                        pltpu.VMEM((rows, LANES), F32),
                        pltpu.VMEM((nb_past, rows, width), F32)],
    )
    return pl.pallas_call(
        functools.partial(_moba_sample_kernel, tq=tq, nb_past=nb_past, cb=cb),
        grid_spec=grid_spec,
        out_shape=jax.ShapeDtypeStruct((nseq * tq, width), F32),
        compiler_params=pltpu.CompilerParams(dimension_semantics=("arbitrary",) * 2,
                                             vmem_limit_bytes=VMEM_LIMIT),
        name="moba_sample",
    )(page_rows, q, k_new, v_new, tab_rows, *([cache_k] * npg), *([cache_v] * npg))


def _mix_kernel(*refs, tm, seq_len, attn_transposed, has_state, alpha):
    if has_state:
        (x_ref, attn_ref, u_ref, halo_ref, st0_ref, st1_ref, cb_ref, ga_ref, gb_ref, cw_ref,
         wao_ref, wco_ref, wo_ref, g_ref, b_ref, x1_ref, ubuf) = refs
    else:
        (x_ref, attn_ref, u_ref, halo_ref, cb_ref, ga_ref, gb_ref, cw_ref,
         wao_ref, wco_ref, wo_ref, g_ref, b_ref, x1_ref, ubuf) = refs
    i = pl.program_id(0)
    if attn_transposed:
        attn = attn_ref[...].astype(F32).T.astype(MXU_DTYPE)
    else:
        attn = attn_ref[...].astype(MXU_DTYPE)
    y_att = jnp.dot(attn, wao_ref[...], preferred_element_type=F32)
    u = u_ref[...]
    ubuf[0:SUBLANES, :] = halo_ref[...]
    ubuf[SUBLANES:SUBLANES + tm, :] = u
    p1 = ubuf[SUBLANES - 1:SUBLANES - 1 + tm, :]
    p2 = ubuf[SUBLANES - 2:SUBLANES - 2 + tm, :]
    row = lax.broadcasted_iota(I32, (tm, 1), 0)
    if seq_len % tm == 0:
        pos = (i * tm) % seq_len + row
    else:
        pos = row % seq_len
    s0 = st0_ref[...] if has_state else 0.0
    s1 = st1_ref[...] if has_state else 0.0
    prev1 = jnp.where(pos >= 1, p1, s1)
    prev2 = jnp.where(pos >= 2, p2, jnp.where(pos == 1, s1, s0))
    conv = cw_ref[0:1, :] * prev2 + cw_ref[1:2, :] * prev1 + cw_ref[2:3, :] * u
    y_conv = jnp.dot((cb_ref[...].astype(F32) * conv).astype(MXU_DTYPE), wco_ref[...],
                     preferred_element_type=F32)
    merged = ga_ref[...].astype(F32) * y_att + gb_ref[...].astype(F32) * y_conv
    z = alpha * x_ref[...] + jnp.dot(merged.astype(MXU_DTYPE), wo_ref[...], preferred_element_type=F32)
    x1 = _layer_norm(z, g_ref[...], b_ref[...])
    x1_ref[...] = x1


def _mix(x2d, attn, u, state, cb, ga, gb, conv_w, wao, wco, wo, g, b, *, tm, seq_len, attn_transposed, alpha):
    t, d = x2d.shape
    cw = u.shape[1]
    assert seq_len % tm == 0 or tm % seq_len == 0
    nt = t // tm
    row = lambda i: (i, 0)
    full = lambda a: pl.BlockSpec(a.shape, lambda i: (0,) * a.ndim)
    has_state = state is not None
    hb = tm // SUBLANES
    attn_spec = (pl.BlockSpec((ATT_W, tm), lambda i: (0, i)) if attn_transposed
                 else pl.BlockSpec((tm, ATT_W), row))
    args = [x2d, attn, u, u]
    in_specs = [pl.BlockSpec((tm, d), row), attn_spec, pl.BlockSpec((tm, cw), row),
                pl.BlockSpec((SUBLANES, cw), lambda i: (jnp.maximum(i * hb - 1, 0), 0))]
    if has_state:
        args += list(state)
        in_specs += [pl.BlockSpec((tm, cw), row)] * 2
    args += [cb, ga, gb, conv_w, wao, wco, wo, g, b]
    in_specs += [pl.BlockSpec((tm, cw), row), pl.BlockSpec((tm, d), row), pl.BlockSpec((tm, d), row),
                 full(conv_w), full(wao), full(wco), full(wo), full(g), full(b)]
    return pl.pallas_call(
        functools.partial(_mix_kernel, tm=tm, seq_len=seq_len, attn_transposed=attn_transposed,
                          has_state=has_state, alpha=alpha),
        grid=(nt,),
        in_specs=in_specs,
        out_specs=pl.BlockSpec((tm, d), row),
        out_shape=jax.ShapeDtypeStruct((t, d), F32),
        scratch_shapes=[pltpu.VMEM((SUBLANES + tm, cw), F32)],
        compiler_params=pltpu.CompilerParams(dimension_semantics=("arbitrary",),
                                             vmem_limit_bytes=VMEM_LIMIT),
        name="mix",
    )(*args)


def _router_kernel(x_ref, wT_ref, b_ref, idx_ref, gate_ref, rank_ref, cnt_ref, seen_ref):
    i = pl.program_id(0)
    ne, tm = wT_ref.shape[0], x_ref.shape[0]

    @pl.when(i == 0)
    def _():
        seen_ref[...] = jnp.zeros(seen_ref.shape, F32)

    logits = lax.dot_general(wT_ref[...], x_ref[...], _nt_dims(), preferred_element_type=F32,
                             precision=HIGHEST) + b_ref[...]
    eid = lax.broadcasted_iota(I32, (ne, tm), 0)
    cnt = jnp.zeros((ne, tm), I32)
    for e in range(ne):
        row = logits[e:e + 1, :]
        beats = (row > logits) | ((row == logits) & (e < eid))
        cnt = cnt + jnp.where(beats, 1, 0)
    sel = cnt < TOP_K
    lmax = jnp.max(logits, axis=0, keepdims=True)
    ex = jnp.where(sel, jnp.exp(logits - lmax), 0.0)
    gates = ex / jnp.sum(ex, axis=0, keepdims=True)
    self_ = jnp.where(sel, 1.0, 0.0)
    tt = lax.broadcasted_iota(I32, (tm, tm), 0) < lax.broadcasted_iota(I32, (tm, tm), 1)
    earlier = jnp.where(tt, 1.0, 0.0).astype(MXU_DTYPE)
    seen = seen_ref[:, 0:1]
    rank = jnp.dot(self_.astype(MXU_DTYPE), earlier, preferred_element_type=F32) + seen
    seen = seen + jnp.sum(self_, axis=1, keepdims=True)
    seen_ref[...] = jnp.broadcast_to(seen, seen_ref.shape)
    cnt_ref[...] = jnp.broadcast_to(seen, cnt_ref.shape)
    ee = lax.broadcasted_iota(I32, (ne, ne), 1) < lax.broadcasted_iota(I32, (ne, ne), 0)
    below = jnp.dot(jnp.where(ee, 1.0, 0.0).astype(MXU_DTYPE), self_.astype(MXU_DTYPE),
                    preferred_element_type=F32)
    eidf = eid.astype(F32)
    for kk in range(TOP_K):
        hit = sel & (below == float(kk))
        idx_ref[kk:kk + 1, :] = jnp.sum(jnp.where(hit, eidf, 0.0), axis=0, keepdims=True).astype(I32)
        gate_ref[kk:kk + 1, :] = jnp.sum(jnp.where(hit, gates, 0.0), axis=0, keepdims=True)
        rank_ref[kk:kk + 1, :] = jnp.sum(jnp.where(hit, rank, 0.0), axis=0, keepdims=True).astype(I32)


def _route(x1, router_wT, router_b_col, *, tm):
    t, d = x1.shape
    ne = router_wT.shape[0]
    nt = t // tm
    col = lambda i: (0, i)
    return pl.pallas_call(
        _router_kernel,
        grid=(nt,),
        in_specs=[pl.BlockSpec((tm, d), lambda i: (i, 0)),
                  pl.BlockSpec((ne, d), lambda i: (0, 0)),
                  pl.BlockSpec((ne, 1), lambda i: (0, 0))],
        out_specs=[pl.BlockSpec((TOP_K, tm), col), pl.BlockSpec((TOP_K, tm), col),
                   pl.BlockSpec((TOP_K, tm), col), pl.BlockSpec((ne, 128), lambda i: (0, 0))],
        out_shape=[jax.ShapeDtypeStruct((TOP_K, t), I32), jax.ShapeDtypeStruct((TOP_K, t), F32),
                   jax.ShapeDtypeStruct((TOP_K, t), I32), jax.ShapeDtypeStruct((ne, 128), F32)],
        scratch_shapes=[pltpu.VMEM((ne, 128), F32)],
        compiler_params=pltpu.CompilerParams(dimension_semantics=("arbitrary",),
                                             vmem_limit_bytes=VMEM_LIMIT),
        name="router",
    )(x1, router_wT, router_b_col)


def _dispatch_kernel(dest_ref, pend_ref, x_ref, xb_ref, zeros_ref, sem, zsem, *, t, tm, tile, ne):
    i = pl.program_id(0)

    def row_copy(r, dst):
        return pltpu.make_async_copy(x_ref.at[pl.ds(r, 1)], xb_ref.at[pl.ds(dst, 1)], sem)

    def clear_copy(e):
        start = pl.multiple_of(pend_ref[e] - tile, tile)
        return pltpu.make_async_copy(zeros_ref, xb_ref.at[pl.ds(start, tile)], zsem)

    @pl.when(i == 0)
    def _():
        zeros_ref[...] = jnp.zeros(zeros_ref.shape, F32)
        for phase in ("start", "wait"):
            for e in range(ne):
                has_rows = pend_ref[e] > (pend_ref[e - 1] if e else 0)

                @pl.when(has_rows)
                def _():
                    if phase == "start":
                        clear_copy(e).start()
                    else:
                        clear_copy(e).wait()

        def tail_copy(j):
            return pltpu.make_async_copy(zeros_ref, xb_ref.at[pl.ds(pl.multiple_of(j * tile, tile), tile)], zsem)

        first_unused = pend_ref[ne - 1] // tile
        n_tiles = xb_ref.shape[0] // tile
        lax.fori_loop(first_unused, n_tiles, lambda j, c: (tail_copy(j).start(), c)[1], 0)
        lax.fori_loop(first_unused, n_tiles, lambda j, c: (tail_copy(j).wait(), c)[1], 0)

    def issue(r, carry):
        for kk in range(TOP_K):
            row_copy(r, dest_ref[kk * t + i * tm + r]).start(priority=kk % 2)
        return carry

    lax.fori_loop(0, tm, issue, 0, unroll=DISPATCH_UNROLL)
    for kk in range(TOP_K):
        pltpu.make_async_copy(x_ref, xb_ref.at[pl.ds(0, tm)], sem).wait()


def _dispatch(dest_flat, pad_end, x1, *, rows, tm, tile):
    t, d = x1.shape
    ne = pad_end.shape[0]
    grid_spec = pltpu.PrefetchScalarGridSpec(
        num_scalar_prefetch=2,
        grid=(t // tm,),
        in_specs=[pl.BlockSpec((tm, d), lambda i, dest, pend: (i, 0))],
        out_specs=pl.BlockSpec(memory_space=pl.ANY),
        scratch_shapes=[pltpu.VMEM((tile, d), F32), pltpu.SemaphoreType.DMA, pltpu.SemaphoreType.DMA],
    )
    return pl.pallas_call(
        functools.partial(_dispatch_kernel, t=t, tm=tm, tile=tile, ne=ne),
        grid_spec=grid_spec,
        out_shape=jax.ShapeDtypeStruct((rows, d), F32),
        compiler_params=pltpu.CompilerParams(dimension_semantics=("arbitrary",),
                                             vmem_limit_bytes=VMEM_LIMIT),
        name="dispatch",
    )(dest_flat, pad_end, x1)


def _expert_kernel(te_ref, nu_ref, x_ref, wgu_ref, bg_ref, bu_ref, wd_ref, bd_ref, y_ref, wg_s, wu_s, wd_s):
    i = pl.program_id(0)
    new_expert = jnp.logical_or(i == 0, te_ref[i] != te_ref[jnp.maximum(i - 1, 0)])

    @pl.when(jnp.logical_and(new_expert, i < nu_ref[0]))
    def _():
        half = LANES
        c = lax.broadcasted_iota(I32, (2 * half, 2 * half), 0)
        o = lax.broadcasted_iota(I32, (2 * half, 2 * half), 1)
        src = jnp.where(o < half, 2 * o, 2 * (o - half) + 1)
        perm = jnp.where(c == src, 1.0, 0.0).astype(MXU_DTYPE)
        for j in range(wgu_ref.shape[2] // (2 * half)):
            wb = wgu_ref[0, :, 2 * half * j:2 * half * (j + 1)].astype(MXU_DTYPE)
            y = jnp.dot(wb, perm, preferred_element_type=F32).astype(MXU_DTYPE)
            wg_s[:, half * j:half * (j + 1)] = y[:, :half]
            wu_s[:, half * j:half * (j + 1)] = y[:, half:]
        wd_s[...] = wd_ref[0].astype(MXU_DTYPE)

    @pl.when(i < nu_ref[0])
    def _():
        x = x_ref[...].astype(MXU_DTYPE)
        hg = jnp.dot(x, wg_s[...], preferred_element_type=F32) + bg_ref[0]
        hu = jnp.dot(x, wu_s[...], preferred_element_type=F32) + bu_ref[0]
        g = jnp.minimum(hg, SWIGLU_LIMIT)
        u = jnp.clip(hu, -SWIGLU_LIMIT, SWIGLU_LIMIT)
        a = g * _sigmoid(SWIGLU_ALPHA * g) * (u + 1.0)
        y_ref[...] = jnp.dot(a.astype(MXU_DTYPE), wd_s[...], preferred_element_type=F32) + bd_ref[0]

    @pl.when(i >= nu_ref[0])
    def _():
        y_ref[...] = jnp.zeros(y_ref.shape, F32)


def _experts(tile_e, n_used, xb, w_gate_up, bg, bu, w_down, bd, *, tm):
    rows, d = xb.shape
    dff = w_down.shape[1]
    nt = rows // tm
    ew = lambda i, te, nu: (te[i], 0, 0)
    grid_spec = pltpu.PrefetchScalarGridSpec(
        num_scalar_prefetch=2,
        grid=(nt,),
        in_specs=[pl.BlockSpec((tm, d), lambda i, te, nu: (jnp.minimum(i, nu[0] - 1), 0)),
                  pl.BlockSpec((1, d, 2 * dff), ew),
                  pl.BlockSpec((1, 1, dff), ew), pl.BlockSpec((1, 1, dff), ew),
                  pl.BlockSpec((1, dff, d), ew), pl.BlockSpec((1, 1, d), ew)],
        out_specs=pl.BlockSpec((tm, d), lambda i, te, nu: (i, 0)),
        scratch_shapes=[pltpu.VMEM((d, dff), MXU_DTYPE), pltpu.VMEM((d, dff), MXU_DTYPE),
                        pltpu.VMEM((dff, d), MXU_DTYPE)],
    )
    return pl.pallas_call(
        _expert_kernel,
        grid_spec=grid_spec,
        out_shape=jax.ShapeDtypeStruct((rows, d), F32),
        compiler_params=pltpu.CompilerParams(dimension_semantics=("arbitrary",),
                                             vmem_limit_bytes=VMEM_LIMIT),
        name="experts",
    )(tile_e, n_used, xb, w_gate_up, bg, bu, w_down, bd)


def _norm2_kernel(x1_ref, gate_ref, *refs, alpha):
    y_refs, (g_ref, b_ref, o_ref) = refs[:TOP_K], refs[TOP_K:]
    gate = gate_ref[...]
    ffn = gate[:, 0:1] * y_refs[0][...]
    for kk in range(1, TOP_K):
        ffn = ffn + gate[:, kk:kk + 1] * y_refs[kk][...]
    o_ref[...] = _layer_norm(alpha * x1_ref[...] + ffn, g_ref[...], b_ref[...])


def _norm2(x1_all, gate_t, y_pairs, g, b, *, row0, nrows, tm, alpha):
    d = x1_all.shape[1]
    off = row0 // tm
    src = lambda i: (i + off, 0)
    return pl.pallas_call(
        functools.partial(_norm2_kernel, alpha=alpha),
        grid=(nrows // tm,),
        in_specs=[pl.BlockSpec((tm, d), src), pl.BlockSpec((tm, TOP_K), src)]
        + [pl.BlockSpec((tm, d), src)] * TOP_K
        + [pl.BlockSpec((1, d), lambda i: (0, 0)), pl.BlockSpec((1, d), lambda i: (0, 0))],
        out_specs=pl.BlockSpec((tm, d), lambda i: (i, 0)),
        out_shape=jax.ShapeDtypeStruct((nrows, d), F32),
        compiler_params=pltpu.CompilerParams(dimension_semantics=("arbitrary",),
                                             vmem_limit_bytes=VMEM_LIMIT),
        name="norm2",
    )(x1_all, gate_t, *y_pairs, g, b)


def _moe(x1_all, router_w, router_b, w_gate_up, b_gate_up, w_down, b_down):
    t, d = x1_all.shape
    ne = router_w.shape[1]
    idx, gate, rank, counts = _route(x1_all, router_w.T, router_b.reshape(ne, 1), tm=TOKEN_TILE)
    tm = EXPERT_TILE
    counts = counts[:, 0].astype(I32)
    padded = (counts + tm - 1) // tm * tm
    pad_end = jnp.cumsum(padded)
    pad_start = pad_end - padded
    n_tiles = (t * TOP_K + ne * (tm - 1)) // tm
    rows = n_tiles * tm
    eids = jnp.arange(ne, dtype=I32)
    start_of = jnp.sum(jnp.where(idx[:, :, None] == eids, pad_start, 0), axis=-1)
    dest = start_of + rank
    tile_row = jnp.arange(n_tiles, dtype=I32) * tm
    tile_e = jnp.minimum(jnp.sum((pad_end[None, :] <= tile_row[:, None]).astype(I32), axis=1), ne - 1)
    n_used = (pad_end[-1:] // tm).astype(I32)
    xb = _dispatch(dest.reshape(-1), pad_end.astype(I32), x1_all, rows=rows, tm=TOKEN_TILE, tile=tm)
    bg = b_gate_up[:, None, 0::2]
    bu = b_gate_up[:, None, 1::2]
    yb = _experts(tile_e, n_used, xb, w_gate_up, bg, bu, w_down, b_down[:, None, :], tm=tm)
    return gate.T, [yb[dest[kk]] for kk in range(TOP_K)]


def kernel(x_prompt, x_sample, cache_k, cache_v, state_conv, page_table, rel_bias, w_in, conv_w, w_att_o,
           w_conv_o, w_o, ln1_g, ln1_b, router_w, router_b, w_gate_up, b_gate_up, w_down, b_down, ln2_g, ln2_b):
    depth = w_in.shape[0]
    batch, seq, d = x_prompt.shape
    nseq, tq, _ = x_sample.shape
    n_pool = cache_k.shape[1]
    ppseq = page_table.shape[1]
    past = ppseq * PAGE_SIZE
    assert seq % MOBA_BLOCK == 0 and past % MOBA_BLOCK == 0 and tq % SUBLANES == 0 and tq <= PAGE_SIZE
    nb_past = past // MOBA_BLOCK
    tp, ts = batch * seq, nseq * tq
    alpha = (2 * depth) ** 0.25
    cw = conv_w.shape[2]
    tab_rows = jnp.repeat(rel_bias.T, tq, axis=0)
    ck = cache_k.transpose(0, 1, 3, 4, 2).reshape(depth * n_pool, ATT_W, PAGE_SIZE)
    cv = cache_v.transpose(0, 1, 3, 4, 2).reshape(depth * n_pool, ATT_W, PAGE_SIZE)
    hp = x_prompt.reshape(tp, d)
    hs = x_sample.reshape(ts, d)
    outs = [[] for _ in range(6)]
    for l in range(depth):
        w_in_b = w_in[l].astype(MXU_DTYPE)
        wao, wco, wo = (w_att_o[l].astype(MXU_DTYPE), w_conv_o[l].astype(MXU_DTYPE), w_o[l].astype(MXU_DTYPE))
        g1, b1, g2, b2 = ln1_g[l][None], ln1_b[l][None], ln2_g[l][None], ln2_b[l][None]
        qT, khm, vT, kp, vp, km, up, cbp, gap, gbp = _project(hp, w_in_b, tm=PROJ_TILE, head_major=True, seq=seq)
        nbt = PROJ_TILE // MOBA_BLOCK
        kmean = km[:, :nbt].reshape(batch, seq // MOBA_BLOCK, N_HEADS, HEAD_DIM).transpose(0, 2, 1, 3)
        kmean = kmean.reshape(batch * N_HEADS, seq // MOBA_BLOCK, HEAD_DIM)
        attn_p = _moba_prompt(rel_bias, qT, khm, vT, kmean, batch=batch, seq=seq)
        x1p = _mix(hp, attn_p, up, None, cbp, gap, gbp, conv_w[l], wao, wco, wo, g1, b1,
                         tm=PROJ_TILE, seq_len=seq, attn_transposed=True, alpha=alpha)
        qs, ks, vs, us, cbs, gas, gbs = _project(hs, w_in_b, tm=ts, head_major=False, wq=w_in[l][:, :ATT_W])
        pages = (page_table + l * n_pool).reshape(-1).astype(I32)
        attn_s = _moba_sample(pages, qs, ks, vs, tab_rows, ck, cv, nseq=nseq, tq=tq, nb_past=nb_past)
        state = (jnp.repeat(state_conv[l][:, 0], tq, axis=0), jnp.repeat(state_conv[l][:, 1], tq, axis=0))
        x1s = _mix(hs, attn_s, us, state, cbs, gas, gbs, conv_w[l], wao, wco, wo, g1, b1,
                         tm=ts, seq_len=tq, attn_transposed=False, alpha=alpha)
        x1_all = jnp.concatenate([x1p, x1s], axis=0)
        gate_t, y_pairs = _moe(x1_all, router_w[l], router_b[l], w_gate_up[l], b_gate_up[l], w_down[l], b_down[l])
        hp = _norm2(x1_all, gate_t, y_pairs, g2, b2, row0=0, nrows=tp, tm=TOKEN_TILE, alpha=alpha)
        hs = _norm2(x1_all, gate_t, y_pairs, g2, b2, row0=tp, nrows=ts, tm=TOKEN_TILE, alpha=alpha)
        outs[0].append(kp.reshape(batch, N_HEADS, HEAD_DIM, seq).transpose(0, 3, 1, 2))
        outs[1].append(vp.reshape(batch, N_HEADS, HEAD_DIM, seq).transpose(0, 3, 1, 2))
        outs[2].append(up.reshape(batch, seq, cw)[:, seq - (CONV_K - 1):])
        outs[3].append(ks.reshape(nseq, tq, N_HEADS, HEAD_DIM))
        outs[4].append(vs.reshape(nseq, tq, N_HEADS, HEAD_DIM))
        outs[5].append(us.reshape(nseq, tq, cw)[:, tq - (CONV_K - 1):])
    stacked = [jnp.stack(o) for o in outs]
    return (hp.reshape(batch, seq, d), hs.reshape(nseq, tq, d), *stacked)
```

```python
import functools
import math

import jax
import jax.numpy as jnp
from jax import lax
from jax.experimental import pallas as pl
from jax.experimental.pallas import tpu as pltpu

F32 = jnp.float32
I32 = jnp.int32
MXU_DTYPE = jnp.bfloat16
HIGHEST = lax.Precision.HIGHEST

N_HEADS = 8
HEAD_DIM = 64
ATT_W = N_HEADS * HEAD_DIM
MOBA_BLOCK = 256
MOBA_TOPK = 3
PAGE_SIZE = 128
PAGES_PER_BLOCK = MOBA_BLOCK // PAGE_SIZE
CONV_K = 3
N_EXPERTS = 32
TOP_K = 4
SWIGLU_LIMIT = 7.0
SWIGLU_ALPHA = 1.702
NUM_BUCKETS = 32
NUM_EXACT = NUM_BUCKETS // 2
MAX_DISTANCE = 128
LN_EPS = 1e-5
SCORE_SCALE = HEAD_DIM ** -0.5
NEG = -1e30
V_ROWS = HEAD_DIM + 16

SUBLANES = 8
LANES = 128
PROJ_TILE = 512
TOKEN_TILE = 256
EXPERT_TILE = 512
DISPATCH_UNROLL = 8
SAMPLE_CHUNK_BLOCKS = 8
VMEM_LIMIT = 56 * 1024 * 1024

assert MOBA_BLOCK >= MAX_DISTANCE


def _nt_dims():
    return (((1,), (1,)), ((), ()))


def _sigmoid(x):
    return 1.0 / (1.0 + jnp.exp(-x))


def _bucket(dist):
    n = jnp.maximum(dist, 0)
    nf = jnp.maximum(n, 1).astype(F32)
    large = NUM_EXACT + (jnp.log(nf / NUM_EXACT) / math.log(MAX_DISTANCE / NUM_EXACT)
                         * (NUM_BUCKETS - NUM_EXACT)).astype(I32)
    large = jnp.minimum(large, NUM_BUCKETS - 1)
    return jnp.where(n < NUM_EXACT, n, large)


def _layer_norm(z, g, b):
    mu = jnp.mean(z, axis=-1, keepdims=True)
    zc = z - mu
    var = jnp.mean(zc * zc, axis=-1, keepdims=True)
    return zc * lax.rsqrt(var + LN_EPS) * g + b


def _proj_kernel(x_ref, w_ref, *refs, tm, head_major, blocks_per_seq):
    xb = x_ref[...].astype(MXU_DTYPE)

    def mm(c0, width):
        return jnp.dot(xb, w_ref[:, c0:c0 + width], preferred_element_type=F32)

    if head_major:
        qT_ref, khm_ref, vT_ref, k_ref, v_ref, km_ref, u_ref, cb_ref, ga_ref, gb_ref = refs
        q = mm(0, ATT_W) * SCORE_SCALE
    else:
        wq_ref, q_ref, k_ref, v_ref, u_ref, cb_ref, ga_ref, gb_ref = refs
        q = jnp.dot(x_ref[...], wq_ref[...], preferred_element_type=F32, precision=HIGHEST) * SCORE_SCALE
    d_model = x_ref.shape[1]
    k = mm(ATT_W, ATT_W)
    v = mm(2 * ATT_W, ATT_W)
    if head_major:
        qT_ref[...] = q.T
        k_ref[0] = k.T
        vt = v.T
        v_ref[0] = vt
        km_ref[...] = jnp.zeros(km_ref.shape, F32)
        lane = lax.broadcasted_iota(I32, (MOBA_BLOCK, 2 * HEAD_DIM), 1)
        ones_rows = jnp.ones((V_ROWS - HEAD_DIM, MOBA_BLOCK), MXU_DTYPE)
        for r in range(tm // MOBA_BLOCK):
            kr = k[r * MOBA_BLOCK:(r + 1) * MOBA_BLOCK]
            km_ref[0, r:r + 1, :] = jnp.sum(kr, axis=0, keepdims=True) * (1.0 / MOBA_BLOCK)
            n_blk = (pl.program_id(0) * (tm // MOBA_BLOCK) + r) % blocks_per_seq
            tag = jnp.where(lane - HEAD_DIM == n_blk, 1.0, 0.0)
            vtr = vt[:, r * MOBA_BLOCK:(r + 1) * MOBA_BLOCK]
            for h in range(N_HEADS):
                pair = kr[:, (h // 2) * 2 * HEAD_DIM:(h // 2 + 1) * 2 * HEAD_DIM]
                if h % 2:
                    pair = pltpu.roll(pair, HEAD_DIM, axis=1)
                khm_ref[h, r] = jnp.where(lane < HEAD_DIM, pair, tag).astype(MXU_DTYPE)
                vT_ref[r, h * V_ROWS:h * V_ROWS + HEAD_DIM] = vtr[h * HEAD_DIM:(h + 1) * HEAD_DIM].astype(MXU_DTYPE)
                vT_ref[r, h * V_ROWS + HEAD_DIM:(h + 1) * V_ROWS] = ones_rows
    else:
        q_ref[...] = q
        k_ref[...] = k
        v_ref[...] = v
    c0 = 3 * ATT_W
    cw = u_ref.shape[1]
    cb_ref[...] = mm(c0, cw).astype(cb_ref.dtype)
    u_ref[...] = mm(c0 + cw, cw) * mm(c0 + 2 * cw, cw)
    ga_ref[...] = _sigmoid(mm(c0 + 3 * cw, d_model)).astype(ga_ref.dtype)
    gb_ref[...] = _sigmoid(mm(c0 + 3 * cw + d_model, d_model)).astype(gb_ref.dtype)


def _project(x2d, w_in_b, *, tm, head_major, seq=None, wq=None):
    t, d = x2d.shape
    cw = (w_in_b.shape[1] - 3 * ATT_W - 2 * d) // 3
    nt = t // tm
    row = lambda i: (i, 0)
    f32s = lambda shape: jax.ShapeDtypeStruct(shape, F32)
    mxs = lambda shape: jax.ShapeDtypeStruct(shape, MXU_DTYPE)
    tail_shapes = [f32s((t, cw)), mxs((t, cw)), mxs((t, d)), mxs((t, d))]
    tail_specs = [pl.BlockSpec((tm, cw), row), pl.BlockSpec((tm, cw), row),
                  pl.BlockSpec((tm, d), row), pl.BlockSpec((tm, d), row)]
    if head_major:
        nbt = tm // MOBA_BLOCK
        tps = seq // tm
        seq_t = pl.BlockSpec((1, ATT_W, tm), lambda i: (i // tps, 0, i % tps))
        out_shape = [f32s((ATT_W, t)), mxs((N_HEADS, t // MOBA_BLOCK, MOBA_BLOCK, 2 * HEAD_DIM)),
                     mxs((t // MOBA_BLOCK, N_HEADS * V_ROWS, MOBA_BLOCK)), f32s((t // seq, ATT_W, seq)),
                     f32s((t // seq, ATT_W, seq)), f32s((nt, SUBLANES, ATT_W))] + tail_shapes
        out_specs = [pl.BlockSpec((ATT_W, tm), lambda i: (0, i)),
                     pl.BlockSpec((N_HEADS, nbt, MOBA_BLOCK, 2 * HEAD_DIM), lambda i: (0, i, 0, 0)),
                     pl.BlockSpec((nbt, N_HEADS * V_ROWS, MOBA_BLOCK), lambda i: (i, 0, 0)),
                     seq_t, seq_t,
                     pl.BlockSpec((1, SUBLANES, ATT_W), lambda i: (i, 0, 0))] + tail_specs
        args, extra_specs = (x2d, w_in_b), []
    else:
        out_shape = [f32s((t, ATT_W))] * 3 + tail_shapes
        out_specs = [pl.BlockSpec((tm, ATT_W), row)] * 3 + tail_specs
        args, extra_specs = (x2d, w_in_b, wq), [pl.BlockSpec(wq.shape, lambda i: (0, 0))]
    return pl.pallas_call(
        functools.partial(_proj_kernel, tm=tm, head_major=head_major,
                          blocks_per_seq=seq // MOBA_BLOCK if head_major else None),
        grid=(nt,),
        in_specs=[pl.BlockSpec((tm, d), row), pl.BlockSpec(w_in_b.shape, lambda i: (0, 0))] + extra_specs,
        out_specs=out_specs,
        out_shape=out_shape,
        compiler_params=pltpu.CompilerParams(dimension_semantics=("arbitrary",),
                                             vmem_limit_bytes=VMEM_LIMIT),
        name="proj",
    )(*args)


def _moba_prompt_kernel(tab_ref, qT_ref, k_ref, vT_ref, km_ref, o_ref, bias_ref, qb_ref, *state):
    h = pl.program_id(0)
    first_sequence = pl.program_id(1) == 0
    nb = km_ref.shape[1]
    blk = MOBA_BLOCK
    n_streams = nb // 2
    m_refs, acc_refs = state[:n_streams], state[n_streams:]

    @pl.when(first_sequence)
    def _():
        kk = lax.broadcasted_iota(I32, (blk, blk), 0)
        qq = lax.broadcasted_iota(I32, (blk, blk), 1)
        for age in range(2):
            dist = qq - kk + age * blk
            bucket = _bucket(dist)
            tile = jnp.zeros((blk, blk), F32)
            for b in range(NUM_BUCKETS):
                tile = jnp.where(bucket == b, tab_ref[b, h], tile)
            bias_ref[age] = jnp.where(dist >= 0, tile, NEG)
        bias_ref[2] = jnp.full((blk, blk), tab_ref[NUM_BUCKETS - 1, h], F32)

    km = km_ref[0]
    nid = lax.broadcasted_iota(I32, (nb, blk), 0)
    spare = jnp.zeros((HEAD_DIM - nb, blk), F32)
    for i in range(nb):
        qT = qT_ref[:, i * blk:(i + 1) * blk]
        gT = jnp.dot(km, qT, preferred_element_type=F32, precision=HIGHEST)
        cnt = jnp.zeros((nb, blk), I32)
        for m in range(i):
            row = gT[m:m + 1, :]
            beats = (row > gT) | ((row == gT) & (m < nid))
            cnt = cnt + jnp.where(beats, 1, 0)
        keep = ((cnt < MOBA_TOPK) & (nid < i)) | (nid == i)
        qb_ref[i] = jnp.concatenate([qT, jnp.where(keep, 0.0, NEG), spare], axis=0).astype(MXU_DTYPE)

    for a in range(n_streams):
        m_refs[a][...] = jnp.full(m_refs[a].shape, NEG, F32)
        acc_refs[a][...] = jnp.zeros(acc_refs[a].shape, F32)

    def sweep_round(t, carry):
        staged = []
        for a in range(n_streams):
            first = t <= a
            slot = jnp.where(first, 0, 1)
            i = jnp.where(first, a, nb - 1 - a)
            n = jnp.where(first, a - t, nb - t)
            age = jnp.minimum(i - n, 2)
            s = jnp.dot(k_ref[0, n], qb_ref[i], preferred_element_type=F32) + bias_ref[age]
            staged.append((slot, n, s))
        updates = []
        for a, (slot, n, s) in enumerate(staged):
            m_old = m_refs[a][slot]
            m_new = jnp.maximum(m_old, jnp.max(s, axis=0, keepdims=True))
            alpha = jnp.exp(m_old - m_new)
            p = jnp.exp(s - m_new)
            acc_new = alpha * acc_refs[a][slot] + jnp.dot(
                vT_ref[n], p.astype(MXU_DTYPE), preferred_element_type=F32)
            updates.append((slot, m_new, acc_new))
        for a, (slot, m_new, acc_new) in enumerate(updates):
            m_refs[a][slot] = m_new
            acc_refs[a][slot] = acc_new
        return carry

    lax.fori_loop(0, nb + 1, sweep_round, 0)
    for a in range(n_streams):
        for slot, i in ((0, a), (1, nb - 1 - a)):
            acc = acc_refs[a][slot]
            o_ref[:, i * blk:(i + 1) * blk] = (acc[:HEAD_DIM] / acc[HEAD_DIM:HEAD_DIM + 1]).astype(o_ref.dtype)


def _moba_prompt(rel_bias, qT, khm, vT, kmean, *, batch, seq):
    nb = seq // MOBA_BLOCK
    assert nb % 2 == 0 and nb <= HEAD_DIM
    t = batch * seq
    stream = lambda shape: [pltpu.VMEM((2,) + shape, F32)] * (nb // 2)
    return pl.pallas_call(
        _moba_prompt_kernel,
        grid=(N_HEADS, batch),
        in_specs=[
            pl.BlockSpec(memory_space=pltpu.SMEM),
            pl.BlockSpec((HEAD_DIM, seq), lambda h, b: (h, b)),
            pl.BlockSpec((1, nb, MOBA_BLOCK, 2 * HEAD_DIM), lambda h, b: (h, b, 0, 0)),
            pl.BlockSpec((nb, V_ROWS, MOBA_BLOCK), lambda h, b: (b, h, 0)),
            pl.BlockSpec((1, nb, HEAD_DIM), lambda h, b: (b * N_HEADS + h, 0, 0)),
        ],
        out_specs=pl.BlockSpec((HEAD_DIM, seq), lambda h, b: (h, b)),
        out_shape=jax.ShapeDtypeStruct((ATT_W, t), MXU_DTYPE),
        scratch_shapes=[pltpu.VMEM((3, MOBA_BLOCK, MOBA_BLOCK), F32),
                        pltpu.VMEM((nb, 2 * HEAD_DIM, MOBA_BLOCK), MXU_DTYPE)]
        + stream((1, MOBA_BLOCK)) + stream((V_ROWS, MOBA_BLOCK)),
        compiler_params=pltpu.CompilerParams(dimension_semantics=("arbitrary",) * 2,
                                             vmem_limit_bytes=VMEM_LIMIT),
        name="moba_prompt",
    )(rel_bias, qT, khm, vT, kmean)


def _moba_sample_kernel(pt_ref, q_ref, kn_ref, vn_ref, tab_ref, *refs, tq, nb_past, cb):
    del pt_ref
    npg = cb * PAGES_PER_BLOCK
    k_pages = refs[:npg]
    v_pages = refs[npg:2 * npg]
    o_ref, km_ref, m_ref, l_ref, acc_ref = refs[2 * npg:]
    c = pl.program_id(1)
    n_chunks = nb_past // cb
    rows = N_HEADS * tq
    width = q_ref.shape[1]
    r_id = lax.broadcasted_iota(I32, (rows, width), 0)
    lane = lax.broadcasted_iota(I32, (rows, width), 1)
    head_mask = (lane // HEAD_DIM) == (r_id // tq)
    q_rows = jnp.concatenate([q_ref[...]] * N_HEADS, axis=0)
    qbd = jnp.where(head_mask, q_rows, 0.0)
    qbd_b = qbd.astype(MXU_DTYPE)
    tab = tab_ref[...]

    def bias_rows(dist):
        bucket = _bucket(dist)
        out = jnp.zeros(dist.shape, F32)
        for b in range(NUM_BUCKETS):
            out = jnp.where(bucket == b, tab[:, b:b + 1], out)
        return out

    c_far = tab[:, NUM_BUCKETS - 1:NUM_BUCKETS]

    @pl.when(c == 0)
    def _():
        km_ref[...] = jnp.zeros(km_ref.shape, F32)
        m_ref[...] = jnp.full(m_ref.shape, NEG, F32)
        l_ref[...] = jnp.zeros(l_ref.shape, F32)

    km_lane = lax.broadcasted_iota(I32, km_ref.shape, 1)
    st_lane = lax.broadcasted_iota(I32, m_ref.shape, 1)
    qi = lax.broadcasted_iota(I32, (rows, MOBA_BLOCK), 0) % tq
    kj = lax.broadcasted_iota(I32, (rows, MOBA_BLOCK), 1)
    last_bias = lax.cond(c == n_chunks - 1, lambda: bias_rows(MOBA_BLOCK + qi - kj),
                         lambda: jnp.broadcast_to(c_far, (rows, MOBA_BLOCK)))
    scores = []
    km_new = jnp.zeros(km_ref.shape, F32)
    for j in range(cb):
        kts = [k_pages[PAGES_PER_BLOCK * j + g][0] for g in range(PAGES_PER_BLOCK)]
        ksum = kts[0]
        for kt in kts[1:]:
            ksum = ksum + kt
        kmean = jnp.sum(ksum, axis=1, keepdims=True) * (1.0 / MOBA_BLOCK)
        km_new = jnp.where(km_lane == c * cb + j, kmean, km_new)
        scores.append(jnp.concatenate(
            [jnp.dot(qbd_b, kt.astype(MXU_DTYPE), preferred_element_type=F32) for kt in kts], axis=1))
    km_ref[...] = km_ref[...] + km_new
    probs = []
    m_new = jnp.full(m_ref.shape, NEG, F32)
    l_new = jnp.zeros(l_ref.shape, F32)
    for j, s in enumerate(scores):
        s = s + (last_bias if j == cb - 1 else c_far)
        m_n = jnp.max(s, axis=1, keepdims=True)
        p = jnp.exp(s - m_n)
        m_new = jnp.where(st_lane == c * cb + j, m_n, m_new)
        l_new = jnp.where(st_lane == c * cb + j, jnp.sum(p, axis=1, keepdims=True), l_new)
        probs.append(p.astype(MXU_DTYPE))
    m_ref[...] = jnp.maximum(m_ref[...], m_new)
    l_ref[...] = l_ref[...] + l_new
    for j, pb in enumerate(probs):
        acc = None
        for g in range(PAGES_PER_BLOCK):
            vt = v_pages[PAGES_PER_BLOCK * j + g][0]
            part = lax.dot_general(pb[:, g * PAGE_SIZE:(g + 1) * PAGE_SIZE], vt.astype(MXU_DTYPE), _nt_dims(),
                                   preferred_element_type=F32)
            acc = part if acc is None else acc + part
        acc_ref[c * cb + j] = acc

    @pl.when(c == n_chunks - 1)
    def _():
        lanes = LANES
        nbp = -(-nb_past // SUBLANES) * SUBLANES
        q_pad = jnp.concatenate([qbd, jnp.zeros((lanes - rows, width), F32)], axis=0)
        gT = lax.dot_general(km_ref[...].T, q_pad, _nt_dims(), preferred_element_type=F32,
                             precision=HIGHEST)[:nbp]
        nid = lax.broadcasted_iota(I32, gT.shape, 0)
        cnt = jnp.zeros(gT.shape, I32)
        for m in range(nb_past):
            row = gT[m:m + 1, :]
            beats = (row > gT) | ((row == gT) & (m < nid))
            cnt = cnt + jnp.where(beats, 1, 0)
        selT = jnp.where((cnt < MOBA_TOPK) & (nid < nb_past), 1.0, 0.0)
        selT = jnp.concatenate([selT, jnp.zeros((lanes - nbp, lanes), F32)], axis=0)
        sel = selT.T[:rows] > 0.5
        pad = jnp.zeros((PAGE_SIZE - tq, width), F32)
        kn = jnp.concatenate([kn_ref[...], pad], axis=0)
        vn = jnp.concatenate([vn_ref[...], pad], axis=0)
        qi = lax.broadcasted_iota(I32, (rows, PAGE_SIZE), 0) % tq
        kj = lax.broadcasted_iota(I32, (rows, PAGE_SIZE), 1)
        s = lax.dot_general(qbd_b, kn.astype(MXU_DTYPE), _nt_dims(), preferred_element_type=F32)
        s = jnp.where(kj <= qi, s + bias_rows(qi - kj), NEG)
        m_o = jnp.max(s, axis=1, keepdims=True)
        p = jnp.exp(s - m_o)
        l_o = jnp.sum(p, axis=1, keepdims=True)
        acc_o = jnp.dot(p.astype(MXU_DTYPE), vn.astype(MXU_DTYPE), preferred_element_type=F32)
        m_blk = m_ref[...]
        m_all = jnp.maximum(m_o, jnp.max(jnp.where(sel, m_blk, NEG), axis=1, keepdims=True))
        w_blk = jnp.where(sel, jnp.exp(m_blk - m_all), 0.0)
        w_o = jnp.exp(m_o - m_all)
        l_all = w_o * l_o + jnp.sum(w_blk * l_ref[...], axis=1, keepdims=True)
        acc_all = w_o * acc_o
        for n in range(nb_past):
            acc_all = acc_all + w_blk[:, n:n + 1] * acc_ref[n]
        out_bd = jnp.where(head_mask, acc_all / l_all, 0.0)
        out = out_bd[0:tq]
        for h in range(1, N_HEADS):
            out = out + out_bd[h * tq:(h + 1) * tq]
        o_ref[...] = out


def _moba_sample(page_rows, q, k_new, v_new, tab_rows, cache_k, cache_v, *, nseq, tq, nb_past):
    assert nb_past <= LANES and N_HEADS * tq <= LANES
    cb = math.gcd(SAMPLE_CHUNK_BLOCKS, nb_past)
    npg = cb * PAGES_PER_BLOCK
    n_chunks = nb_past // cb
    ppseq = nb_past * PAGES_PER_BLOCK
    rows = N_HEADS * tq
    width = q.shape[1]
    tok = lambda b, c, pt: (b, 0)

    def page_spec(g):
        return pl.BlockSpec((1, width, PAGE_SIZE), lambda b, c, pt: (pt[b * ppseq + c * npg + g], 0, 0))

    grid_spec = pltpu.PrefetchScalarGridSpec(
        num_scalar_prefetch=1,
        grid=(nseq, n_chunks),
        in_specs=[pl.BlockSpec((tq, width), tok)] * 3
        + [pl.BlockSpec(tab_rows.shape, lambda b, c, pt: (0, 0))]
        + [page_spec(g) for g in range(npg)] * 2,
        out_specs=pl.BlockSpec((tq, width), tok),
        scratch_shapes=[pltpu.VMEM((width, LANES), F32),
                        pltpu.VMEM((rows, LANES), F32),
                        pltpu.VMEM((rows, LANES), F32),
                        pltpu.VMEM((nb_past, rows, width), F32)],
    )
    return pl.pallas_call(
        functools.partial(_moba_sample_kernel, tq=tq, nb_past=nb_past, cb=cb),
        grid_spec=grid_spec,
        out_shape=jax.ShapeDtypeStruct((nseq * tq, width), F32),
        compiler_params=pltpu.CompilerParams(dimension_semantics=("arbitrary",) * 2,
                                             vmem_limit_bytes=VMEM_LIMIT),
        name="moba_sample",
    )(page_rows, q, k_new, v_new, tab_rows, *([cache_k] * npg), *([cache_v] * npg))


def _mix_kernel(*refs, tm, seq_len, attn_transposed, has_state, alpha):
    if has_state:
        (x_ref, attn_ref, u_ref, halo_ref, st0_ref, st1_ref, cb_ref, ga_ref, gb_ref, cw_ref,
         wao_ref, wco_ref, wo_ref, g_ref, b_ref, x1_ref, ubuf) = refs
    else:
        (x_ref, attn_ref, u_ref, halo_ref, cb_ref, ga_ref, gb_ref, cw_ref,
         wao_ref, wco_ref, wo_ref, g_ref, b_ref, x1_ref, ubuf) = refs
    i = pl.program_id(0)
    if attn_transposed:
        attn = attn_ref[...].astype(F32).T.astype(MXU_DTYPE)
    else:
        attn = attn_ref[...].astype(MXU_DTYPE)
    y_att = jnp.dot(attn, wao_ref[...], preferred_element_type=F32)
    u = u_ref[...]
    ubuf[0:SUBLANES, :] = halo_ref[...]
    ubuf[SUBLANES:SUBLANES + tm, :] = u
    p1 = ubuf[SUBLANES - 1:SUBLANES - 1 + tm, :]
    p2 = ubuf[SUBLANES - 2:SUBLANES - 2 + tm, :]
    row = lax.broadcasted_iota(I32, (tm, 1), 0)
    if seq_len % tm == 0:
        pos = (i * tm) % seq_len + row
    else:
        pos = row % seq_len
    s0 = st0_ref[...] if has_state else 0.0
    s1 = st1_ref[...] if has_state else 0.0
    prev1 = jnp.where(pos >= 1, p1, s1)
    prev2 = jnp.where(pos >= 2, p2, jnp.where(pos == 1, s1, s0))
    conv = cw_ref[0:1, :] * prev2 + cw_ref[1:2, :] * prev1 + cw_ref[2:3, :] * u
    y_conv = jnp.dot((cb_ref[...].astype(F32) * conv).astype(MXU_DTYPE), wco_ref[...],
                     preferred_element_type=F32)
    merged = ga_ref[...].astype(F32) * y_att + gb_ref[...].astype(F32) * y_conv
    z = alpha * x_ref[...] + jnp.dot(merged.astype(MXU_DTYPE), wo_ref[...], preferred_element_type=F32)
    x1 = _layer_norm(z, g_ref[...], b_ref[...])
    x1_ref[...] = x1


def _mix(x2d, attn, u, state, cb, ga, gb, conv_w, wao, wco, wo, g, b, *, tm, seq_len, attn_transposed, alpha):
    t, d = x2d.shape
    cw = u.shape[1]
    assert seq_len % tm == 0 or tm % seq_len == 0
    nt = t // tm
    row = lambda i: (i, 0)
    full = lambda a: pl.BlockSpec(a.shape, lambda i: (0,) * a.ndim)
    has_state = state is not None
    hb = tm // SUBLANES
    attn_spec = (pl.BlockSpec((ATT_W, tm), lambda i: (0, i)) if attn_transposed
                 else pl.BlockSpec((tm, ATT_W), row))
    args = [x2d, attn, u, u]
    in_specs = [pl.BlockSpec((tm, d), row), attn_spec, pl.BlockSpec((tm, cw), row),
                pl.BlockSpec((SUBLANES, cw), lambda i: (jnp.maximum(i * hb - 1, 0), 0))]
    if has_state:
        args += list(state)
        in_specs += [pl.BlockSpec((tm, cw), row)] * 2
    args += [cb, ga, gb, conv_w, wao, wco, wo, g, b]
    in_specs += [pl.BlockSpec((tm, cw), row), pl.BlockSpec((tm, d), row), pl.BlockSpec((tm, d), row),
                 full(conv_w), full(wao), full(wco), full(wo), full(g), full(b)]
    return pl.pallas_call(
        functools.partial(_mix_kernel, tm=tm, seq_len=seq_len, attn_transposed=attn_transposed,
                          has_state=has_state, alpha=alpha),
        grid=(nt,),
        in_specs=in_specs,
        out_specs=pl.BlockSpec((tm, d), row),
        out_shape=jax.ShapeDtypeStruct((t, d), F32),
        scratch_shapes=[pltpu.VMEM((SUBLANES + tm, cw), F32)],
        compiler_params=pltpu.CompilerParams(dimension_semantics=("arbitrary",),
                                             vmem_limit_bytes=VMEM_LIMIT),
        name="mix",
    )(*args)


def _router_kernel(x_ref, wT_ref, b_ref, idx_ref, gate_ref, rank_ref, cnt_ref, seen_ref):
    i = pl.program_id(0)
    ne, tm = wT_ref.shape[0], x_ref.shape[0]

    @pl.when(i == 0)
    def _():
        seen_ref[...] = jnp.zeros(seen_ref.shape, F32)

    logits = lax.dot_general(wT_ref[...], x_ref[...], _nt_dims(), preferred_element_type=F32,
                             precision=HIGHEST) + b_ref[...]
    eid = lax.broadcasted_iota(I32, (ne, tm), 0)
    cnt = jnp.zeros((ne, tm), I32)
    for e in range(ne):
        row = logits[e:e + 1, :]
        beats = (row > logits) | ((row == logits) & (e < eid))
        cnt = cnt + jnp.where(beats, 1, 0)
    sel = cnt < TOP_K
    lmax = jnp.max(logits, axis=0, keepdims=True)
    ex = jnp.where(sel, jnp.exp(logits - lmax), 0.0)
    gates = ex / jnp.sum(ex, axis=0, keepdims=True)
    self_ = jnp.where(sel, 1.0, 0.0)
    tt = lax.broadcasted_iota(I32, (tm, tm), 0) < lax.broadcasted_iota(I32, (tm, tm), 1)
    earlier = jnp.where(tt, 1.0, 0.0).astype(MXU_DTYPE)
    seen = seen_ref[:, 0:1]
    rank = jnp.dot(self_.astype(MXU_DTYPE), earlier, preferred_element_type=F32) + seen
    seen = seen + jnp.sum(self_, axis=1, keepdims=True)
    seen_ref[...] = jnp.broadcast_to(seen, seen_ref.shape)
    cnt_ref[...] = jnp.broadcast_to(seen, cnt_ref.shape)
    ee = lax.broadcasted_iota(I32, (ne, ne), 1) < lax.broadcasted_iota(I32, (ne, ne), 0)
    below = jnp.dot(jnp.where(ee, 1.0, 0.0).astype(MXU_DTYPE), self_.astype(MXU_DTYPE),
                    preferred_element_type=F32)
    eidf = eid.astype(F32)
    for kk in range(TOP_K):
        hit = sel & (below == float(kk))
        idx_ref[kk:kk + 1, :] = jnp.sum(jnp.where(hit, eidf, 0.0), axis=0, keepdims=True).astype(I32)
        gate_ref[kk:kk + 1, :] = jnp.sum(jnp.where(hit, gates, 0.0), axis=0, keepdims=True)
        rank_ref[kk:kk + 1, :] = jnp.sum(jnp.where(hit, rank, 0.0), axis=0, keepdims=True).astype(I32)


def _route(x1, router_wT, router_b_col, *, tm):
    t, d = x1.shape
    ne = router_wT.shape[0]
    nt = t // tm
    col = lambda i: (0, i)
    return pl.pallas_call(
        _router_kernel,
        grid=(nt,),
        in_specs=[pl.BlockSpec((tm, d), lambda i: (i, 0)),
                  pl.BlockSpec((ne, d), lambda i: (0, 0)),
                  pl.BlockSpec((ne, 1), lambda i: (0, 0))],
        out_specs=[pl.BlockSpec((TOP_K, tm), col), pl.BlockSpec((TOP_K, tm), col),
                   pl.BlockSpec((TOP_K, tm), col), pl.BlockSpec((ne, 128), lambda i: (0, 0))],
        out_shape=[jax.ShapeDtypeStruct((TOP_K, t), I32), jax.ShapeDtypeStruct((TOP_K, t), F32),
                   jax.ShapeDtypeStruct((TOP_K, t), I32), jax.ShapeDtypeStruct((ne, 128), F32)],
        scratch_shapes=[pltpu.VMEM((ne, 128), F32)],
        compiler_params=pltpu.CompilerParams(dimension_semantics=("arbitrary",),
                                             vmem_limit_bytes=VMEM_LIMIT),
        name="router",
    )(x1, router_wT, router_b_col)


def _dispatch_kernel(dest_ref, pend_ref, x_ref, xb_ref, zeros_ref, sem, zsem, *, t, tm, tile, ne):
    i = pl.program_id(0)

    def row_copy(r, dst):
        return pltpu.make_async_copy(x_ref.at[pl.ds(r, 1)], xb_ref.at[pl.ds(dst, 1)], sem)

    def clear_copy(e):
        start = pl.multiple_of(pend_ref[e] - tile, tile)
        return pltpu.make_async_copy(zeros_ref, xb_ref.at[pl.ds(start, tile)], zsem)

    @pl.when(i == 0)
    def _():
        zeros_ref[...] = jnp.zeros(zeros_ref.shape, F32)
        for phase in ("start", "wait"):
            for e in range(ne):
                has_rows = pend_ref[e] > (pend_ref[e - 1] if e else 0)

                @pl.when(has_rows)
                def _():
                    if phase == "start":
                        clear_copy(e).start()
                    else:
                        clear_copy(e).wait()

        def tail_copy(j):
            return pltpu.make_async_copy(zeros_ref, xb_ref.at[pl.ds(pl.multiple_of(j * tile, tile), tile)], zsem)

        first_unused = pend_ref[ne - 1] // tile
        n_tiles = xb_ref.shape[0] // tile
        lax.fori_loop(first_unused, n_tiles, lambda j, c: (tail_copy(j).start(), c)[1], 0)
        lax.fori_loop(first_unused, n_tiles, lambda j, c: (tail_copy(j).wait(), c)[1], 0)

    def issue(r, carry):
        for kk in range(TOP_K):
            row_copy(r, dest_ref[kk * t + i * tm + r]).start(priority=kk % 2)
        return carry

    lax.fori_loop(0, tm, issue, 0, unroll=DISPATCH_UNROLL)
    for kk in range(TOP_K):
        pltpu.make_async_copy(x_ref, xb_ref.at[pl.ds(0, tm)], sem).wait()


def _dispatch(dest_flat, pad_end, x1, *, rows, tm, tile):
    t, d = x1.shape
    ne = pad_end.shape[0]
    grid_spec = pltpu.PrefetchScalarGridSpec(
        num_scalar_prefetch=2,
        grid=(t // tm,),
        in_specs=[pl.BlockSpec((tm, d), lambda i, dest, pend: (i, 0))],
        out_specs=pl.BlockSpec(memory_space=pl.ANY),
        scratch_shapes=[pltpu.VMEM((tile, d), F32), pltpu.SemaphoreType.DMA, pltpu.SemaphoreType.DMA],
    )
    return pl.pallas_call(
        functools.partial(_dispatch_kernel, t=t, tm=tm, tile=tile, ne=ne),
        grid_spec=grid_spec,
        out_shape=jax.ShapeDtypeStruct((rows, d), F32),
        compiler_params=pltpu.CompilerParams(dimension_semantics=("arbitrary",),
                                             vmem_limit_bytes=VMEM_LIMIT),
        name="dispatch",
    )(dest_flat, pad_end, x1)


def _expert_kernel(te_ref, nu_ref, x_ref, wgu_ref, bg_ref, bu_ref, wd_ref, bd_ref, y_ref, wg_s, wu_s, wd_s):
    i = pl.program_id(0)
    new_expert = jnp.logical_or(i == 0, te_ref[i] != te_ref[jnp.maximum(i - 1, 0)])

    @pl.when(jnp.logical_and(new_expert, i < nu_ref[0]))
    def _():
        half = LANES
        c = lax.broadcasted_iota(I32, (2 * half, 2 * half), 0)
        o = lax.broadcasted_iota(I32, (2 * half, 2 * half), 1)
        src = jnp.where(o < half, 2 * o, 2 * (o - half) + 1)
        perm = jnp.where(c == src, 1.0, 0.0).astype(MXU_DTYPE)
        for j in range(wgu_ref.shape[2] // (2 * half)):
            wb = wgu_ref[0, :, 2 * half * j:2 * half * (j + 1)].astype(MXU_DTYPE)
            y = jnp.dot(wb, perm, preferred_element_type=F32).astype(MXU_DTYPE)
            wg_s[:, half * j:half * (j + 1)] = y[:, :half]
            wu_s[:, half * j:half * (j + 1)] = y[:, half:]
        wd_s[...] = wd_ref[0].astype(MXU_DTYPE)

    @pl.when(i < nu_ref[0])
    def _():
        x = x_ref[...].astype(MXU_DTYPE)
        hg = jnp.dot(x, wg_s[...], preferred_element_type=F32) + bg_ref[0]
        hu = jnp.dot(x, wu_s[...], preferred_element_type=F32) + bu_ref[0]
        g = jnp.minimum(hg, SWIGLU_LIMIT)
        u = jnp.clip(hu, -SWIGLU_LIMIT, SWIGLU_LIMIT)
        a = g * _sigmoid(SWIGLU_ALPHA * g) * (u + 1.0)
        y_ref[...] = jnp.dot(a.astype(MXU_DTYPE), wd_s[...], preferred_element_type=F32) + bd_ref[0]

    @pl.when(i >= nu_ref[0])
    def _():
        y_ref[...] = jnp.zeros(y_ref.shape, F32)


def _experts(tile_e, n_used, xb, w_gate_up, bg, bu, w_down, bd, *, tm):
    rows, d = xb.shape
    dff = w_down.shape[1]
    nt = rows // tm
    ew = lambda i, te, nu: (te[i], 0, 0)
    grid_spec = pltpu.PrefetchScalarGridSpec(
        num_scalar_prefetch=2,
        grid=(nt,),
        in_specs=[pl.BlockSpec((tm, d), lambda i, te, nu: (jnp.minimum(i, nu[0] - 1), 0)),
                  pl.BlockSpec((1, d, 2 * dff), ew),
                  pl.BlockSpec((1, 1, dff), ew), pl.BlockSpec((1, 1, dff), ew),
                  pl.BlockSpec((1, dff, d), ew), pl.BlockSpec((1, 1, d), ew)],
        out_specs=pl.BlockSpec((tm, d), lambda i, te, nu: (i, 0)),
        scratch_shapes=[pltpu.VMEM((d, dff), MXU_DTYPE), pltpu.VMEM((d, dff), MXU_DTYPE),
                        pltpu.VMEM((dff, d), MXU_DTYPE)],
    )
    return pl.pallas_call(
        _expert_kernel,
        grid_spec=grid_spec,
        out_shape=jax.ShapeDtypeStruct((rows, d), F32),
        compiler_params=pltpu.CompilerParams(dimension_semantics=("arbitrary",),
                                             vmem_limit_bytes=VMEM_LIMIT),
        name="experts",
    )(tile_e, n_used, xb, w_gate_up, bg, bu, w_down, bd)


def _norm2_kernel(x1_ref, gate_ref, *refs, alpha):
    y_refs, (g_ref, b_ref, o_ref) = refs[:TOP_K], refs[TOP_K:]
    gate = gate_ref[...]
    ffn = gate[:, 0:1] * y_refs[0][...]
    for kk in range(1, TOP_K):
        ffn = ffn + gate[:, kk:kk + 1] * y_refs[kk][...]
    o_ref[...] = _layer_norm(alpha * x1_ref[...] + ffn, g_ref[...], b_ref[...])


def _norm2(x1_all, gate_t, y_pairs, g, b, *, row0, nrows, tm, alpha):
    d = x1_all.shape[1]
    off = row0 // tm
    src = lambda i: (i + off, 0)
    return pl.pallas_call(
        functools.partial(_norm2_kernel, alpha=alpha),
        grid=(nrows // tm,),
        in_specs=[pl.BlockSpec((tm, d), src), pl.BlockSpec((tm, TOP_K), src)]
        + [pl.BlockSpec((tm, d), src)] * TOP_K
        + [pl.BlockSpec((1, d), lambda i: (0, 0)), pl.BlockSpec((1, d), lambda i: (0, 0))],
        out_specs=pl.BlockSpec((tm, d), lambda i: (i, 0)),
        out_shape=jax.ShapeDtypeStruct((nrows, d), F32),
        compiler_params=pltpu.CompilerParams(dimension_semantics=("arbitrary",),
                                             vmem_limit_bytes=VMEM_LIMIT),
        name="norm2",
    )(x1_all, gate_t, *y_pairs, g, b)


def _moe(x1_all, router_w, router_b, w_gate_up, b_gate_up, w_down, b_down):
    t, d = x1_all.shape
    ne = router_w.shape[1]
    idx, gate, rank, counts = _route(x1_all, router_w.T, router_b.reshape(ne, 1), tm=TOKEN_TILE)
    tm = EXPERT_TILE
    counts = counts[:, 0].astype(I32)
    padded = (counts + tm - 1) // tm * tm
    pad_end = jnp.cumsum(padded)
    pad_start = pad_end - padded
    n_tiles = (t * TOP_K + ne * (tm - 1)) // tm
    rows = n_tiles * tm
    eids = jnp.arange(ne, dtype=I32)
    start_of = jnp.sum(jnp.where(idx[:, :, None] == eids, pad_start, 0), axis=-1)
    dest = start_of + rank
    tile_row = jnp.arange(n_tiles, dtype=I32) * tm
    tile_e = jnp.minimum(jnp.sum((pad_end[None, :] <= tile_row[:, None]).astype(I32), axis=1), ne - 1)
    n_used = (pad_end[-1:] // tm).astype(I32)
    xb = _dispatch(dest.reshape(-1), pad_end.astype(I32), x1_all, rows=rows, tm=TOKEN_TILE, tile=tm)
    bg = b_gate_up[:, None, 0::2]
    bu = b_gate_up[:, None, 1::2]
    yb = _experts(tile_e, n_used, xb, w_gate_up, bg, bu, w_down, b_down[:, None, :], tm=tm)
    return gate.T, [yb[dest[kk]] for kk in range(TOP_K)]


def kernel(x_prompt, x_sample, cache_k, cache_v, state_conv, page_table, rel_bias, w_in, conv_w, w_att_o,
           w_conv_o, w_o, ln1_g, ln1_b, router_w, router_b, w_gate_up, b_gate_up, w_down, b_down, ln2_g, ln2_b):
    depth = w_in.shape[0]
    batch, seq, d = x_prompt.shape
    nseq, tq, _ = x_sample.shape
    n_pool = cache_k.shape[1]
    ppseq = page_table.shape[1]
    past = ppseq * PAGE_SIZE
    assert seq % MOBA_BLOCK == 0 and past % MOBA_BLOCK == 0 and tq % SUBLANES == 0 and tq <= PAGE_SIZE
    nb_past = past // MOBA_BLOCK
    tp, ts = batch * seq, nseq * tq
    alpha = (2 * depth) ** 0.25
    cw = conv_w.shape[2]
    tab_rows = jnp.repeat(rel_bias.T, tq, axis=0)
    ck = cache_k.transpose(0, 1, 3, 4, 2).reshape(depth * n_pool, ATT_W, PAGE_SIZE)
    cv = cache_v.transpose(0, 1, 3, 4, 2).reshape(depth * n_pool, ATT_W, PAGE_SIZE)
    hp = x_prompt.reshape(tp, d)
    hs = x_sample.reshape(ts, d)
    outs = [[] for _ in range(6)]
    for l in range(depth):
        w_in_b = w_in[l].astype(MXU_DTYPE)
        wao, wco, wo = (w_att_o[l].astype(MXU_DTYPE), w_conv_o[l].astype(MXU_DTYPE), w_o[l].astype(MXU_DTYPE))
        g1, b1, g2, b2 = ln1_g[l][None], ln1_b[l][None], ln2_g[l][None], ln2_b[l][None]
        qT, khm, vT, kp, vp, km, up, cbp, gap, gbp = _project(hp, w_in_b, tm=PROJ_TILE, head_major=True, seq=seq)
        nbt = PROJ_TILE // MOBA_BLOCK
        kmean = km[:, :nbt].reshape(batch, seq // MOBA_BLOCK, N_HEADS, HEAD_DIM).transpose(0, 2, 1, 3)
        kmean = kmean.reshape(batch * N_HEADS, seq // MOBA_BLOCK, HEAD_DIM)
        attn_p = _moba_prompt(rel_bias, qT, khm, vT, kmean, batch=batch, seq=seq)
        x1p = _mix(hp, attn_p, up, None, cbp, gap, gbp, conv_w[l], wao, wco, wo, g1, b1,
                         tm=PROJ_TILE, seq_len=seq, attn_transposed=True, alpha=alpha)
        qs, ks, vs, us, cbs, gas, gbs = _project(hs, w_in_b, tm=ts, head_major=False, wq=w_in[l][:, :ATT_W])
        pages = (page_table + l * n_pool).reshape(-1).astype(I32)
        attn_s = _moba_sample(pages, qs, ks, vs, tab_rows, ck, cv, nseq=nseq, tq=tq, nb_past=nb_past)
        state = (jnp.repeat(state_conv[l][:, 0], tq, axis=0), jnp.repeat(state_conv[l][:, 1], tq, axis=0))
        x1s = _mix(hs, attn_s, us, state, cbs, gas, gbs, conv_w[l], wao, wco, wo, g1, b1,
                         tm=ts, seq_len=tq, attn_transposed=False, alpha=alpha)
        x1_all = jnp.concatenate([x1p, x1s], axis=0)
        gate_t, y_pairs = _moe(x1_all, router_w[l], router_b[l], w_gate_up[l], b_gate_up[l], w_down[l], b_down[l])
        hp = _norm2(x1_all, gate_t, y_pairs, g2, b2, row0=0, nrows=tp, tm=TOKEN_TILE, alpha=alpha)
        hs = _norm2(x1_all, gate_t, y_pairs, g2, b2, row0=tp, nrows=ts, tm=TOKEN_TILE, alpha=alpha)
        outs[0].append(kp.reshape(batch, N_HEADS, HEAD_DIM, seq).transpose(0, 3, 1, 2))
        outs[1].append(vp.reshape(batch, N_HEADS, HEAD_DIM, seq).transpose(0, 3, 1, 2))
        outs[2].append(up.reshape(batch, seq, cw)[:, seq - (CONV_K - 1):])
        outs[3].append(ks.reshape(nseq, tq, N_HEADS, HEAD_DIM))
        outs[4].append(vs.reshape(nseq, tq, N_HEADS, HEAD_DIM))
        outs[5].append(us.reshape(nseq, tq, cw)[:, tq - (CONV_K - 1):])
    stacked = [jnp.stack(o) for o in outs]
    return (hp.reshape(batch, seq, d), hs.reshape(nseq, tq, d), *stacked)
```

```python
import functools
import math

import jax
import jax.numpy as jnp
from jax import lax
from jax.experimental import pallas as pl
from jax.experimental.pallas import tpu as pltpu

F32 = jnp.float32
I32 = jnp.int32
MXU_DTYPE = jnp.bfloat16
HIGHEST = lax.Precision.HIGHEST

N_HEADS = 8
HEAD_DIM = 64
ATT_W = N_HEADS * HEAD_DIM
MOBA_BLOCK = 256
MOBA_TOPK = 3
PAGE_SIZE = 128
PAGES_PER_BLOCK = MOBA_BLOCK // PAGE_SIZE
CONV_K = 3
N_EXPERTS = 32
TOP_K = 4
SWIGLU_LIMIT = 7.0
SWIGLU_ALPHA = 1.702
NUM_BUCKETS = 32
NUM_EXACT = NUM_BUCKETS // 2
MAX_DISTANCE = 128
LN_EPS = 1e-5
SCORE_SCALE = HEAD_DIM ** -0.5
NEG = -1e30
V_ROWS = HEAD_DIM + 16

SUBLANES = 8
LANES = 128
PROJ_TILE = 512
TOKEN_TILE = 256
EXPERT_TILE = 512
DISPATCH_UNROLL = 8
SAMPLE_CHUNK_BLOCKS = 16
VMEM_LIMIT = 56 * 1024 * 1024

assert MOBA_BLOCK >= MAX_DISTANCE


def _nt_dims():
    return (((1,), (1,)), ((), ()))


def _sigmoid(x):
    return 1.0 / (1.0 + jnp.exp(-x))


def _bucket(dist):
    n = jnp.maximum(dist, 0)
    nf = jnp.maximum(n, 1).astype(F32)
    large = NUM_EXACT + (jnp.log(nf / NUM_EXACT) / math.log(MAX_DISTANCE / NUM_EXACT)
                         * (NUM_BUCKETS - NUM_EXACT)).astype(I32)
    large = jnp.minimum(large, NUM_BUCKETS - 1)
    return jnp.where(n < NUM_EXACT, n, large)


def _layer_norm(z, g, b):
    mu = jnp.mean(z, axis=-1, keepdims=True)
    zc = z - mu
    var = jnp.mean(zc * zc, axis=-1, keepdims=True)
    return zc * lax.rsqrt(var + LN_EPS) * g + b


def _proj_kernel(x_ref, w_ref, *refs, tm, head_major, blocks_per_seq):
    xb = x_ref[...].astype(MXU_DTYPE)

    def mm(c0, width):
        return jnp.dot(xb, w_ref[:, c0:c0 + width], preferred_element_type=F32)

    if head_major:
        qT_ref, khm_ref, vT_ref, k_ref, v_ref, km_ref, u_ref, cb_ref, ga_ref, gb_ref = refs
        q = mm(0, ATT_W) * SCORE_SCALE
    else:
        wq_ref, q_ref, k_ref, v_ref, u_ref, cb_ref, ga_ref, gb_ref = refs
        q = jnp.dot(x_ref[...], wq_ref[...], preferred_element_type=F32, precision=HIGHEST) * SCORE_SCALE
    d_model = x_ref.shape[1]
    k = mm(ATT_W, ATT_W)
    v = mm(2 * ATT_W, ATT_W)
    if head_major:
        qT_ref[...] = q.T
        k_ref[0] = k.T
        vt = v.T
        v_ref[0] = vt
        km_ref[...] = jnp.zeros(km_ref.shape, F32)
        lane = lax.broadcasted_iota(I32, (MOBA_BLOCK, 2 * HEAD_DIM), 1)
        ones_rows = jnp.ones((V_ROWS - HEAD_DIM, MOBA_BLOCK), MXU_DTYPE)
        for r in range(tm // MOBA_BLOCK):
            kr = k[r * MOBA_BLOCK:(r + 1) * MOBA_BLOCK]
            km_ref[0, r:r + 1, :] = jnp.sum(kr, axis=0, keepdims=True) * (1.0 / MOBA_BLOCK)
            n_blk = (pl.program_id(0) * (tm // MOBA_BLOCK) + r) % blocks_per_seq
            tag = jnp.where(lane - HEAD_DIM == n_blk, 1.0, 0.0)
            vtr = vt[:, r * MOBA_BLOCK:(r + 1) * MOBA_BLOCK]
            for h in range(N_HEADS):
                pair = kr[:, (h // 2) * 2 * HEAD_DIM:(h // 2 + 1) * 2 * HEAD_DIM]
                if h % 2:
                    pair = pltpu.roll(pair, HEAD_DIM, axis=1)
                khm_ref[h, r] = jnp.where(lane < HEAD_DIM, pair, tag).astype(MXU_DTYPE)
                vT_ref[r, h * V_ROWS:h * V_ROWS + HEAD_DIM] = vtr[h * HEAD_DIM:(h + 1) * HEAD_DIM].astype(MXU_DTYPE)
                vT_ref[r, h * V_ROWS + HEAD_DIM:(h + 1) * V_ROWS] = ones_rows
    else:
        q_ref[...] = q
        k_ref[...] = k
        v_ref[...] = v
    c0 = 3 * ATT_W
    cw = u_ref.shape[1]
    cb_ref[...] = mm(c0, cw).astype(cb_ref.dtype)
    u_ref[...] = mm(c0 + cw, cw) * mm(c0 + 2 * cw, cw)
    ga_ref[...] = _sigmoid(mm(c0 + 3 * cw, d_model)).astype(ga_ref.dtype)
    gb_ref[...] = _sigmoid(mm(c0 + 3 * cw + d_model, d_model)).astype(gb_ref.dtype)


def _project(x2d, w_in_b, *, tm, head_major, seq=None, wq=None):
    t, d = x2d.shape
    cw = (w_in_b.shape[1] - 3 * ATT_W - 2 * d) // 3
    nt = t // tm
    row = lambda i: (i, 0)
    f32s = lambda shape: jax.ShapeDtypeStruct(shape, F32)
    mxs = lambda shape: jax.ShapeDtypeStruct(shape, MXU_DTYPE)
    tail_shapes = [f32s((t, cw)), mxs((t, cw)), mxs((t, d)), mxs((t, d))]
    tail_specs = [pl.BlockSpec((tm, cw), row), pl.BlockSpec((tm, cw), row),
                  pl.BlockSpec((tm, d), row), pl.BlockSpec((tm, d), row)]
    if head_major:
        nbt = tm // MOBA_BLOCK
        tps = seq // tm
        seq_t = pl.BlockSpec((1, ATT_W, tm), lambda i: (i // tps, 0, i % tps))
        out_shape = [f32s((ATT_W, t)), mxs((N_HEADS, t // MOBA_BLOCK, MOBA_BLOCK, 2 * HEAD_DIM)),
                     mxs((t // MOBA_BLOCK, N_HEADS * V_ROWS, MOBA_BLOCK)), f32s((t // seq, ATT_W, seq)),
                     f32s((t // seq, ATT_W, seq)), f32s((nt, SUBLANES, ATT_W))] + tail_shapes
        out_specs = [pl.BlockSpec((ATT_W, tm), lambda i: (0, i)),
                     pl.BlockSpec((N_HEADS, nbt, MOBA_BLOCK, 2 * HEAD_DIM), lambda i: (0, i, 0, 0)),
                     pl.BlockSpec((nbt, N_HEADS * V_ROWS, MOBA_BLOCK), lambda i: (i, 0, 0)),
                     seq_t, seq_t,
                     pl.BlockSpec((1, SUBLANES, ATT_W), lambda i: (i, 0, 0))] + tail_specs
        args, extra_specs = (x2d, w_in_b), []
    else:
        out_shape = [f32s((t, ATT_W))] * 3 + tail_shapes
        out_specs = [pl.BlockSpec((tm, ATT_W), row)] * 3 + tail_specs
        args, extra_specs = (x2d, w_in_b, wq), [pl.BlockSpec(wq.shape, lambda i: (0, 0))]
    return pl.pallas_call(
        functools.partial(_proj_kernel, tm=tm, head_major=head_major,
                          blocks_per_seq=seq // MOBA_BLOCK if head_major else None),
        grid=(nt,),
        in_specs=[pl.BlockSpec((tm, d), row), pl.BlockSpec(w_in_b.shape, lambda i: (0, 0))] + extra_specs,
        out_specs=out_specs,
        out_shape=out_shape,
        compiler_params=pltpu.CompilerParams(dimension_semantics=("arbitrary",),
                                             vmem_limit_bytes=VMEM_LIMIT),
        name="proj",
    )(*args)


def _moba_prompt_kernel(tab_ref, qT_ref, k_ref, vT_ref, km_ref, o_ref, bias_ref, qb_ref, *state):
    h = pl.program_id(0)
    first_sequence = pl.program_id(1) == 0
    nb = km_ref.shape[1]
    blk = MOBA_BLOCK
    n_streams = nb // 2
    m_refs, acc_refs = state[:n_streams], state[n_streams:]

    @pl.when(first_sequence)
    def _():
        kk = lax.broadcasted_iota(I32, (blk, blk), 0)
        qq = lax.broadcasted_iota(I32, (blk, blk), 1)
        for age in range(2):
            dist = qq - kk + age * blk
            bucket = _bucket(dist)
            tile = jnp.zeros((blk, blk), F32)
            for b in range(NUM_BUCKETS):
                tile = jnp.where(bucket == b, tab_ref[b, h], tile)
            bias_ref[age] = jnp.where(dist >= 0, tile, NEG)
        bias_ref[2] = jnp.full((blk, blk), tab_ref[NUM_BUCKETS - 1, h], F32)

    km = km_ref[0]
    nid = lax.broadcasted_iota(I32, (nb, blk), 0)
    spare = jnp.zeros((HEAD_DIM - nb, blk), F32)
    for i in range(nb):
        qT = qT_ref[:, i * blk:(i + 1) * blk]
        gT = jnp.dot(km, qT, preferred_element_type=F32, precision=HIGHEST)
        cnt = jnp.zeros((nb, blk), I32)
        for m in range(i):
            row = gT[m:m + 1, :]
            beats = (row > gT) | ((row == gT) & (m < nid))
            cnt = cnt + jnp.where(beats, 1, 0)
        keep = ((cnt < MOBA_TOPK) & (nid < i)) | (nid == i)
        qb_ref[i] = jnp.concatenate([qT, jnp.where(keep, 0.0, NEG), spare], axis=0).astype(MXU_DTYPE)

    for a in range(n_streams):
        m_refs[a][...] = jnp.full(m_refs[a].shape, NEG, F32)
        acc_refs[a][...] = jnp.zeros(acc_refs[a].shape, F32)

    def sweep_round(t, carry):
        staged = []
        for a in range(n_streams):
            first = t <= a
            slot = jnp.where(first, 0, 1)
            i = jnp.where(first, a, nb - 1 - a)
            n = jnp.where(first, a - t, nb - t)
            age = jnp.minimum(i - n, 2)
            s = jnp.dot(k_ref[0, n], qb_ref[i], preferred_element_type=F32) + bias_ref[age]
            staged.append((slot, n, s))
        updates = []
        for a, (slot, n, s) in enumerate(staged):
            m_old = m_refs[a][slot]
            m_new = jnp.maximum(m_old, jnp.max(s, axis=0, keepdims=True))
            alpha = jnp.exp(m_old - m_new)
            p = jnp.exp(s - m_new)
            acc_new = alpha * acc_refs[a][slot] + jnp.dot(
                vT_ref[n], p.astype(MXU_DTYPE), preferred_element_type=F32)
            updates.append((slot, m_new, acc_new))
        for a, (slot, m_new, acc_new) in enumerate(updates):
            m_refs[a][slot] = m_new
            acc_refs[a][slot] = acc_new
        return carry

    lax.fori_loop(0, nb + 1, sweep_round, 0)
    for a in range(n_streams):
        for slot, i in ((0, a), (1, nb - 1 - a)):
            acc = acc_refs[a][slot]
            o_ref[:, i * blk:(i + 1) * blk] = (acc[:HEAD_DIM] / acc[HEAD_DIM:HEAD_DIM + 1]).astype(o_ref.dtype)


def _moba_prompt(rel_bias, qT, khm, vT, kmean, *, batch, seq):
    nb = seq // MOBA_BLOCK
    assert nb % 2 == 0 and nb <= HEAD_DIM
    t = batch * seq
    stream = lambda shape: [pltpu.VMEM((2,) + shape, F32)] * (nb // 2)
    return pl.pallas_call(
        _moba_prompt_kernel,
        grid=(N_HEADS, batch),
        in_specs=[
            pl.BlockSpec(memory_space=pltpu.SMEM),
            pl.BlockSpec((HEAD_DIM, seq), lambda h, b: (h, b)),
            pl.BlockSpec((1, nb, MOBA_BLOCK, 2 * HEAD_DIM), lambda h, b: (h, b, 0, 0)),
            pl.BlockSpec((nb, V_ROWS, MOBA_BLOCK), lambda h, b: (b, h, 0)),
            pl.BlockSpec((1, nb, HEAD_DIM), lambda h, b: (b * N_HEADS + h, 0, 0)),
        ],
        out_specs=pl.BlockSpec((HEAD_DIM, seq), lambda h, b: (h, b)),
        out_shape=jax.ShapeDtypeStruct((ATT_W, t), MXU_DTYPE),
        scratch_shapes=[pltpu.VMEM((3, MOBA_BLOCK, MOBA_BLOCK), F32),
                        pltpu.VMEM((nb, 2 * HEAD_DIM, MOBA_BLOCK), MXU_DTYPE)]
        + stream((1, MOBA_BLOCK)) + stream((V_ROWS, MOBA_BLOCK)),
        compiler_params=pltpu.CompilerParams(dimension_semantics=("arbitrary",) * 2,
                                             vmem_limit_bytes=VMEM_LIMIT),
        name="moba_prompt",
    )(rel_bias, qT, khm, vT, kmean)


def _moba_sample_kernel(pt_ref, q_ref, kn_ref, vn_ref, tab_ref, *refs, tq, nb_past, cb):
    del pt_ref
    npg = cb * PAGES_PER_BLOCK
    k_pages = refs[:npg]
    v_pages = refs[npg:2 * npg]
    o_ref, km_ref, m_ref, l_ref, acc_ref = refs[2 * npg:]
    c = pl.program_id(1)
    n_chunks = nb_past // cb
    rows = N_HEADS * tq
    width = q_ref.shape[1]
    r_id = lax.broadcasted_iota(I32, (rows, width), 0)
    lane = lax.broadcasted_iota(I32, (rows, width), 1)
    head_mask = (lane // HEAD_DIM) == (r_id // tq)
    q_rows = jnp.concatenate([q_ref[...]] * N_HEADS, axis=0)
    qbd = jnp.where(head_mask, q_rows, 0.0)
    qbd_b = qbd.astype(MXU_DTYPE)
    tab = tab_ref[...]

    def bias_rows(dist):
        bucket = _bucket(dist)
        out = jnp.zeros(dist.shape, F32)
        for b in range(NUM_BUCKETS):
            out = jnp.where(bucket == b, tab[:, b:b + 1], out)
        return out

    c_far = tab[:, NUM_BUCKETS - 1:NUM_BUCKETS]

    @pl.when(c == 0)
    def _():
        km_ref[...] = jnp.zeros(km_ref.shape, F32)
        m_ref[...] = jnp.full(m_ref.shape, NEG, F32)
        l_ref[...] = jnp.zeros(l_ref.shape, F32)

    km_lane = lax.broadcasted_iota(I32, km_ref.shape, 1)
    st_lane = lax.broadcasted_iota(I32, m_ref.shape, 1)
    qi = lax.broadcasted_iota(I32, (rows, MOBA_BLOCK), 0) % tq
    kj = lax.broadcasted_iota(I32, (rows, MOBA_BLOCK), 1)
    last_bias = lax.cond(c == n_chunks - 1, lambda: bias_rows(MOBA_BLOCK + qi - kj),
                         lambda: jnp.broadcast_to(c_far, (rows, MOBA_BLOCK)))
    scores = []
    km_new = jnp.zeros(km_ref.shape, F32)
    for j in range(cb):
        kts = [k_pages[PAGES_PER_BLOCK * j + g][0] for g in range(PAGES_PER_BLOCK)]
        ksum = kts[0]
        for kt in kts[1:]:
            ksum = ksum + kt
        kmean = jnp.sum(ksum, axis=1, keepdims=True) * (1.0 / MOBA_BLOCK)
        km_new = jnp.where(km_lane == c * cb + j, kmean, km_new)
        scores.append(jnp.concatenate(
            [jnp.dot(qbd_b, kt.astype(MXU_DTYPE), preferred_element_type=F32) for kt in kts], axis=1))
    km_ref[...] = km_ref[...] + km_new
    probs = []
    m_new = jnp.full(m_ref.shape, NEG, F32)
    l_new = jnp.zeros(l_ref.shape, F32)
    for j, s in enumerate(scores):
        s = s + (last_bias if j == cb - 1 else c_far)
        m_n = jnp.max(s, axis=1, keepdims=True)
        p = jnp.exp(s - m_n)
        m_new = jnp.where(st_lane == c * cb + j, m_n, m_new)
        l_new = jnp.where(st_lane == c * cb + j, jnp.sum(p, axis=1, keepdims=True), l_new)
        probs.append(p.astype(MXU_DTYPE))
    m_ref[...] = jnp.maximum(m_ref[...], m_new)
    l_ref[...] = l_ref[...] + l_new
    for j, pb in enumerate(probs):
        acc = None
        for g in range(PAGES_PER_BLOCK):
            vt = v_pages[PAGES_PER_BLOCK * j + g][0]
            part = lax.dot_general(pb[:, g * PAGE_SIZE:(g + 1) * PAGE_SIZE], vt.astype(MXU_DTYPE), _nt_dims(),
                                   preferred_element_type=F32)
            acc = part if acc is None else acc + part
        acc_ref[c * cb + j] = acc

    @pl.when(c == n_chunks - 1)
    def _():
        lanes = LANES
        nbp = -(-nb_past // SUBLANES) * SUBLANES
        q_pad = jnp.concatenate([qbd, jnp.zeros((lanes - rows, width), F32)], axis=0)
        gT = lax.dot_general(km_ref[...].T, q_pad, _nt_dims(), preferred_element_type=F32,
                             precision=HIGHEST)[:nbp]
        nid = lax.broadcasted_iota(I32, gT.shape, 0)
        cnt = jnp.zeros(gT.shape, I32)
        for m in range(nb_past):
            row = gT[m:m + 1, :]
            beats = (row > gT) | ((row == gT) & (m < nid))
            cnt = cnt + jnp.where(beats, 1, 0)
        selT = jnp.where((cnt < MOBA_TOPK) & (nid < nb_past), 1.0, 0.0)
        selT = jnp.concatenate([selT, jnp.zeros((lanes - nbp, lanes), F32)], axis=0)
        sel = selT.T[:rows] > 0.5
        pad = jnp.zeros((PAGE_SIZE - tq, width), F32)
        kn = jnp.concatenate([kn_ref[...], pad], axis=0)
        vn = jnp.concatenate([vn_ref[...], pad], axis=0)
        qi = lax.broadcasted_iota(I32, (rows, PAGE_SIZE), 0) % tq
        kj = lax.broadcasted_iota(I32, (rows, PAGE_SIZE), 1)
        s = lax.dot_general(qbd_b, kn.astype(MXU_DTYPE), _nt_dims(), preferred_element_type=F32)
        s = jnp.where(kj <= qi, s + bias_rows(qi - kj), NEG)
        m_o = jnp.max(s, axis=1, keepdims=True)
        p = jnp.exp(s - m_o)
        l_o = jnp.sum(p, axis=1, keepdims=True)
        acc_o = jnp.dot(p.astype(MXU_DTYPE), vn.astype(MXU_DTYPE), preferred_element_type=F32)
        m_blk = m_ref[...]
        m_all = jnp.maximum(m_o, jnp.max(jnp.where(sel, m_blk, NEG), axis=1, keepdims=True))
        w_blk = jnp.where(sel, jnp.exp(m_blk - m_all), 0.0)
        w_o = jnp.exp(m_o - m_all)
        l_all = w_o * l_o + jnp.sum(w_blk * l_ref[...], axis=1, keepdims=True)
        acc_all = w_o * acc_o
        for n in range(nb_past):
            acc_all = acc_all + w_blk[:, n:n + 1] * acc_ref[n]
        out_bd = jnp.where(head_mask, acc_all / l_all, 0.0)
        out = out_bd[0:tq]
        for h in range(1, N_HEADS):
            out = out + out_bd[h * tq:(h + 1) * tq]
        o_ref[...] = out


def _moba_sample(page_rows, q, k_new, v_new, tab_rows, cache_k, cache_v, *, nseq, tq, nb_past):
    assert nb_past <= LANES and N_HEADS * tq <= LANES
    cb = math.gcd(SAMPLE_CHUNK_BLOCKS, nb_past)
    npg = cb * PAGES_PER_BLOCK
    n_chunks = nb_past // cb
    ppseq = nb_past * PAGES_PER_BLOCK
    rows = N_HEADS * tq
    width = q.shape[1]
    tok = lambda b, c, pt: (b, 0)

    def page_spec(g):
        return pl.BlockSpec((1, width, PAGE_SIZE), lambda b, c, pt: (pt[b * ppseq + c * npg + g], 0, 0))

    grid_spec = pltpu.PrefetchScalarGridSpec(
        num_scalar_prefetch=1,
        grid=(nseq, n_chunks),
        in_specs=[pl.BlockSpec((tq, width), tok)] * 3
        + [pl.BlockSpec(tab_rows.shape, lambda b, c, pt: (0, 0))]
        + [page_spec(g) for g in range(npg)] * 2,
        out_specs=pl.BlockSpec((tq, width), tok),
        scratch_shapes=[pltpu.VMEM((width, LANES), F32),
                        pltpu.VMEM((rows, LANES), F32),
                        pltpu.VMEM((rows, LANES), F32),
                        pltpu.VMEM((nb_past, rows, width), F32)],
    )
    return pl.pallas_call(
        functools.partial(_moba_sample_kernel, tq=tq, nb_past=nb_past, cb=cb),
        grid_spec=grid_spec,
        out_shape=jax.ShapeDtypeStruct((nseq * tq, width), F32),
        compiler_params=pltpu.CompilerParams(dimension_semantics=("arbitrary",) * 2,
                                             vmem_limit_bytes=VMEM_LIMIT),
        name="moba_sample",
    )(page_rows, q, k_new, v_new, tab_rows, *([cache_k] * npg), *([cache_v] * npg))


def _mix_kernel(*refs, tm, seq_len, attn_transposed, has_state, alpha):
    if has_state:
        (x_ref, attn_ref, u_ref, halo_ref, st0_ref, st1_ref, cb_ref, ga_ref, gb_ref, cw_ref,
         wao_ref, wco_ref, wo_ref, g_ref, b_ref, x1_ref, ubuf) = refs
    else:
        (x_ref, attn_ref, u_ref, halo_ref, cb_ref, ga_ref, gb_ref, cw_ref,
         wao_ref, wco_ref, wo_ref, g_ref, b_ref, x1_ref, ubuf) = refs
    i = pl.program_id(0)
    if attn_transposed:
        attn = attn_ref[...].astype(F32).T.astype(MXU_DTYPE)
    else:
        attn = attn_ref[...].astype(MXU_DTYPE)
    y_att = jnp.dot(attn, wao_ref[...], preferred_element_type=F32)
    u = u_ref[...]
    ubuf[0:SUBLANES, :] = halo_ref[...]
    ubuf[SUBLANES:SUBLANES + tm, :] = u
    p1 = ubuf[SUBLANES - 1:SUBLANES - 1 + tm, :]
    p2 = ubuf[SUBLANES - 2:SUBLANES - 2 + tm, :]
    row = lax.broadcasted_iota(I32, (tm, 1), 0)
    if seq_len % tm == 0:
        pos = (i * tm) % seq_len + row
    else:
        pos = row % seq_len
    s0 = st0_ref[...] if has_state else 0.0
    s1 = st1_ref[...] if has_state else 0.0
    prev1 = jnp.where(pos >= 1, p1, s1)
    prev2 = jnp.where(pos >= 2, p2, jnp.where(pos == 1, s1, s0))
    conv = cw_ref[0:1, :] * prev2 + cw_ref[1:2, :] * prev1 + cw_ref[2:3, :] * u
    y_conv = jnp.dot((cb_ref[...].astype(F32) * conv).astype(MXU_DTYPE), wco_ref[...],
                     preferred_element_type=F32)
    merged = ga_ref[...].astype(F32) * y_att + gb_ref[...].astype(F32) * y_conv
    z = alpha * x_ref[...] + jnp.dot(merged.astype(MXU_DTYPE), wo_ref[...], preferred_element_type=F32)
    x1 = _layer_norm(z, g_ref[...], b_ref[...])
    x1_ref[...] = x1


def _mix(x2d, attn, u, state, cb, ga, gb, conv_w, wao, wco, wo, g, b, *, tm, seq_len, attn_transposed, alpha):
    t, d = x2d.shape
    cw = u.shape[1]
    assert seq_len % tm == 0 or tm % seq_len == 0
    nt = t // tm
    row = lambda i: (i, 0)
    full = lambda a: pl.BlockSpec(a.shape, lambda i: (0,) * a.ndim)
    has_state = state is not None
    hb = tm // SUBLANES
    attn_spec = (pl.BlockSpec((ATT_W, tm), lambda i: (0, i)) if attn_transposed
                 else pl.BlockSpec((tm, ATT_W), row))
    args = [x2d, attn, u, u]
    in_specs = [pl.BlockSpec((tm, d), row), attn_spec, pl.BlockSpec((tm, cw), row),
                pl.BlockSpec((SUBLANES, cw), lambda i: (jnp.maximum(i * hb - 1, 0), 0))]
    if has_state:
        args += list(state)
        in_specs += [pl.BlockSpec((tm, cw), row)] * 2
    args += [cb, ga, gb, conv_w, wao, wco, wo, g, b]
    in_specs += [pl.BlockSpec((tm, cw), row), pl.BlockSpec((tm, d), row), pl.BlockSpec((tm, d), row),
                 full(conv_w), full(wao), full(wco), full(wo), full(g), full(b)]
    return pl.pallas_call(
        functools.partial(_mix_kernel, tm=tm, seq_len=seq_len, attn_transposed=attn_transposed,
                          has_state=has_state, alpha=alpha),
        grid=(nt,),
        in_specs=in_specs,
        out_specs=pl.BlockSpec((tm, d), row),
        out_shape=jax.ShapeDtypeStruct((t, d), F32),
        scratch_shapes=[pltpu.VMEM((SUBLANES + tm, cw), F32)],
        compiler_params=pltpu.CompilerParams(dimension_semantics=("arbitrary",),
                                             vmem_limit_bytes=VMEM_LIMIT),
        name="mix",
    )(*args)


def _router_kernel(xp_ref, xs_ref, wT_ref, b_ref, idx_ref, gate_ref, rank_ref, cnt_ref, seen_ref, *, ntp):
    i = pl.program_id(0)
    ne, tm = wT_ref.shape[0], xp_ref.shape[0]
    x = jnp.where(i < ntp, xp_ref[...], xs_ref[...])

    @pl.when(i == 0)
    def _():
        seen_ref[...] = jnp.zeros(seen_ref.shape, F32)

    logits = lax.dot_general(wT_ref[...], x, _nt_dims(), preferred_element_type=F32,
                             precision=HIGHEST) + b_ref[...]
    eid = lax.broadcasted_iota(I32, (ne, tm), 0)
    cnt = jnp.zeros((ne, tm), I32)
    for e in range(ne):
        row = logits[e:e + 1, :]
        beats = (row > logits) | ((row == logits) & (e < eid))
        cnt = cnt + jnp.where(beats, 1, 0)
    sel = cnt < TOP_K
    lmax = jnp.max(logits, axis=0, keepdims=True)
    ex = jnp.where(sel, jnp.exp(logits - lmax), 0.0)
    gates = ex / jnp.sum(ex, axis=0, keepdims=True)
    self_ = jnp.where(sel, 1.0, 0.0)
    tt = lax.broadcasted_iota(I32, (tm, tm), 0) < lax.broadcasted_iota(I32, (tm, tm), 1)
    earlier = jnp.where(tt, 1.0, 0.0).astype(MXU_DTYPE)
    seen = seen_ref[:, 0:1]
    rank = jnp.dot(self_.astype(MXU_DTYPE), earlier, preferred_element_type=F32) + seen
    seen = seen + jnp.sum(self_, axis=1, keepdims=True)
    seen_ref[...] = jnp.broadcast_to(seen, seen_ref.shape)
    cnt_ref[...] = jnp.broadcast_to(seen, cnt_ref.shape)
    ee = lax.broadcasted_iota(I32, (ne, ne), 1) < lax.broadcasted_iota(I32, (ne, ne), 0)
    below = jnp.dot(jnp.where(ee, 1.0, 0.0).astype(MXU_DTYPE), self_.astype(MXU_DTYPE),
                    preferred_element_type=F32)
    eidf = eid.astype(F32)
    for kk in range(TOP_K):
        hit = sel & (below == float(kk))
        idx_ref[kk:kk + 1, :] = jnp.sum(jnp.where(hit, eidf, 0.0), axis=0, keepdims=True).astype(I32)
        gate_ref[kk:kk + 1, :] = jnp.sum(jnp.where(hit, gates, 0.0), axis=0, keepdims=True)
        rank_ref[kk:kk + 1, :] = jnp.sum(jnp.where(hit, rank, 0.0), axis=0, keepdims=True).astype(I32)


def _route(x1p, x1s, router_wT, router_b_col, *, tm):
    d = x1p.shape[1]
    ntp, nts = x1p.shape[0] // tm, x1s.shape[0] // tm
    t = (ntp + nts) * tm
    ne = router_wT.shape[0]
    col = lambda i: (0, i)
    return pl.pallas_call(
        functools.partial(_router_kernel, ntp=ntp),
        grid=(ntp + nts,),
        in_specs=[pl.BlockSpec((tm, d), lambda i: (jnp.minimum(i, ntp - 1), 0)),
                  pl.BlockSpec((tm, d), lambda i: (jnp.maximum(i - ntp, 0), 0)),
                  pl.BlockSpec((ne, d), lambda i: (0, 0)),
                  pl.BlockSpec((ne, 1), lambda i: (0, 0))],
        out_specs=[pl.BlockSpec((TOP_K, tm), col), pl.BlockSpec((TOP_K, tm), col),
                   pl.BlockSpec((TOP_K, tm), col), pl.BlockSpec((ne, 128), lambda i: (0, 0))],
        out_shape=[jax.ShapeDtypeStruct((TOP_K, t), I32), jax.ShapeDtypeStruct((TOP_K, t), F32),
                   jax.ShapeDtypeStruct((TOP_K, t), I32), jax.ShapeDtypeStruct((ne, 128), F32)],
        scratch_shapes=[pltpu.VMEM((ne, 128), F32)],
        compiler_params=pltpu.CompilerParams(dimension_semantics=("arbitrary",),
                                             vmem_limit_bytes=VMEM_LIMIT),
        name="router",
    )(x1p, x1s, router_wT, router_b_col)


def _dispatch_kernel(dest_ref, pend_ref, xp_ref, xs_ref, xb_ref, zeros_ref, sem, zsem, *, t, tm, tile, ne, ntp):
    i = pl.program_id(0)

    def row_copy(x_ref, r, dst):
        return pltpu.make_async_copy(x_ref.at[pl.ds(r, 1)], xb_ref.at[pl.ds(dst, 1)], sem)

    def clear_copy(e):
        start = pl.multiple_of(pend_ref[e] - tile, tile)
        return pltpu.make_async_copy(zeros_ref, xb_ref.at[pl.ds(start, tile)], zsem)

    @pl.when(i == 0)
    def _():
        zeros_ref[...] = jnp.zeros(zeros_ref.shape, F32)
        for phase in ("start", "wait"):
            for e in range(ne):
                has_rows = pend_ref[e] > (pend_ref[e - 1] if e else 0)

                @pl.when(has_rows)
                def _():
                    if phase == "start":
                        clear_copy(e).start()
                    else:
                        clear_copy(e).wait()

        def tail_copy(j):
            return pltpu.make_async_copy(zeros_ref, xb_ref.at[pl.ds(pl.multiple_of(j * tile, tile), tile)], zsem)

        first_unused = pend_ref[ne - 1] // tile
        n_tiles = xb_ref.shape[0] // tile
        lax.fori_loop(first_unused, n_tiles, lambda j, c: (tail_copy(j).start(), c)[1], 0)
        lax.fori_loop(first_unused, n_tiles, lambda j, c: (tail_copy(j).wait(), c)[1], 0)

    def issue_from(x_ref):
        def issue(r, carry):
            for kk in range(TOP_K):
                row_copy(x_ref, r, dest_ref[kk * t + i * tm + r]).start(priority=kk % 2)
            return carry

        lax.fori_loop(0, tm, issue, 0, unroll=DISPATCH_UNROLL)

    pl.when(i < ntp)(lambda: issue_from(xp_ref))
    pl.when(i >= ntp)(lambda: issue_from(xs_ref))
    for kk in range(TOP_K):
        pltpu.make_async_copy(xp_ref, xb_ref.at[pl.ds(0, tm)], sem).wait()


def _dispatch(dest_flat, pad_end, x1p, x1s, *, rows, tm, tile):
    d = x1p.shape[1]
    ntp, nts = x1p.shape[0] // tm, x1s.shape[0] // tm
    t = (ntp + nts) * tm
    ne = pad_end.shape[0]
    grid_spec = pltpu.PrefetchScalarGridSpec(
        num_scalar_prefetch=2,
        grid=(ntp + nts,),
        in_specs=[pl.BlockSpec((tm, d), lambda i, dest, pend: (jnp.minimum(i, ntp - 1), 0)),
                  pl.BlockSpec((tm, d), lambda i, dest, pend: (jnp.maximum(i - ntp, 0), 0))],
        out_specs=pl.BlockSpec(memory_space=pl.ANY),
        scratch_shapes=[pltpu.VMEM((tile, d), F32), pltpu.SemaphoreType.DMA, pltpu.SemaphoreType.DMA],
    )
    return pl.pallas_call(
        functools.partial(_dispatch_kernel, t=t, tm=tm, tile=tile, ne=ne, ntp=ntp),
        grid_spec=grid_spec,
        out_shape=jax.ShapeDtypeStruct((rows, d), F32),
        compiler_params=pltpu.CompilerParams(dimension_semantics=("arbitrary",),
                                             vmem_limit_bytes=VMEM_LIMIT),
        name="dispatch",
    )(dest_flat, pad_end, x1p, x1s)


def _expert_kernel(te_ref, nu_ref, x_ref, wgu_ref, bg_ref, bu_ref, wd_ref, bd_ref, y_ref, wg_s, wu_s, wd_s):
    i = pl.program_id(0)
    new_expert = jnp.logical_or(i == 0, te_ref[i] != te_ref[jnp.maximum(i - 1, 0)])

    @pl.when(jnp.logical_and(new_expert, i < nu_ref[0]))
    def _():
        half = LANES
        c = lax.broadcasted_iota(I32, (2 * half, 2 * half), 0)
        o = lax.broadcasted_iota(I32, (2 * half, 2 * half), 1)
        src = jnp.where(o < half, 2 * o, 2 * (o - half) + 1)
        perm = jnp.where(c == src, 1.0, 0.0).astype(MXU_DTYPE)
        for j in range(wgu_ref.shape[2] // (2 * half)):
            wb = wgu_ref[0, :, 2 * half * j:2 * half * (j + 1)].astype(MXU_DTYPE)
            y = jnp.dot(wb, perm, preferred_element_type=F32).astype(MXU_DTYPE)
            wg_s[:, half * j:half * (j + 1)] = y[:, :half]
            wu_s[:, half * j:half * (j + 1)] = y[:, half:]
        wd_s[...] = wd_ref[0].astype(MXU_DTYPE)

    @pl.when(i < nu_ref[0])
    def _():
        x = x_ref[...].astype(MXU_DTYPE)
        hg = jnp.dot(x, wg_s[...], preferred_element_type=F32) + bg_ref[0]
        hu = jnp.dot(x, wu_s[...], preferred_element_type=F32) + bu_ref[0]
        g = jnp.minimum(hg, SWIGLU_LIMIT)
        u = jnp.clip(hu, -SWIGLU_LIMIT, SWIGLU_LIMIT)
        a = g * _sigmoid(SWIGLU_ALPHA * g) * (u + 1.0)
        y_ref[...] = jnp.dot(a.astype(MXU_DTYPE), wd_s[...], preferred_element_type=F32) + bd_ref[0]

    @pl.when(i >= nu_ref[0])
    def _():
        y_ref[...] = jnp.zeros(y_ref.shape, F32)


def _experts(tile_e, n_used, xb, w_gate_up, bg, bu, w_down, bd, *, tm):
    rows, d = xb.shape
    dff = w_down.shape[1]
    nt = rows // tm
    ew = lambda i, te, nu: (te[i], 0, 0)
    grid_spec = pltpu.PrefetchScalarGridSpec(
        num_scalar_prefetch=2,
        grid=(nt,),
        in_specs=[pl.BlockSpec((tm, d), lambda i, te, nu: (jnp.minimum(i, nu[0] - 1), 0)),
                  pl.BlockSpec((1, d, 2 * dff), ew),
                  pl.BlockSpec((1, 1, dff), ew), pl.BlockSpec((1, 1, dff), ew),
                  pl.BlockSpec((1, dff, d), ew), pl.BlockSpec((1, 1, d), ew)],
        out_specs=pl.BlockSpec((tm, d), lambda i, te, nu: (i, 0)),
        scratch_shapes=[pltpu.VMEM((d, dff), MXU_DTYPE), pltpu.VMEM((d, dff), MXU_DTYPE),
                        pltpu.VMEM((dff, d), MXU_DTYPE)],
    )
    return pl.pallas_call(
        _expert_kernel,
        grid_spec=grid_spec,
        out_shape=jax.ShapeDtypeStruct((rows, d), F32),
        compiler_params=pltpu.CompilerParams(dimension_semantics=("arbitrary",),
                                             vmem_limit_bytes=VMEM_LIMIT),
        name="experts",
    )(tile_e, n_used, xb, w_gate_up, bg, bu, w_down, bd)


def _norm2_kernel(x1_ref, gate_ref, *refs, alpha):
    y_refs, (g_ref, b_ref, o_ref) = refs[:TOP_K], refs[TOP_K:]
    gate = gate_ref[...]
    ffn = gate[:, 0:1] * y_refs[0][...]
    for kk in range(1, TOP_K):
        ffn = ffn + gate[:, kk:kk + 1] * y_refs[kk][...]
    o_ref[...] = _layer_norm(alpha * x1_ref[...] + ffn, g_ref[...], b_ref[...])


def _norm2(x1, gate_t, y_pairs, g, b, *, row0, tm, alpha):
    nrows, d = x1.shape
    off = row0 // tm
    src = lambda i: (i + off, 0)
    return pl.pallas_call(
        functools.partial(_norm2_kernel, alpha=alpha),
        grid=(nrows // tm,),
        in_specs=[pl.BlockSpec((tm, d), lambda i: (i, 0)), pl.BlockSpec((tm, TOP_K), src)]
        + [pl.BlockSpec((tm, d), src)] * TOP_K
        + [pl.BlockSpec((1, d), lambda i: (0, 0)), pl.BlockSpec((1, d), lambda i: (0, 0))],
        out_specs=pl.BlockSpec((tm, d), lambda i: (i, 0)),
        out_shape=jax.ShapeDtypeStruct((nrows, d), F32),
        compiler_params=pltpu.CompilerParams(dimension_semantics=("arbitrary",),
                                             vmem_limit_bytes=VMEM_LIMIT),
        name="norm2",
    )(x1, gate_t, *y_pairs, g, b)


def _moe(x1p, x1s, router_w, router_b, w_gate_up, b_gate_up, w_down, b_down):
    t = x1p.shape[0] + x1s.shape[0]
    ne = router_w.shape[1]
    idx, gate, rank, counts = _route(x1p, x1s, router_w.T, router_b.reshape(ne, 1), tm=TOKEN_TILE)
    tm = EXPERT_TILE
    counts = counts[:, 0].astype(I32)
    padded = (counts + tm - 1) // tm * tm
    pad_end = jnp.cumsum(padded)
    pad_start = pad_end - padded
    n_tiles = (t * TOP_K + ne * (tm - 1)) // tm
    rows = n_tiles * tm
    eids = jnp.arange(ne, dtype=I32)
    start_of = jnp.sum(jnp.where(idx[:, :, None] == eids, pad_start, 0), axis=-1)
    dest = start_of + rank
    tile_row = jnp.arange(n_tiles, dtype=I32) * tm
    tile_e = jnp.minimum(jnp.sum((pad_end[None, :] <= tile_row[:, None]).astype(I32), axis=1), ne - 1)
    n_used = (pad_end[-1:] // tm).astype(I32)
    xb = _dispatch(dest.reshape(-1), pad_end.astype(I32), x1p, x1s, rows=rows, tm=TOKEN_TILE, tile=tm)
    bg = b_gate_up[:, None, 0::2]
    bu = b_gate_up[:, None, 1::2]
    yb = _experts(tile_e, n_used, xb, w_gate_up, bg, bu, w_down, b_down[:, None, :], tm=tm)
    return gate.T, [yb[dest[kk]] for kk in range(TOP_K)]


def kernel(x_prompt, x_sample, cache_k, cache_v, state_conv, page_table, rel_bias, w_in, conv_w, w_att_o,
           w_conv_o, w_o, ln1_g, ln1_b, router_w, router_b, w_gate_up, b_gate_up, w_down, b_down, ln2_g, ln2_b):
    depth = w_in.shape[0]
    batch, seq, d = x_prompt.shape
    nseq, tq, _ = x_sample.shape
    n_pool = cache_k.shape[1]
    ppseq = page_table.shape[1]
    past = ppseq * PAGE_SIZE
    assert seq % MOBA_BLOCK == 0 and past % MOBA_BLOCK == 0 and tq % SUBLANES == 0 and tq <= PAGE_SIZE
    nb_past = past // MOBA_BLOCK
    tp, ts = batch * seq, nseq * tq
    alpha = (2 * depth) ** 0.25
    cw = conv_w.shape[2]
    tab_rows = jnp.repeat(rel_bias.T, tq, axis=0)
    ck = cache_k.transpose(0, 1, 3, 4, 2).reshape(depth * n_pool, ATT_W, PAGE_SIZE)
    cv = cache_v.transpose(0, 1, 3, 4, 2).reshape(depth * n_pool, ATT_W, PAGE_SIZE)
    hp = x_prompt.reshape(tp, d)
    hs = x_sample.reshape(ts, d)
    outs = [[] for _ in range(6)]
    for l in range(depth):
        w_in_b = w_in[l].astype(MXU_DTYPE)
        wao, wco, wo = (w_att_o[l].astype(MXU_DTYPE), w_conv_o[l].astype(MXU_DTYPE), w_o[l].astype(MXU_DTYPE))
        g1, b1, g2, b2 = ln1_g[l][None], ln1_b[l][None], ln2_g[l][None], ln2_b[l][None]
        qT, khm, vT, kp, vp, km, up, cbp, gap, gbp = _project(hp, w_in_b, tm=PROJ_TILE, head_major=True, seq=seq)
        nbt = PROJ_TILE // MOBA_BLOCK
        kmean = km[:, :nbt].reshape(batch, seq // MOBA_BLOCK, N_HEADS, HEAD_DIM).transpose(0, 2, 1, 3)
        kmean = kmean.reshape(batch * N_HEADS, seq // MOBA_BLOCK, HEAD_DIM)
        attn_p = _moba_prompt(rel_bias, qT, khm, vT, kmean, batch=batch, seq=seq)
        x1p = _mix(hp, attn_p, up, None, cbp, gap, gbp, conv_w[l], wao, wco, wo, g1, b1,
                         tm=PROJ_TILE, seq_len=seq, attn_transposed=True, alpha=alpha)
        qs, ks, vs, us, cbs, gas, gbs = _project(hs, w_in_b, tm=ts, head_major=False, wq=w_in[l][:, :ATT_W])
        pages = (page_table + l * n_pool).reshape(-1).astype(I32)
        attn_s = _moba_sample(pages, qs, ks, vs, tab_rows, ck, cv, nseq=nseq, tq=tq, nb_past=nb_past)
        state = (jnp.repeat(state_conv[l][:, 0], tq, axis=0), jnp.repeat(state_conv[l][:, 1], tq, axis=0))
        x1s = _mix(hs, attn_s, us, state, cbs, gas, gbs, conv_w[l], wao, wco, wo, g1, b1,
                         tm=ts, seq_len=tq, attn_transposed=False, alpha=alpha)
        gate_t, y_pairs = _moe(x1p, x1s, router_w[l], router_b[l], w_gate_up[l], b_gate_up[l], w_down[l], b_down[l])
        hp = _norm2(x1p, gate_t, y_pairs, g2, b2, row0=0, tm=TOKEN_TILE, alpha=alpha)
        hs = _norm2(x1s, gate_t, y_pairs, g2, b2, row0=tp, tm=TOKEN_TILE, alpha=alpha)
        outs[0].append(kp.reshape(batch, N_HEADS, HEAD_DIM, seq).transpose(0, 3, 1, 2))
        outs[1].append(vp.reshape(batch, N_HEADS, HEAD_DIM, seq).transpose(0, 3, 1, 2))
        outs[2].append(up.reshape(batch, seq, cw)[:, seq - (CONV_K - 1):])
        outs[3].append(ks.reshape(nseq, tq, N_HEADS, HEAD_DIM))
        outs[4].append(vs.reshape(nseq, tq, N_HEADS, HEAD_DIM))
        outs[5].append(us.reshape(nseq, tq, cw)[:, tq - (CONV_K - 1):])
    stacked = [jnp.stack(o) for o in outs]
    return (hp.reshape(batch, seq, d), hs.reshape(nseq, tq, d), *stacked)
```

```python
import functools
import math

import jax
import jax.numpy as jnp
from jax import lax
from jax.experimental import pallas as pl
from jax.experimental.pallas import tpu as pltpu

F32 = jnp.float32
I32 = jnp.int32
MXU_DTYPE = jnp.bfloat16
HIGHEST = lax.Precision.HIGHEST

N_HEADS = 8
HEAD_DIM = 64
ATT_W = N_HEADS * HEAD_DIM
MOBA_BLOCK = 256
MOBA_TOPK = 3
PAGE_SIZE = 128
PAGES_PER_BLOCK = MOBA_BLOCK // PAGE_SIZE
CONV_K = 3
N_EXPERTS = 32
TOP_K = 4
SWIGLU_LIMIT = 7.0
SWIGLU_ALPHA = 1.702
NUM_BUCKETS = 32
NUM_EXACT = NUM_BUCKETS // 2
MAX_DISTANCE = 128
LN_EPS = 1e-5
SCORE_SCALE = HEAD_DIM ** -0.5
NEG = -1e30

SUBLANES = 8
LANES = 128
PACKED_ROWS = 16
V_ROWS = HEAD_DIM + PACKED_ROWS

PROJ_TILE = 512
TOKEN_TILE = 256
EXPERT_TILE = 512
DISPATCH_UNROLL = 8
SAMPLE_CHUNK_BLOCKS = 16
VMEM_LIMIT = 56 * 1024 * 1024

assert MOBA_BLOCK >= MAX_DISTANCE


def _nt_dims():
    return (((1,), (1,)), ((), ()))


def _sigmoid(x):
    return 1.0 / (1.0 + jnp.exp(-x))


def _bucket(dist):
    n = jnp.maximum(dist, 0)
    nf = jnp.maximum(n, 1).astype(F32)
    large = NUM_EXACT + (jnp.log(nf / NUM_EXACT) / math.log(MAX_DISTANCE / NUM_EXACT)
                         * (NUM_BUCKETS - NUM_EXACT)).astype(I32)
    large = jnp.minimum(large, NUM_BUCKETS - 1)
    return jnp.where(n < NUM_EXACT, n, large)


def _layer_norm(z, g, b):
    mu = jnp.mean(z, axis=-1, keepdims=True)
    zc = z - mu
    var = jnp.mean(zc * zc, axis=-1, keepdims=True)
    return zc * lax.rsqrt(var + LN_EPS) * g + b


def _proj_kernel(x_ref, w_ref, *refs, tm, head_major, blocks_per_seq):
    xb = x_ref[...].astype(MXU_DTYPE)

    def mm(c0, width):
        return jnp.dot(xb, w_ref[:, c0:c0 + width], preferred_element_type=F32)

    if head_major:
        qT_ref, khm_ref, vT_ref, k_ref, v_ref, km_ref, u_ref, cb_ref, ga_ref, gb_ref = refs
        q = mm(0, ATT_W) * SCORE_SCALE
    else:
        wq_ref, q_ref, k_ref, v_ref, u_ref, cb_ref, ga_ref, gb_ref = refs
        q = jnp.dot(x_ref[...], wq_ref[...], preferred_element_type=F32, precision=HIGHEST) * SCORE_SCALE
    d_model = x_ref.shape[1]
    k = mm(ATT_W, ATT_W)
    v = mm(2 * ATT_W, ATT_W)
    if head_major:
        qT_ref[...] = q.T
        k_ref[0] = k.T
        vt = v.T
        v_ref[0] = vt
        km_ref[...] = jnp.zeros(km_ref.shape, F32)
        lane = lax.broadcasted_iota(I32, (MOBA_BLOCK, 2 * HEAD_DIM), 1)
        ones_rows = jnp.ones((V_ROWS - HEAD_DIM, MOBA_BLOCK), MXU_DTYPE)
        for r in range(tm // MOBA_BLOCK):
            kr = k[r * MOBA_BLOCK:(r + 1) * MOBA_BLOCK]
            km_ref[0, r:r + 1, :] = jnp.sum(kr, axis=0, keepdims=True) * (1.0 / MOBA_BLOCK)
            n_blk = (pl.program_id(0) * (tm // MOBA_BLOCK) + r) % blocks_per_seq
            tag = jnp.where(lane - HEAD_DIM == n_blk, 1.0, 0.0)
            vtr = vt[:, r * MOBA_BLOCK:(r + 1) * MOBA_BLOCK]
            for h in range(N_HEADS):
                pair = kr[:, (h // 2) * 2 * HEAD_DIM:(h // 2 + 1) * 2 * HEAD_DIM]
                if h % 2:
                    pair = pltpu.roll(pair, HEAD_DIM, axis=1)
                khm_ref[h, r] = jnp.where(lane < HEAD_DIM, pair, tag).astype(MXU_DTYPE)
                vT_ref[r, h * V_ROWS:h * V_ROWS + HEAD_DIM] = vtr[h * HEAD_DIM:(h + 1) * HEAD_DIM].astype(MXU_DTYPE)
                vT_ref[r, h * V_ROWS + HEAD_DIM:(h + 1) * V_ROWS] = ones_rows
    else:
        q_ref[...] = q
        k_ref[...] = k
        v_ref[...] = v
    c0 = 3 * ATT_W
    cw = u_ref.shape[1]
    cb_ref[...] = mm(c0, cw).astype(cb_ref.dtype)
    u_ref[...] = mm(c0 + cw, cw) * mm(c0 + 2 * cw, cw)
    ga_ref[...] = _sigmoid(mm(c0 + 3 * cw, d_model)).astype(ga_ref.dtype)
    gb_ref[...] = _sigmoid(mm(c0 + 3 * cw + d_model, d_model)).astype(gb_ref.dtype)


def _project(x2d, w_in_b, *, tm, head_major, seq=None, wq=None):
    t, d = x2d.shape
    cw = (w_in_b.shape[1] - 3 * ATT_W - 2 * d) // 3
    nt = t // tm
    row = lambda i: (i, 0)
    f32s = lambda shape: jax.ShapeDtypeStruct(shape, F32)
    mxs = lambda shape: jax.ShapeDtypeStruct(shape, MXU_DTYPE)
    tail_shapes = [f32s((t, cw)), mxs((t, cw)), mxs((t, d)), mxs((t, d))]
    tail_specs = [pl.BlockSpec((tm, cw), row), pl.BlockSpec((tm, cw), row),
                  pl.BlockSpec((tm, d), row), pl.BlockSpec((tm, d), row)]
    if head_major:
        nbt = tm // MOBA_BLOCK
        tps = seq // tm
        seq_t = pl.BlockSpec((1, ATT_W, tm), lambda i: (i // tps, 0, i % tps))
        out_shape = [f32s((ATT_W, t)), mxs((N_HEADS, t // MOBA_BLOCK, MOBA_BLOCK, 2 * HEAD_DIM)),
                     mxs((t // MOBA_BLOCK, N_HEADS * V_ROWS, MOBA_BLOCK)), f32s((t // seq, ATT_W, seq)),
                     f32s((t // seq, ATT_W, seq)), f32s((nt, SUBLANES, ATT_W))] + tail_shapes
        out_specs = [pl.BlockSpec((ATT_W, tm), lambda i: (0, i)),
                     pl.BlockSpec((N_HEADS, nbt, MOBA_BLOCK, 2 * HEAD_DIM), lambda i: (0, i, 0, 0)),
                     pl.BlockSpec((nbt, N_HEADS * V_ROWS, MOBA_BLOCK), lambda i: (i, 0, 0)),
                     seq_t, seq_t,
                     pl.BlockSpec((1, SUBLANES, ATT_W), lambda i: (i, 0, 0))] + tail_specs
        args, extra_specs = (x2d, w_in_b), []
    else:
        out_shape = [f32s((t, ATT_W))] * 3 + tail_shapes
        out_specs = [pl.BlockSpec((tm, ATT_W), row)] * 3 + tail_specs
        args, extra_specs = (x2d, w_in_b, wq), [pl.BlockSpec(wq.shape, lambda i: (0, 0))]
    return pl.pallas_call(
        functools.partial(_proj_kernel, tm=tm, head_major=head_major,
                          blocks_per_seq=seq // MOBA_BLOCK if head_major else None),
        grid=(nt,),
        in_specs=[pl.BlockSpec((tm, d), row), pl.BlockSpec(w_in_b.shape, lambda i: (0, 0))] + extra_specs,
        out_specs=out_specs,
        out_shape=out_shape,
        compiler_params=pltpu.CompilerParams(dimension_semantics=("arbitrary",),
                                             vmem_limit_bytes=VMEM_LIMIT),
        name="proj",
    )(*args)


def _moba_prompt_kernel(tab_ref, qT_ref, k_ref, vT_ref, km_ref, o_ref, bias_ref, qb_ref, *state):
    h = pl.program_id(0)
    first_sequence = pl.program_id(1) == 0
    nb = km_ref.shape[1]
    blk = MOBA_BLOCK
    n_streams = nb // 2
    m_refs, acc_refs = state[:n_streams], state[n_streams:]

    @pl.when(first_sequence)
    def _():
        kk = lax.broadcasted_iota(I32, (blk, blk), 0)
        qq = lax.broadcasted_iota(I32, (blk, blk), 1)
        for age in range(2):
            dist = qq - kk + age * blk
            bucket = _bucket(dist)
            tile = jnp.zeros((blk, blk), F32)
            for b in range(NUM_BUCKETS):
                tile = jnp.where(bucket == b, tab_ref[b, h], tile)
            bias_ref[age] = jnp.where(dist >= 0, tile, NEG)
        bias_ref[2] = jnp.full((blk, blk), tab_ref[NUM_BUCKETS - 1, h], F32)

    km = km_ref[0]
    nid = lax.broadcasted_iota(I32, (nb, blk), 0)
    spare = jnp.zeros((HEAD_DIM - nb, blk), F32)
    for i in range(nb):
        qT = qT_ref[:, i * blk:(i + 1) * blk]
        gT = jnp.dot(km, qT, preferred_element_type=F32, precision=HIGHEST)
        cnt = jnp.zeros((nb, blk), I32)
        for m in range(i):
            row = gT[m:m + 1, :]
            beats = (row > gT) | ((row == gT) & (m < nid))
            cnt = cnt + jnp.where(beats, 1, 0)
        keep = ((cnt < MOBA_TOPK) & (nid < i)) | (nid == i)
        qb_ref[i] = jnp.concatenate([qT, jnp.where(keep, 0.0, NEG), spare], axis=0).astype(MXU_DTYPE)

    for a in range(n_streams):
        m_refs[a][...] = jnp.full(m_refs[a].shape, NEG, F32)
        acc_refs[a][...] = jnp.zeros(acc_refs[a].shape, F32)

    def sweep_round(t, carry):
        staged = []
        for a in range(n_streams):
            first = t <= a
            slot = jnp.where(first, 0, 1)
            i = jnp.where(first, a, nb - 1 - a)
            n = jnp.where(first, a - t, nb - t)
            age = jnp.minimum(i - n, 2)
            s = jnp.dot(k_ref[0, n], qb_ref[i], preferred_element_type=F32) + bias_ref[age]
            staged.append((slot, n, s))
        updates = []
        for a, (slot, n, s) in enumerate(staged):
            m_old = m_refs[a][slot]
            m_new = jnp.maximum(m_old, jnp.max(s, axis=0, keepdims=True))
            alpha = jnp.exp(m_old - m_new)
            p = jnp.exp(s - m_new)
            acc_new = alpha * acc_refs[a][slot] + jnp.dot(
                vT_ref[n], p.astype(MXU_DTYPE), preferred_element_type=F32)
            updates.append((slot, m_new, acc_new))
        for a, (slot, m_new, acc_new) in enumerate(updates):
            m_refs[a][slot] = m_new
            acc_refs[a][slot] = acc_new
        return carry

    lax.fori_loop(0, nb + 1, sweep_round, 0)
    for a in range(n_streams):
        for slot, i in ((0, a), (1, nb - 1 - a)):
            acc = acc_refs[a][slot]
            o_ref[:, i * blk:(i + 1) * blk] = (acc[:HEAD_DIM] / acc[HEAD_DIM:HEAD_DIM + 1]).astype(o_ref.dtype)


def _moba_prompt(rel_bias, qT, khm, vT, kmean, *, batch, seq):
    nb = seq // MOBA_BLOCK
    assert nb % 2 == 0 and nb <= HEAD_DIM
    t = batch * seq
    stream = lambda shape: [pltpu.VMEM((2,) + shape, F32)] * (nb // 2)
    return pl.pallas_call(
        _moba_prompt_kernel,
        grid=(N_HEADS, batch),
        in_specs=[
            pl.BlockSpec(memory_space=pltpu.SMEM),
            pl.BlockSpec((HEAD_DIM, seq), lambda h, b: (h, b)),
            pl.BlockSpec((1, nb, MOBA_BLOCK, 2 * HEAD_DIM), lambda h, b: (h, b, 0, 0)),
            pl.BlockSpec((nb, V_ROWS, MOBA_BLOCK), lambda h, b: (b, h, 0)),
            pl.BlockSpec((1, nb, HEAD_DIM), lambda h, b: (b * N_HEADS + h, 0, 0)),
        ],
        out_specs=pl.BlockSpec((HEAD_DIM, seq), lambda h, b: (h, b)),
        out_shape=jax.ShapeDtypeStruct((ATT_W, t), MXU_DTYPE),
        scratch_shapes=[pltpu.VMEM((3, MOBA_BLOCK, MOBA_BLOCK), F32),
                        pltpu.VMEM((nb, 2 * HEAD_DIM, MOBA_BLOCK), MXU_DTYPE)]
        + stream((1, MOBA_BLOCK)) + stream((V_ROWS, MOBA_BLOCK)),
        compiler_params=pltpu.CompilerParams(dimension_semantics=("arbitrary",) * 2,
                                             vmem_limit_bytes=VMEM_LIMIT),
        name="moba_prompt",
    )(rel_bias, qT, khm, vT, kmean)


def _moba_sample_kernel(pt_ref, q_ref, kn_ref, vn_ref, tab_ref, *refs, tq, nb_past, cb):
    del pt_ref
    npg = cb * PAGES_PER_BLOCK
    k_pages = refs[:npg]
    v_pages = refs[npg:2 * npg]
    o_ref, km_ref, m_ref, l_ref, acc_ref = refs[2 * npg:]
    c = pl.program_id(1)
    n_chunks = nb_past // cb
    rows = N_HEADS * tq
    width = q_ref.shape[1]
    r_id = lax.broadcasted_iota(I32, (rows, width), 0)
    lane = lax.broadcasted_iota(I32, (rows, width), 1)
    head_mask = (lane // HEAD_DIM) == (r_id // tq)
    q_rows = jnp.concatenate([q_ref[...]] * N_HEADS, axis=0)
    qbd = jnp.where(head_mask, q_rows, 0.0)
    qbd_b = qbd.astype(MXU_DTYPE)
    tab = tab_ref[...]

    def bias_rows(dist):
        bucket = _bucket(dist)
        out = jnp.zeros(dist.shape, F32)
        for b in range(NUM_BUCKETS):
            out = jnp.where(bucket == b, tab[:, b:b + 1], out)
        return out

    c_far = tab[:, NUM_BUCKETS - 1:NUM_BUCKETS]

    @pl.when(c == 0)
    def _():
        km_ref[...] = jnp.zeros(km_ref.shape, F32)
        m_ref[...] = jnp.full(m_ref.shape, NEG, F32)
        l_ref[...] = jnp.zeros(l_ref.shape, F32)

    km_lane = lax.broadcasted_iota(I32, km_ref.shape, 1)
    st_lane = lax.broadcasted_iota(I32, m_ref.shape, 1)
    qi = lax.broadcasted_iota(I32, (rows, MOBA_BLOCK), 0) % tq
    kj = lax.broadcasted_iota(I32, (rows, MOBA_BLOCK), 1)
    last_bias = lax.cond(c == n_chunks - 1, lambda: bias_rows(MOBA_BLOCK + qi - kj),
                         lambda: jnp.broadcast_to(c_far, (rows, MOBA_BLOCK)))
    scores = []
    km_new = jnp.zeros(km_ref.shape, F32)
    for j in range(cb):
        kts = [k_pages[PAGES_PER_BLOCK * j + g][0] for g in range(PAGES_PER_BLOCK)]
        ksum = kts[0]
        for kt in kts[1:]:
            ksum = ksum + kt
        kmean = jnp.sum(ksum, axis=1, keepdims=True) * (1.0 / MOBA_BLOCK)
        km_new = jnp.where(km_lane == c * cb + j, kmean, km_new)
        scores.append(jnp.concatenate(
            [jnp.dot(qbd_b, kt.astype(MXU_DTYPE), preferred_element_type=F32) for kt in kts], axis=1))
    km_ref[...] = km_ref[...] + km_new
    probs = []
    m_new = jnp.full(m_ref.shape, NEG, F32)
    l_new = jnp.zeros(l_ref.shape, F32)
    for j, s in enumerate(scores):
        s = s + (last_bias if j == cb - 1 else c_far)
        m_n = jnp.max(s, axis=1, keepdims=True)
        p = jnp.exp(s - m_n)
        m_new = jnp.where(st_lane == c * cb + j, m_n, m_new)
        l_new = jnp.where(st_lane == c * cb + j, jnp.sum(p, axis=1, keepdims=True), l_new)
        probs.append(p.astype(MXU_DTYPE))
    m_ref[...] = jnp.maximum(m_ref[...], m_new)
    l_ref[...] = l_ref[...] + l_new
    for j, pb in enumerate(probs):
        acc = None
        for g in range(PAGES_PER_BLOCK):
            vt = v_pages[PAGES_PER_BLOCK * j + g][0]
            part = lax.dot_general(pb[:, g * PAGE_SIZE:(g + 1) * PAGE_SIZE], vt.astype(MXU_DTYPE), _nt_dims(),
                                   preferred_element_type=F32)
            acc = part if acc is None else acc + part
        acc_ref[c * cb + j] = acc

    @pl.when(c == n_chunks - 1)
    def _():
        lanes = LANES
        nbp = -(-nb_past // SUBLANES) * SUBLANES
        q_pad = jnp.concatenate([qbd, jnp.zeros((lanes - rows, width), F32)], axis=0)
        gT = lax.dot_general(km_ref[...].T, q_pad, _nt_dims(), preferred_element_type=F32,
                             precision=HIGHEST)[:nbp]
        nid = lax.broadcasted_iota(I32, gT.shape, 0)
        cnt = jnp.zeros(gT.shape, I32)
        for m in range(nb_past):
            row = gT[m:m + 1, :]
            beats = (row > gT) | ((row == gT) & (m < nid))
            cnt = cnt + jnp.where(beats, 1, 0)
        selT = jnp.where((cnt < MOBA_TOPK) & (nid < nb_past), 1.0, 0.0)
        selT = jnp.concatenate([selT, jnp.zeros((lanes - nbp, lanes), F32)], axis=0)
        sel = selT.T[:rows] > 0.5
        pad = jnp.zeros((PAGE_SIZE - tq, width), F32)
        kn = jnp.concatenate([kn_ref[...], pad], axis=0)
        vn = jnp.concatenate([vn_ref[...], pad], axis=0)
        qi = lax.broadcasted_iota(I32, (rows, PAGE_SIZE), 0) % tq
        kj = lax.broadcasted_iota(I32, (rows, PAGE_SIZE), 1)
        s = lax.dot_general(qbd_b, kn.astype(MXU_DTYPE), _nt_dims(), preferred_element_type=F32)
        s = jnp.where(kj <= qi, s + bias_rows(qi - kj), NEG)
        m_o = jnp.max(s, axis=1, keepdims=True)
        p = jnp.exp(s - m_o)
        l_o = jnp.sum(p, axis=1, keepdims=True)
        acc_o = jnp.dot(p.astype(MXU_DTYPE), vn.astype(MXU_DTYPE), preferred_element_type=F32)
        m_blk = m_ref[...]
        m_all = jnp.maximum(m_o, jnp.max(jnp.where(sel, m_blk, NEG), axis=1, keepdims=True))
        w_blk = jnp.where(sel, jnp.exp(m_blk - m_all), 0.0)
        w_o = jnp.exp(m_o - m_all)
        l_all = w_o * l_o + jnp.sum(w_blk * l_ref[...], axis=1, keepdims=True)
        acc_all = w_o * acc_o
        for n in range(nb_past):
            acc_all = acc_all + w_blk[:, n:n + 1] * acc_ref[n]
        out_bd = jnp.where(head_mask, acc_all / l_all, 0.0)
        out = out_bd[0:tq]
        for h in range(1, N_HEADS):
            out = out + out_bd[h * tq:(h + 1) * tq]
        o_ref[...] = out


def _moba_sample(page_rows, q, k_new, v_new, tab_rows, cache_k, cache_v, *, nseq, tq, nb_past):
    assert nb_past <= LANES and N_HEADS * tq <= LANES
    cb = math.gcd(SAMPLE_CHUNK_BLOCKS, nb_past)
    npg = cb * PAGES_PER_BLOCK
    n_chunks = nb_past // cb
    ppseq = nb_past * PAGES_PER_BLOCK
    rows = N_HEADS * tq
    width = q.shape[1]
    tok = lambda b, c, pt: (b, 0)

    def page_spec(g):
        return pl.BlockSpec((1, width, PAGE_SIZE), lambda b, c, pt: (pt[b * ppseq + c * npg + g], 0, 0))

    grid_spec = pltpu.PrefetchScalarGridSpec(
        num_scalar_prefetch=1,
        grid=(nseq, n_chunks),
        in_specs=[pl.BlockSpec((tq, width), tok)] * 3
        + [pl.BlockSpec(tab_rows.shape, lambda b, c, pt: (0, 0))]
        + [page_spec(g) for g in range(npg)] * 2,
        out_specs=pl.BlockSpec((tq, width), tok),
        scratch_shapes=[pltpu.VMEM((width, LANES), F32),
                        pltpu.VMEM((rows, LANES), F32),
                        pltpu.VMEM((rows, LANES), F32),
                        pltpu.VMEM((nb_past, rows, width), F32)],
    )
    return pl.pallas_call(
        functools.partial(_moba_sample_kernel, tq=tq, nb_past=nb_past, cb=cb),
        grid_spec=grid_spec,
        out_shape=jax.ShapeDtypeStruct((nseq * tq, width), F32),
        compiler_params=pltpu.CompilerParams(dimension_semantics=("arbitrary",) * 2,
                                             vmem_limit_bytes=VMEM_LIMIT),
        name="moba_sample",
    )(page_rows, q, k_new, v_new, tab_rows, *([cache_k] * npg), *([cache_v] * npg))


def _mix_kernel(*refs, tm, seq_len, attn_transposed, has_state, alpha):
    if has_state:
        (x_ref, attn_ref, u_ref, halo_ref, st0_ref, st1_ref, cb_ref, ga_ref, gb_ref, cw_ref,
         wao_ref, wco_ref, wo_ref, g_ref, b_ref, x1_ref, ubuf) = refs
    else:
        (x_ref, attn_ref, u_ref, halo_ref, cb_ref, ga_ref, gb_ref, cw_ref,
         wao_ref, wco_ref, wo_ref, g_ref, b_ref, x1_ref, ubuf) = refs
    i = pl.program_id(0)
    if attn_transposed:
        attn = attn_ref[...].astype(F32).T.astype(MXU_DTYPE)
    else:
        attn = attn_ref[...].astype(MXU_DTYPE)
    y_att = jnp.dot(attn, wao_ref[...], preferred_element_type=F32)
    u = u_ref[...]
    ubuf[0:SUBLANES, :] = halo_ref[...]
    ubuf[SUBLANES:SUBLANES + tm, :] = u
    p1 = ubuf[SUBLANES - 1:SUBLANES - 1 + tm, :]
    p2 = ubuf[SUBLANES - 2:SUBLANES - 2 + tm, :]
    row = lax.broadcasted_iota(I32, (tm, 1), 0)
    if seq_len % tm == 0:
        pos = (i * tm) % seq_len + row
    else:
        pos = row % seq_len
    s0 = st0_ref[...] if has_state else 0.0
    s1 = st1_ref[...] if has_state else 0.0
    prev1 = jnp.where(pos >= 1, p1, s1)
    prev2 = jnp.where(pos >= 2, p2, jnp.where(pos == 1, s1, s0))
    conv = cw_ref[0:1, :] * prev2 + cw_ref[1:2, :] * prev1 + cw_ref[2:3, :] * u
    y_conv = jnp.dot((cb_ref[...].astype(F32) * conv).astype(MXU_DTYPE), wco_ref[...],
                     preferred_element_type=F32)
    merged = ga_ref[...].astype(F32) * y_att + gb_ref[...].astype(F32) * y_conv
    z = alpha * x_ref[...] + jnp.dot(merged.astype(MXU_DTYPE), wo_ref[...], preferred_element_type=F32)
    x1 = _layer_norm(z, g_ref[...], b_ref[...])
    x1_ref[...] = x1


def _mix(x2d, attn, u, state, cb, ga, gb, conv_w, wao, wco, wo, g, b, *, tm, seq_len, attn_transposed, alpha):
    t, d = x2d.shape
    cw = u.shape[1]
    assert seq_len % tm == 0 or tm % seq_len == 0
    nt = t // tm
    row = lambda i: (i, 0)
    full = lambda a: pl.BlockSpec(a.shape, lambda i: (0,) * a.ndim)
    has_state = state is not None
    hb = tm // SUBLANES
    attn_spec = (pl.BlockSpec((ATT_W, tm), lambda i: (0, i)) if attn_transposed
                 else pl.BlockSpec((tm, ATT_W), row))
    args = [x2d, attn, u, u]
    in_specs = [pl.BlockSpec((tm, d), row), attn_spec, pl.BlockSpec((tm, cw), row),
                pl.BlockSpec((SUBLANES, cw), lambda i: (jnp.maximum(i * hb - 1, 0), 0))]
    if has_state:
        args += list(state)
        in_specs += [pl.BlockSpec((tm, cw), row)] * 2
    args += [cb, ga, gb, conv_w, wao, wco, wo, g, b]
    in_specs += [pl.BlockSpec((tm, cw), row), pl.BlockSpec((tm, d), row), pl.BlockSpec((tm, d), row),
                 full(conv_w), full(wao), full(wco), full(wo), full(g), full(b)]
    return pl.pallas_call(
        functools.partial(_mix_kernel, tm=tm, seq_len=seq_len, attn_transposed=attn_transposed,
                          has_state=has_state, alpha=alpha),
        grid=(nt,),
        in_specs=in_specs,
        out_specs=pl.BlockSpec((tm, d), row),
        out_shape=jax.ShapeDtypeStruct((t, d), F32),
        scratch_shapes=[pltpu.VMEM((SUBLANES + tm, cw), F32)],
        compiler_params=pltpu.CompilerParams(dimension_semantics=("arbitrary",),
                                             vmem_limit_bytes=VMEM_LIMIT),
        name="mix",
    )(*args)


def _router_kernel(xp_ref, xs_ref, wT_ref, b_ref, idx_ref, gate_ref, rank_ref, cnt_ref, seen_ref, *, ntp):
    i = pl.program_id(0)
    ne, tm = wT_ref.shape[0], xp_ref.shape[0]
    x = jnp.where(i < ntp, xp_ref[...], xs_ref[...])

    @pl.when(i == 0)
    def _():
        seen_ref[...] = jnp.zeros(seen_ref.shape, F32)

    logits = lax.dot_general(wT_ref[...], x, _nt_dims(), preferred_element_type=F32,
                             precision=HIGHEST) + b_ref[...]
    eid = lax.broadcasted_iota(I32, (ne, tm), 0)
    cnt = jnp.zeros((ne, tm), I32)
    for e in range(ne):
        row = logits[e:e + 1, :]
        beats = (row > logits) | ((row == logits) & (e < eid))
        cnt = cnt + jnp.where(beats, 1, 0)
    sel = cnt < TOP_K
    lmax = jnp.max(logits, axis=0, keepdims=True)
    ex = jnp.where(sel, jnp.exp(logits - lmax), 0.0)
    gates = ex / jnp.sum(ex, axis=0, keepdims=True)
    self_ = jnp.where(sel, 1.0, 0.0)
    tt = lax.broadcasted_iota(I32, (tm, tm), 0) < lax.broadcasted_iota(I32, (tm, tm), 1)
    earlier = jnp.where(tt, 1.0, 0.0).astype(MXU_DTYPE)
    seen = seen_ref[:, 0:1]
    rank = jnp.dot(self_.astype(MXU_DTYPE), earlier, preferred_element_type=F32) + seen
    seen = seen + jnp.sum(self_, axis=1, keepdims=True)
    seen_ref[...] = jnp.broadcast_to(seen, seen_ref.shape)
    cnt_ref[...] = jnp.broadcast_to(seen, cnt_ref.shape)
    ee = lax.broadcasted_iota(I32, (ne, ne), 1) < lax.broadcasted_iota(I32, (ne, ne), 0)
    below = jnp.dot(jnp.where(ee, 1.0, 0.0).astype(MXU_DTYPE), self_.astype(MXU_DTYPE),
                    preferred_element_type=F32)
    eidf = eid.astype(F32)
    for kk in range(TOP_K):
        hit = sel & (below == float(kk))
        idx_ref[kk:kk + 1, :] = jnp.sum(jnp.where(hit, eidf, 0.0), axis=0, keepdims=True).astype(I32)
        gate_ref[kk:kk + 1, :] = jnp.sum(jnp.where(hit, gates, 0.0), axis=0, keepdims=True)
        rank_ref[kk:kk + 1, :] = jnp.sum(jnp.where(hit, rank, 0.0), axis=0, keepdims=True).astype(I32)


def _route(x1p, x1s, router_wT, router_b_col, *, tm):
    d = x1p.shape[1]
    ntp, nts = x1p.shape[0] // tm, x1s.shape[0] // tm
    t = (ntp + nts) * tm
    ne = router_wT.shape[0]
    col = lambda i: (0, i)
    return pl.pallas_call(
        functools.partial(_router_kernel, ntp=ntp),
        grid=(ntp + nts,),
        in_specs=[pl.BlockSpec((tm, d), lambda i: (jnp.minimum(i, ntp - 1), 0)),
                  pl.BlockSpec((tm, d), lambda i: (jnp.maximum(i - ntp, 0), 0)),
                  pl.BlockSpec((ne, d), lambda i: (0, 0)),
                  pl.BlockSpec((ne, 1), lambda i: (0, 0))],
        out_specs=[pl.BlockSpec((TOP_K, tm), col), pl.BlockSpec((TOP_K, tm), col),
                   pl.BlockSpec((TOP_K, tm), col), pl.BlockSpec((ne, LANES), lambda i: (0, 0))],
        out_shape=[jax.ShapeDtypeStruct((TOP_K, t), I32), jax.ShapeDtypeStruct((TOP_K, t), F32),
                   jax.ShapeDtypeStruct((TOP_K, t), I32), jax.ShapeDtypeStruct((ne, LANES), F32)],
        scratch_shapes=[pltpu.VMEM((ne, LANES), F32)],
        compiler_params=pltpu.CompilerParams(dimension_semantics=("arbitrary",),
                                             vmem_limit_bytes=VMEM_LIMIT),
        name="router",
    )(x1p, x1s, router_wT, router_b_col)


def _dispatch_kernel(dest_ref, pend_ref, xp_ref, xs_ref, xb_ref, zeros_ref, sem, zsem, *, t, tm, tile, ne, ntp):
    i = pl.program_id(0)

    def row_copy(x_ref, r, dst):
        return pltpu.make_async_copy(x_ref.at[pl.ds(r, 1)], xb_ref.at[pl.ds(dst, 1)], sem)

    def clear_copy(e):
        start = pl.multiple_of(pend_ref[e] - tile, tile)
        return pltpu.make_async_copy(zeros_ref, xb_ref.at[pl.ds(start, tile)], zsem)

    @pl.when(i == 0)
    def _():
        zeros_ref[...] = jnp.zeros(zeros_ref.shape, F32)
        for phase in ("start", "wait"):
            for e in range(ne):
                has_rows = pend_ref[e] > (pend_ref[e - 1] if e else 0)

                @pl.when(has_rows)
                def _():
                    if phase == "start":
                        clear_copy(e).start()
                    else:
                        clear_copy(e).wait()

        def tail_copy(j):
            return pltpu.make_async_copy(zeros_ref, xb_ref.at[pl.ds(pl.multiple_of(j * tile, tile), tile)], zsem)

        first_unused = pend_ref[ne - 1] // tile
        n_tiles = xb_ref.shape[0] // tile
        lax.fori_loop(first_unused, n_tiles, lambda j, c: (tail_copy(j).start(), c)[1], 0)
        lax.fori_loop(first_unused, n_tiles, lambda j, c: (tail_copy(j).wait(), c)[1], 0)

    def issue_from(x_ref):
        def issue(r, carry):
            for kk in range(TOP_K):
                row_copy(x_ref, r, dest_ref[kk * t + i * tm + r]).start(priority=kk % 2)
            return carry

        lax.fori_loop(0, tm, issue, 0, unroll=DISPATCH_UNROLL)

    pl.when(i < ntp)(lambda: issue_from(xp_ref))
    pl.when(i >= ntp)(lambda: issue_from(xs_ref))
    for kk in range(TOP_K):
        pltpu.make_async_copy(xp_ref, xb_ref.at[pl.ds(0, tm)], sem).wait()


def _dispatch(dest_flat, pad_end, x1p, x1s, *, rows, tm, tile):
    d = x1p.shape[1]
    ntp, nts = x1p.shape[0] // tm, x1s.shape[0] // tm
    t = (ntp + nts) * tm
    ne = pad_end.shape[0]
    grid_spec = pltpu.PrefetchScalarGridSpec(
        num_scalar_prefetch=2,
        grid=(ntp + nts,),
        in_specs=[pl.BlockSpec((tm, d), lambda i, dest, pend: (jnp.minimum(i, ntp - 1), 0)),
                  pl.BlockSpec((tm, d), lambda i, dest, pend: (jnp.maximum(i - ntp, 0), 0))],
        out_specs=pl.BlockSpec(memory_space=pl.ANY),
        scratch_shapes=[pltpu.VMEM((tile, d), F32), pltpu.SemaphoreType.DMA, pltpu.SemaphoreType.DMA],
    )
    return pl.pallas_call(
        functools.partial(_dispatch_kernel, t=t, tm=tm, tile=tile, ne=ne, ntp=ntp),
        grid_spec=grid_spec,
        out_shape=jax.ShapeDtypeStruct((rows, d), F32),
        compiler_params=pltpu.CompilerParams(dimension_semantics=("arbitrary",),
                                             vmem_limit_bytes=VMEM_LIMIT),
        name="dispatch",
    )(dest_flat, pad_end, x1p, x1s)


def _expert_kernel(te_ref, nu_ref, x_ref, wgu_ref, bg_ref, bu_ref, wd_ref, bd_ref, y_ref, wg_s, wu_s, wd_s):
    i = pl.program_id(0)
    new_expert = jnp.logical_or(i == 0, te_ref[i] != te_ref[jnp.maximum(i - 1, 0)])

    @pl.when(jnp.logical_and(new_expert, i < nu_ref[0]))
    def _():
        half = LANES
        c = lax.broadcasted_iota(I32, (2 * half, 2 * half), 0)
        o = lax.broadcasted_iota(I32, (2 * half, 2 * half), 1)
        src = jnp.where(o < half, 2 * o, 2 * (o - half) + 1)
        perm = jnp.where(c == src, 1.0, 0.0).astype(MXU_DTYPE)
        for j in range(wgu_ref.shape[2] // (2 * half)):
            wb = wgu_ref[0, :, 2 * half * j:2 * half * (j + 1)].astype(MXU_DTYPE)
            y = jnp.dot(wb, perm, preferred_element_type=F32).astype(MXU_DTYPE)
            wg_s[:, half * j:half * (j + 1)] = y[:, :half]
            wu_s[:, half * j:half * (j + 1)] = y[:, half:]
        wd_s[...] = wd_ref[0].astype(MXU_DTYPE)

    @pl.when(i < nu_ref[0])
    def _():
        x = x_ref[...].astype(MXU_DTYPE)
        hg = jnp.dot(x, wg_s[...], preferred_element_type=F32) + bg_ref[0]
        hu = jnp.dot(x, wu_s[...], preferred_element_type=F32) + bu_ref[0]
        g = jnp.minimum(hg, SWIGLU_LIMIT)
        u = jnp.clip(hu, -SWIGLU_LIMIT, SWIGLU_LIMIT)
        a = g * _sigmoid(SWIGLU_ALPHA * g) * (u + 1.0)
        y_ref[...] = jnp.dot(a.astype(MXU_DTYPE), wd_s[...], preferred_element_type=F32) + bd_ref[0]

    @pl.when(i >= nu_ref[0])
    def _():
        y_ref[...] = jnp.zeros(y_ref.shape, F32)


def _experts(tile_e, n_used, xb, w_gate_up, bg, bu, w_down, bd, *, tm):
    rows, d = xb.shape
    dff = w_down.shape[1]
    nt = rows // tm
    ew = lambda i, te, nu: (te[i], 0, 0)
    grid_spec = pltpu.PrefetchScalarGridSpec(
        num_scalar_prefetch=2,
        grid=(nt,),
        in_specs=[pl.BlockSpec((tm, d), lambda i, te, nu: (jnp.minimum(i, nu[0] - 1), 0)),
                  pl.BlockSpec((1, d, 2 * dff), ew),
                  pl.BlockSpec((1, 1, dff), ew), pl.BlockSpec((1, 1, dff), ew),
                  pl.BlockSpec((1, dff, d), ew), pl.BlockSpec((1, 1, d), ew)],
        out_specs=pl.BlockSpec((tm, d), lambda i, te, nu: (i, 0)),
        scratch_shapes=[pltpu.VMEM((d, dff), MXU_DTYPE), pltpu.VMEM((d, dff), MXU_DTYPE),
                        pltpu.VMEM((dff, d), MXU_DTYPE)],
    )
    return pl.pallas_call(
        _expert_kernel,
        grid_spec=grid_spec,
        out_shape=jax.ShapeDtypeStruct((rows, d), F32),
        compiler_params=pltpu.CompilerParams(dimension_semantics=("arbitrary",),
                                             vmem_limit_bytes=VMEM_LIMIT),
        name="experts",
    )(tile_e, n_used, xb, w_gate_up, bg, bu, w_down, bd)


def _norm2_kernel(x1_ref, gate_ref, *refs, alpha):
    y_refs, (g_ref, b_ref, o_ref) = refs[:TOP_K], refs[TOP_K:]
    gate = gate_ref[...]
    ffn = gate[:, 0:1] * y_refs[0][...]
    for kk in range(1, TOP_K):
        ffn = ffn + gate[:, kk:kk + 1] * y_refs[kk][...]
    o_ref[...] = _layer_norm(alpha * x1_ref[...] + ffn, g_ref[...], b_ref[...])


def _norm2(x1, gate_t, y_pairs, g, b, *, row0, tm, alpha):
    nrows, d = x1.shape
    off = row0 // tm
    src = lambda i: (i + off, 0)
    return pl.pallas_call(
        functools.partial(_norm2_kernel, alpha=alpha),
        grid=(nrows // tm,),
        in_specs=[pl.BlockSpec((tm, d), lambda i: (i, 0)), pl.BlockSpec((tm, TOP_K), src)]
        + [pl.BlockSpec((tm, d), src)] * TOP_K
        + [pl.BlockSpec((1, d), lambda i: (0, 0)), pl.BlockSpec((1, d), lambda i: (0, 0))],
        out_specs=pl.BlockSpec((tm, d), lambda i: (i, 0)),
        out_shape=jax.ShapeDtypeStruct((nrows, d), F32),
        compiler_params=pltpu.CompilerParams(dimension_semantics=("arbitrary",),
                                             vmem_limit_bytes=VMEM_LIMIT),
        name="norm2",
    )(x1, gate_t, *y_pairs, g, b)


def _moe(x1p, x1s, router_w, router_b, w_gate_up, b_gate_up, w_down, b_down):
    t = x1p.shape[0] + x1s.shape[0]
    ne = router_w.shape[1]
    idx, gate, rank, counts = _route(x1p, x1s, router_w.T, router_b.reshape(ne, 1), tm=TOKEN_TILE)
    tm = EXPERT_TILE
    counts = counts[:, 0].astype(I32)
    padded = (counts + tm - 1) // tm * tm
    pad_end = jnp.cumsum(padded)
    pad_start = pad_end - padded
    n_tiles = (t * TOP_K + ne * (tm - 1)) // tm
    rows = n_tiles * tm
    eids = jnp.arange(ne, dtype=I32)
    start_of = jnp.sum(jnp.where(idx[:, :, None] == eids, pad_start, 0), axis=-1)
    dest = start_of + rank
    tile_row = jnp.arange(n_tiles, dtype=I32) * tm
    tile_e = jnp.minimum(jnp.sum((pad_end[None, :] <= tile_row[:, None]).astype(I32), axis=1), ne - 1)
    n_used = (pad_end[-1:] // tm).astype(I32)
    xb = _dispatch(dest.reshape(-1), pad_end.astype(I32), x1p, x1s, rows=rows, tm=TOKEN_TILE, tile=tm)
    bg = b_gate_up[:, None, 0::2]
    bu = b_gate_up[:, None, 1::2]
    yb = _experts(tile_e, n_used, xb, w_gate_up, bg, bu, w_down, b_down[:, None, :], tm=tm)
    return gate.T, [yb[dest[kk]] for kk in range(TOP_K)]


def kernel(x_prompt, x_sample, cache_k, cache_v, state_conv, page_table, rel_bias, w_in, conv_w, w_att_o,
           w_conv_o, w_o, ln1_g, ln1_b, router_w, router_b, w_gate_up, b_gate_up, w_down, b_down, ln2_g, ln2_b):
    depth = w_in.shape[0]
    batch, seq, d = x_prompt.shape
    nseq, tq, _ = x_sample.shape
    n_pool = cache_k.shape[1]
    ppseq = page_table.shape[1]
    past = ppseq * PAGE_SIZE
    assert seq % MOBA_BLOCK == 0 and past % MOBA_BLOCK == 0 and tq % SUBLANES == 0 and tq <= PAGE_SIZE
    nb_past = past // MOBA_BLOCK
    tp, ts = batch * seq, nseq * tq
    alpha = (2 * depth) ** 0.25
    cw = conv_w.shape[2]
    tab_rows = jnp.repeat(rel_bias.T, tq, axis=0)
    ck = cache_k.transpose(0, 1, 3, 4, 2).reshape(depth * n_pool, ATT_W, PAGE_SIZE)
    cv = cache_v.transpose(0, 1, 3, 4, 2).reshape(depth * n_pool, ATT_W, PAGE_SIZE)
    hp = x_prompt.reshape(tp, d)
    hs = x_sample.reshape(ts, d)
    outs = [[] for _ in range(6)]
    for l in range(depth):
        w_in_b = w_in[l].astype(MXU_DTYPE)
        wao, wco, wo = (w_att_o[l].astype(MXU_DTYPE), w_conv_o[l].astype(MXU_DTYPE), w_o[l].astype(MXU_DTYPE))
        g1, b1, g2, b2 = ln1_g[l][None], ln1_b[l][None], ln2_g[l][None], ln2_b[l][None]
        qT, khm, vT, kp, vp, km, up, cbp, gap, gbp = _project(hp, w_in_b, tm=PROJ_TILE, head_major=True, seq=seq)
        nbt = PROJ_TILE // MOBA_BLOCK
        kmean = km[:, :nbt].reshape(batch, seq // MOBA_BLOCK, N_HEADS, HEAD_DIM).transpose(0, 2, 1, 3)
        kmean = kmean.reshape(batch * N_HEADS, seq // MOBA_BLOCK, HEAD_DIM)
        attn_p = _moba_prompt(rel_bias, qT, khm, vT, kmean, batch=batch, seq=seq)
        x1p = _mix(hp, attn_p, up, None, cbp, gap, gbp, conv_w[l], wao, wco, wo, g1, b1,
                         tm=PROJ_TILE, seq_len=seq, attn_transposed=True, alpha=alpha)
        qs, ks, vs, us, cbs, gas, gbs = _project(hs, w_in_b, tm=ts, head_major=False, wq=w_in[l][:, :ATT_W])
        pages = (page_table + l * n_pool).reshape(-1).astype(I32)
        attn_s = _moba_sample(pages, qs, ks, vs, tab_rows, ck, cv, nseq=nseq, tq=tq, nb_past=nb_past)
        state = (jnp.repeat(state_conv[l][:, 0], tq, axis=0), jnp.repeat(state_conv[l][:, 1], tq, axis=0))
        x1s = _mix(hs, attn_s, us, state, cbs, gas, gbs, conv_w[l], wao, wco, wo, g1, b1,
                         tm=ts, seq_len=tq, attn_transposed=False, alpha=alpha)
        gate_t, y_pairs = _moe(x1p, x1s, router_w[l], router_b[l], w_gate_up[l], b_gate_up[l], w_down[l], b_down[l])
        hp = _norm2(x1p, gate_t, y_pairs, g2, b2, row0=0, tm=TOKEN_TILE, alpha=alpha)
        hs = _norm2(x1s, gate_t, y_pairs, g2, b2, row0=tp, tm=TOKEN_TILE, alpha=alpha)
        outs[0].append(kp.reshape(batch, N_HEADS, HEAD_DIM, seq).transpose(0, 3, 1, 2))
        outs[1].append(vp.reshape(batch, N_HEADS, HEAD_DIM, seq).transpose(0, 3, 1, 2))
        outs[2].append(up.reshape(batch, seq, cw)[:, seq - (CONV_K - 1):])
        outs[3].append(ks.reshape(nseq, tq, N_HEADS, HEAD_DIM))
        outs[4].append(vs.reshape(nseq, tq, N_HEADS, HEAD_DIM))
        outs[5].append(us.reshape(nseq, tq, cw)[:, tq - (CONV_K - 1):])
    stacked = [jnp.stack(o) for o in outs]
    return (hp.reshape(batch, seq, d), hs.reshape(nseq, tq, d), *stacked)
```

```python
import functools
import math

import jax
import jax.numpy as jnp
from jax import lax
from jax.experimental import pallas as pl
from jax.experimental.pallas import tpu as pltpu

F32 = jnp.float32
I32 = jnp.int32
MXU_DTYPE = jnp.bfloat16
HIGHEST = lax.Precision.HIGHEST

N_HEADS = 8
HEAD_DIM = 64
ATT_W = N_HEADS * HEAD_DIM
MOBA_BLOCK = 256
MOBA_TOPK = 3
PAGE_SIZE = 128
PAGES_PER_BLOCK = MOBA_BLOCK // PAGE_SIZE
CONV_K = 3
N_EXPERTS = 32
TOP_K = 4
SWIGLU_LIMIT = 7.0
SWIGLU_ALPHA = 1.702
NUM_BUCKETS = 32
NUM_EXACT = NUM_BUCKETS // 2
MAX_DISTANCE = 128
LN_EPS = 1e-5
SCORE_SCALE = HEAD_DIM ** -0.5
NEG = -1e30

SUBLANES = 8
LANES = 128
PACKED_ROWS = 16
V_ROWS = HEAD_DIM + PACKED_ROWS

PROJ_TILE = 512
TOKEN_TILE = 256
EXPERT_TILE = 512
DISPATCH_UNROLL = 8
SAMPLE_CHUNK_BLOCKS = 16
VMEM_LIMIT = 56 * 1024 * 1024

assert MOBA_BLOCK >= MAX_DISTANCE


def _nt_dims():
    return (((1,), (1,)), ((), ()))


def _sigmoid(x):
    return 1.0 / (1.0 + jnp.exp(-x))


def _bucket(dist):
    n = jnp.maximum(dist, 0)
    nf = jnp.maximum(n, 1).astype(F32)
    large = NUM_EXACT + (jnp.log(nf / NUM_EXACT) / math.log(MAX_DISTANCE / NUM_EXACT)
                         * (NUM_BUCKETS - NUM_EXACT)).astype(I32)
    large = jnp.minimum(large, NUM_BUCKETS - 1)
    return jnp.where(n < NUM_EXACT, n, large)


def _layer_norm(z, g, b):
    mu = jnp.mean(z, axis=-1, keepdims=True)
    zc = z - mu
    var = jnp.mean(zc * zc, axis=-1, keepdims=True)
    return zc * lax.rsqrt(var + LN_EPS) * g + b


def _proj_kernel(x_ref, w_ref, *refs, tm, head_major, blocks_per_seq):
    xb = x_ref[...].astype(MXU_DTYPE)

    def mm(c0, width):
        return jnp.dot(xb, w_ref[:, c0:c0 + width], preferred_element_type=F32)

    if head_major:
        qT_ref, khm_ref, vT_ref, k_ref, v_ref, km_ref, u_ref, cb_ref, ga_ref, gb_ref = refs
        q = mm(0, ATT_W) * SCORE_SCALE
    else:
        wq_ref, q_ref, k_ref, v_ref, u_ref, cb_ref, ga_ref, gb_ref = refs
        q = jnp.dot(x_ref[...], wq_ref[...], preferred_element_type=F32, precision=HIGHEST) * SCORE_SCALE
    d_model = x_ref.shape[1]
    k = mm(ATT_W, ATT_W)
    v = mm(2 * ATT_W, ATT_W)
    if head_major:
        qT_ref[...] = q.T
        k_ref[0] = k.T
        vt = v.T
        v_ref[0] = vt
        km_ref[...] = jnp.zeros(km_ref.shape, F32)
        lane = lax.broadcasted_iota(I32, (MOBA_BLOCK, 2 * HEAD_DIM), 1)
        ones_rows = jnp.ones((V_ROWS - HEAD_DIM, MOBA_BLOCK), MXU_DTYPE)
        for r in range(tm // MOBA_BLOCK):
            kr = k[r * MOBA_BLOCK:(r + 1) * MOBA_BLOCK]
            km_ref[0, r:r + 1, :] = jnp.sum(kr, axis=0, keepdims=True) * (1.0 / MOBA_BLOCK)
            n_blk = (pl.program_id(0) * (tm // MOBA_BLOCK) + r) % blocks_per_seq
            tag = jnp.where(lane - HEAD_DIM == n_blk, 1.0, 0.0)
            vtr = vt[:, r * MOBA_BLOCK:(r + 1) * MOBA_BLOCK]
            for h in range(N_HEADS):
                pair = kr[:, (h // 2) * 2 * HEAD_DIM:(h // 2 + 1) * 2 * HEAD_DIM]
                if h % 2:
                    pair = pltpu.roll(pair, HEAD_DIM, axis=1)
                khm_ref[h, r] = jnp.where(lane < HEAD_DIM, pair, tag).astype(MXU_DTYPE)
                vT_ref[r, h * V_ROWS:h * V_ROWS + HEAD_DIM] = vtr[h * HEAD_DIM:(h + 1) * HEAD_DIM].astype(MXU_DTYPE)
                vT_ref[r, h * V_ROWS + HEAD_DIM:(h + 1) * V_ROWS] = ones_rows
    else:
        q_ref[...] = q
        k_ref[...] = k
        v_ref[...] = v
    c0 = 3 * ATT_W
    cw = u_ref.shape[1]
    cb_ref[...] = mm(c0, cw).astype(cb_ref.dtype)
    u_ref[...] = mm(c0 + cw, cw) * mm(c0 + 2 * cw, cw)
    ga_ref[...] = _sigmoid(mm(c0 + 3 * cw, d_model)).astype(ga_ref.dtype)
    gb_ref[...] = _sigmoid(mm(c0 + 3 * cw + d_model, d_model)).astype(gb_ref.dtype)


def _project(x2d, w_in_b, *, tm, head_major, seq=None, wq=None):
    t, d = x2d.shape
    cw = (w_in_b.shape[1] - 3 * ATT_W - 2 * d) // 3
    nt = t // tm
    row = lambda i: (i, 0)
    f32s = lambda shape: jax.ShapeDtypeStruct(shape, F32)
    mxs = lambda shape: jax.ShapeDtypeStruct(shape, MXU_DTYPE)
    tail_shapes = [f32s((t, cw)), mxs((t, cw)), mxs((t, d)), mxs((t, d))]
    tail_specs = [pl.BlockSpec((tm, cw), row), pl.BlockSpec((tm, cw), row),
                  pl.BlockSpec((tm, d), row), pl.BlockSpec((tm, d), row)]
    if head_major:
        nbt = tm // MOBA_BLOCK
        tps = seq // tm
        seq_t = pl.BlockSpec((1, ATT_W, tm), lambda i: (i // tps, 0, i % tps))
        out_shape = [f32s((ATT_W, t)), mxs((N_HEADS, t // MOBA_BLOCK, MOBA_BLOCK, 2 * HEAD_DIM)),
                     mxs((t // MOBA_BLOCK, N_HEADS * V_ROWS, MOBA_BLOCK)), f32s((t // seq, ATT_W, seq)),
                     f32s((t // seq, ATT_W, seq)), f32s((nt, SUBLANES, ATT_W))] + tail_shapes
        out_specs = [pl.BlockSpec((ATT_W, tm), lambda i: (0, i)),
                     pl.BlockSpec((N_HEADS, nbt, MOBA_BLOCK, 2 * HEAD_DIM), lambda i: (0, i, 0, 0)),
                     pl.BlockSpec((nbt, N_HEADS * V_ROWS, MOBA_BLOCK), lambda i: (i, 0, 0)),
                     seq_t, seq_t,
                     pl.BlockSpec((1, SUBLANES, ATT_W), lambda i: (i, 0, 0))] + tail_specs
        args, extra_specs = (x2d, w_in_b), []
    else:
        out_shape = [f32s((t, ATT_W))] * 3 + tail_shapes
        out_specs = [pl.BlockSpec((tm, ATT_W), row)] * 3 + tail_specs
        args, extra_specs = (x2d, w_in_b, wq), [pl.BlockSpec(wq.shape, lambda i: (0, 0))]
    return pl.pallas_call(
        functools.partial(_proj_kernel, tm=tm, head_major=head_major,
                          blocks_per_seq=seq // MOBA_BLOCK if head_major else None),
        grid=(nt,),
        in_specs=[pl.BlockSpec((tm, d), row), pl.BlockSpec(w_in_b.shape, lambda i: (0, 0))] + extra_specs,
        out_specs=out_specs,
        out_shape=out_shape,
        compiler_params=pltpu.CompilerParams(dimension_semantics=("arbitrary",),
                                             vmem_limit_bytes=VMEM_LIMIT),
        name="proj",
    )(*args)


def _moba_prompt_kernel(tab_ref, qT_ref, k_ref, vT_ref, km_ref, o_ref, bias_ref, qb_ref, *state):
    h = pl.program_id(0)
    first_sequence = pl.program_id(1) == 0
    nb = km_ref.shape[1]
    blk = MOBA_BLOCK
    n_streams = nb // 2
    m_refs, acc_refs = state[:n_streams], state[n_streams:]

    @pl.when(first_sequence)
    def _():
        kk = lax.broadcasted_iota(I32, (blk, blk), 0)
        qq = lax.broadcasted_iota(I32, (blk, blk), 1)
        for age in range(2):
            dist = qq - kk + age * blk
            bucket = _bucket(dist)
            tile = jnp.zeros((blk, blk), F32)
            for b in range(NUM_BUCKETS):
                tile = jnp.where(bucket == b, tab_ref[b, h], tile)
            bias_ref[age] = jnp.where(dist >= 0, tile, NEG)
        bias_ref[2] = jnp.full((blk, blk), tab_ref[NUM_BUCKETS - 1, h], F32)

    km = km_ref[0]
    nid = lax.broadcasted_iota(I32, (nb, blk), 0)
    spare = jnp.zeros((HEAD_DIM - nb, blk), F32)
    for i in range(nb):
        qT = qT_ref[:, i * blk:(i + 1) * blk]
        gT = jnp.dot(km, qT, preferred_element_type=F32, precision=HIGHEST)
        cnt = jnp.zeros((nb, blk), I32)
        for m in range(i):
            row = gT[m:m + 1, :]
            beats = (row > gT) | ((row == gT) & (m < nid))
            cnt = cnt + jnp.where(beats, 1, 0)
        keep = ((cnt < MOBA_TOPK) & (nid < i)) | (nid == i)
        qb_ref[i] = jnp.concatenate([qT, jnp.where(keep, 0.0, NEG), spare], axis=0).astype(MXU_DTYPE)

    for a in range(n_streams):
        m_refs[a][...] = jnp.full(m_refs[a].shape, NEG, F32)
        acc_refs[a][...] = jnp.zeros(acc_refs[a].shape, F32)

    def sweep_round(t, carry):
        staged = []
        for a in range(n_streams):
            first = t <= a
            slot = jnp.where(first, 0, 1)
            i = jnp.where(first, a, nb - 1 - a)
            n = jnp.where(first, a - t, nb - t)
            age = jnp.minimum(i - n, 2)
            s = jnp.dot(k_ref[0, n], qb_ref[i], preferred_element_type=F32) + bias_ref[age]
            staged.append((slot, n, s))
        updates = []
        for a, (slot, n, s) in enumerate(staged):
            m_old = m_refs[a][slot]
            m_new = jnp.maximum(m_old, jnp.max(s, axis=0, keepdims=True))
            alpha = jnp.exp(m_old - m_new)
            p = jnp.exp(s - m_new)
            acc_new = alpha * acc_refs[a][slot] + jnp.dot(
                vT_ref[n], p.astype(MXU_DTYPE), preferred_element_type=F32)
            updates.append((slot, m_new, acc_new))
        for a, (slot, m_new, acc_new) in enumerate(updates):
            m_refs[a][slot] = m_new
            acc_refs[a][slot] = acc_new
        return carry

    lax.fori_loop(0, nb + 1, sweep_round, 0)
    for a in range(n_streams):
        for slot, i in ((0, a), (1, nb - 1 - a)):
            acc = acc_refs[a][slot]
            o_ref[:, i * blk:(i + 1) * blk] = (acc[:HEAD_DIM] / acc[HEAD_DIM:HEAD_DIM + 1]).astype(o_ref.dtype)


def _moba_prompt(rel_bias, qT, khm, vT, kmean, *, batch, seq):
    nb = seq // MOBA_BLOCK
    assert nb % 2 == 0 and nb <= HEAD_DIM
    t = batch * seq
    stream = lambda shape: [pltpu.VMEM((2,) + shape, F32)] * (nb // 2)
    return pl.pallas_call(
        _moba_prompt_kernel,
        grid=(N_HEADS, batch),
        in_specs=[
            pl.BlockSpec(memory_space=pltpu.SMEM),
            pl.BlockSpec((HEAD_DIM, seq), lambda h, b: (h, b)),
            pl.BlockSpec((1, nb, MOBA_BLOCK, 2 * HEAD_DIM), lambda h, b: (h, b, 0, 0)),
            pl.BlockSpec((nb, V_ROWS, MOBA_BLOCK), lambda h, b: (b, h, 0)),
            pl.BlockSpec((1, nb, HEAD_DIM), lambda h, b: (b * N_HEADS + h, 0, 0)),
        ],
        out_specs=pl.BlockSpec((HEAD_DIM, seq), lambda h, b: (h, b)),
        out_shape=jax.ShapeDtypeStruct((ATT_W, t), MXU_DTYPE),
        scratch_shapes=[pltpu.VMEM((3, MOBA_BLOCK, MOBA_BLOCK), F32),
                        pltpu.VMEM((nb, 2 * HEAD_DIM, MOBA_BLOCK), MXU_DTYPE)]
        + stream((1, MOBA_BLOCK)) + stream((V_ROWS, MOBA_BLOCK)),
        compiler_params=pltpu.CompilerParams(dimension_semantics=("arbitrary",) * 2,
                                             vmem_limit_bytes=VMEM_LIMIT),
        name="moba_prompt",
    )(rel_bias, qT, khm, vT, kmean)


def _moba_sample_kernel(pt_ref, q_ref, kn_ref, vn_ref, tab_ref, *refs, tq, nb_past, cb):
    del pt_ref
    npg = cb * PAGES_PER_BLOCK
    k_pages = refs[:npg]
    v_pages = refs[npg:2 * npg]
    o_ref, km_ref, m_ref, l_ref, acc_ref = refs[2 * npg:]
    c = pl.program_id(1)
    n_chunks = nb_past // cb
    rows = N_HEADS * tq
    width = q_ref.shape[1]
    r_id = lax.broadcasted_iota(I32, (rows, width), 0)
    lane = lax.broadcasted_iota(I32, (rows, width), 1)
    head_mask = (lane // HEAD_DIM) == (r_id // tq)
    q_rows = jnp.concatenate([q_ref[...]] * N_HEADS, axis=0)
    qbd = jnp.where(head_mask, q_rows, 0.0)
    qbd_b = qbd.astype(MXU_DTYPE)
    tab = tab_ref[...]

    def bias_rows(dist):
        bucket = _bucket(dist)
        out = jnp.zeros(dist.shape, F32)
        for b in range(NUM_BUCKETS):
            out = jnp.where(bucket == b, tab[:, b:b + 1], out)
        return out

    c_far = tab[:, NUM_BUCKETS - 1:NUM_BUCKETS]

    @pl.when(c == 0)
    def _():
        km_ref[...] = jnp.zeros(km_ref.shape, F32)
        m_ref[...] = jnp.full(m_ref.shape, NEG, F32)
        l_ref[...] = jnp.zeros(l_ref.shape, F32)

    km_lane = lax.broadcasted_iota(I32, km_ref.shape, 1)
    st_lane = lax.broadcasted_iota(I32, m_ref.shape, 1)
    qi = lax.broadcasted_iota(I32, (rows, MOBA_BLOCK), 0) % tq
    kj = lax.broadcasted_iota(I32, (rows, MOBA_BLOCK), 1)
    last_bias = lax.cond(c == n_chunks - 1, lambda: bias_rows(MOBA_BLOCK + qi - kj),
                         lambda: jnp.broadcast_to(c_far, (rows, MOBA_BLOCK)))
    scores = []
    km_new = jnp.zeros(km_ref.shape, F32)
    for j in range(cb):
        kts = [k_pages[PAGES_PER_BLOCK * j + g][0] for g in range(PAGES_PER_BLOCK)]
        ksum = kts[0]
        for kt in kts[1:]:
            ksum = ksum + kt
        kmean = jnp.sum(ksum, axis=1, keepdims=True) * (1.0 / MOBA_BLOCK)
        km_new = jnp.where(km_lane == c * cb + j, kmean, km_new)
        scores.append(jnp.concatenate(
            [jnp.dot(qbd_b, kt.astype(MXU_DTYPE), preferred_element_type=F32) for kt in kts], axis=1))
    km_ref[...] = km_ref[...] + km_new
    probs = []
    m_new = jnp.full(m_ref.shape, NEG, F32)
    l_new = jnp.zeros(l_ref.shape, F32)
    for j, s in enumerate(scores):
        s = s + (last_bias if j == cb - 1 else c_far)
        m_n = jnp.max(s, axis=1, keepdims=True)
        p = jnp.exp(s - m_n)
        m_new = jnp.where(st_lane == c * cb + j, m_n, m_new)
        l_new = jnp.where(st_lane == c * cb + j, jnp.sum(p, axis=1, keepdims=True), l_new)
        probs.append(p.astype(MXU_DTYPE))
    m_ref[...] = jnp.maximum(m_ref[...], m_new)
    l_ref[...] = l_ref[...] + l_new
    for j, pb in enumerate(probs):
        acc = None
        for g in range(PAGES_PER_BLOCK):
            vt = v_pages[PAGES_PER_BLOCK * j + g][0]
            part = lax.dot_general(pb[:, g * PAGE_SIZE:(g + 1) * PAGE_SIZE], vt.astype(MXU_DTYPE), _nt_dims(),
                                   preferred_element_type=F32)
            acc = part if acc is None else acc + part
        acc_ref[c * cb + j] = acc

    @pl.when(c == n_chunks - 1)
    def _():
        lanes = LANES
        nbp = -(-nb_past // SUBLANES) * SUBLANES
        q_pad = jnp.concatenate([qbd, jnp.zeros((lanes - rows, width), F32)], axis=0)
        gT = lax.dot_general(km_ref[...].T, q_pad, _nt_dims(), preferred_element_type=F32,
                             precision=HIGHEST)[:nbp]
        nid = lax.broadcasted_iota(I32, gT.shape, 0)
        cnt = jnp.zeros(gT.shape, I32)
        for m in range(nb_past):
            row = gT[m:m + 1, :]
            beats = (row > gT) | ((row == gT) & (m < nid))
            cnt = cnt + jnp.where(beats, 1, 0)
        selT = jnp.where((cnt < MOBA_TOPK) & (nid < nb_past), 1.0, 0.0)
        selT = jnp.concatenate([selT, jnp.zeros((lanes - nbp, lanes), F32)], axis=0)
        sel = selT.T[:rows] > 0.5
        pad = jnp.zeros((PAGE_SIZE - tq, width), F32)
        kn = jnp.concatenate([kn_ref[...], pad], axis=0)
        vn = jnp.concatenate([vn_ref[...], pad], axis=0)
        qi = lax.broadcasted_iota(I32, (rows, PAGE_SIZE), 0) % tq
        kj = lax.broadcasted_iota(I32, (rows, PAGE_SIZE), 1)
        s = lax.dot_general(qbd_b, kn.astype(MXU_DTYPE), _nt_dims(), preferred_element_type=F32)
        s = jnp.where(kj <= qi, s + bias_rows(qi - kj), NEG)
        m_o = jnp.max(s, axis=1, keepdims=True)
        p = jnp.exp(s - m_o)
        l_o = jnp.sum(p, axis=1, keepdims=True)
        acc_o = jnp.dot(p.astype(MXU_DTYPE), vn.astype(MXU_DTYPE), preferred_element_type=F32)
        m_blk = m_ref[...]
        m_all = jnp.maximum(m_o, jnp.max(jnp.where(sel, m_blk, NEG), axis=1, keepdims=True))
        w_blk = jnp.where(sel, jnp.exp(m_blk - m_all), 0.0)
        w_o = jnp.exp(m_o - m_all)
        l_all = w_o * l_o + jnp.sum(w_blk * l_ref[...], axis=1, keepdims=True)
        acc_all = w_o * acc_o
        for n in range(nb_past):
            acc_all = acc_all + w_blk[:, n:n + 1] * acc_ref[n]
        out_bd = jnp.where(head_mask, acc_all / l_all, 0.0)
        out = out_bd[0:tq]
        for h in range(1, N_HEADS):
            out = out + out_bd[h * tq:(h + 1) * tq]
        o_ref[...] = out


def _moba_sample(page_rows, q, k_new, v_new, tab_rows, cache_k, cache_v, *, nseq, tq, nb_past):
    assert nb_past <= LANES and N_HEADS * tq <= LANES
    cb = math.gcd(SAMPLE_CHUNK_BLOCKS, nb_past)
    npg = cb * PAGES_PER_BLOCK
    n_chunks = nb_past // cb
    ppseq = nb_past * PAGES_PER_BLOCK
    rows = N_HEADS * tq
    width = q.shape[1]
    tok = lambda b, c, pt: (b, 0)

    def page_spec(g):
        return pl.BlockSpec((1, width, PAGE_SIZE), lambda b, c, pt: (pt[b * ppseq + c * npg + g], 0, 0))

    grid_spec = pltpu.PrefetchScalarGridSpec(
        num_scalar_prefetch=1,
        grid=(nseq, n_chunks),
        in_specs=[pl.BlockSpec((tq, width), tok)] * 3
        + [pl.BlockSpec(tab_rows.shape, lambda b, c, pt: (0, 0))]
        + [page_spec(g) for g in range(npg)] * 2,
        out_specs=pl.BlockSpec((tq, width), tok),
        scratch_shapes=[pltpu.VMEM((width, LANES), F32),
                        pltpu.VMEM((rows, LANES), F32),
                        pltpu.VMEM((rows, LANES), F32),
                        pltpu.VMEM((nb_past, rows, width), F32)],
    )
    return pl.pallas_call(
        functools.partial(_moba_sample_kernel, tq=tq, nb_past=nb_past, cb=cb),
        grid_spec=grid_spec,
        out_shape=jax.ShapeDtypeStruct((nseq * tq, width), F32),
        compiler_params=pltpu.CompilerParams(dimension_semantics=("arbitrary",) * 2,
                                             vmem_limit_bytes=VMEM_LIMIT),
        name="moba_sample",
    )(page_rows, q, k_new, v_new, tab_rows, *([cache_k] * npg), *([cache_v] * npg))


def _mix_kernel(*refs, tm, seq_len, attn_transposed, has_state, alpha):
    if has_state:
        (x_ref, attn_ref, u_ref, halo_ref, st0_ref, st1_ref, cb_ref, ga_ref, gb_ref, cw_ref,
         wao_ref, wco_ref, wo_ref, g_ref, b_ref, x1_ref, ubuf) = refs
    else:
        (x_ref, attn_ref, u_ref, halo_ref, cb_ref, ga_ref, gb_ref, cw_ref,
         wao_ref, wco_ref, wo_ref, g_ref, b_ref, x1_ref, ubuf) = refs
    i = pl.program_id(0)
    if attn_transposed:
        attn = attn_ref[...].astype(F32).T.astype(MXU_DTYPE)
    else:
        attn = attn_ref[...].astype(MXU_DTYPE)
    y_att = jnp.dot(attn, wao_ref[...], preferred_element_type=F32)
    u = u_ref[...]
    ubuf[0:SUBLANES, :] = halo_ref[...]
    ubuf[SUBLANES:SUBLANES + tm, :] = u
    p1 = ubuf[SUBLANES - 1:SUBLANES - 1 + tm, :]
    p2 = ubuf[SUBLANES - 2:SUBLANES - 2 + tm, :]
    row = lax.broadcasted_iota(I32, (tm, 1), 0)
    if seq_len % tm == 0:
        pos = (i * tm) % seq_len + row
    else:
        pos = row % seq_len
    s0 = st0_ref[...] if has_state else 0.0
    s1 = st1_ref[...] if has_state else 0.0
    prev1 = jnp.where(pos >= 1, p1, s1)
    prev2 = jnp.where(pos >= 2, p2, jnp.where(pos == 1, s1, s0))
    conv = cw_ref[0:1, :] * prev2 + cw_ref[1:2, :] * prev1 + cw_ref[2:3, :] * u
    y_conv = jnp.dot((cb_ref[...].astype(F32) * conv).astype(MXU_DTYPE), wco_ref[...],
                     preferred_element_type=F32)
    merged = ga_ref[...].astype(F32) * y_att + gb_ref[...].astype(F32) * y_conv
    z = alpha * x_ref[...] + jnp.dot(merged.astype(MXU_DTYPE), wo_ref[...], preferred_element_type=F32)
    x1 = _layer_norm(z, g_ref[...], b_ref[...])
    x1_ref[...] = x1


def _mix(x2d, attn, u, state, cb, ga, gb, conv_w, wao, wco, wo, g, b, *, tm, seq_len, attn_transposed, alpha):
    t, d = x2d.shape
    cw = u.shape[1]
    assert seq_len % tm == 0 or tm % seq_len == 0
    nt = t // tm
    row = lambda i: (i, 0)
    full = lambda a: pl.BlockSpec(a.shape, lambda i: (0,) * a.ndim)
    has_state = state is not None
    hb = tm // SUBLANES
    attn_spec = (pl.BlockSpec((ATT_W, tm), lambda i: (0, i)) if attn_transposed
                 else pl.BlockSpec((tm, ATT_W), row))
    args = [x2d, attn, u, u]
    in_specs = [pl.BlockSpec((tm, d), row), attn_spec, pl.BlockSpec((tm, cw), row),
                pl.BlockSpec((SUBLANES, cw), lambda i: (jnp.maximum(i * hb - 1, 0), 0))]
    if has_state:
        args += list(state)
        in_specs += [pl.BlockSpec((tm, cw), row)] * 2
    args += [cb, ga, gb, conv_w, wao, wco, wo, g, b]
    in_specs += [pl.BlockSpec((tm, cw), row), pl.BlockSpec((tm, d), row), pl.BlockSpec((tm, d), row),
                 full(conv_w), full(wao), full(wco), full(wo), full(g), full(b)]
    return pl.pallas_call(
        functools.partial(_mix_kernel, tm=tm, seq_len=seq_len, attn_transposed=attn_transposed,
                          has_state=has_state, alpha=alpha),
        grid=(nt,),
        in_specs=in_specs,
        out_specs=pl.BlockSpec((tm, d), row),
        out_shape=jax.ShapeDtypeStruct((t, d), F32),
        scratch_shapes=[pltpu.VMEM((SUBLANES + tm, cw), F32)],
        compiler_params=pltpu.CompilerParams(dimension_semantics=("arbitrary",),
                                             vmem_limit_bytes=VMEM_LIMIT),
        name="mix",
    )(*args)


def _router_kernel(xp_ref, xs_ref, wT_ref, b_ref, idx_ref, gate_ref, rank_ref, cnt_ref, seen_ref, *, ntp):
    i = pl.program_id(0)
    ne, tm = wT_ref.shape[0], xp_ref.shape[0]
    x = jnp.where(i < ntp, xp_ref[...], xs_ref[...])

    @pl.when(i == 0)
    def _():
        seen_ref[...] = jnp.zeros(seen_ref.shape, F32)

    def halves(a):
        hi = a.astype(MXU_DTYPE)
        return hi, (a - hi.astype(F32)).astype(MXU_DTYPE)

    def nt(a, b):
        return lax.dot_general(a, b, _nt_dims(), preferred_element_type=F32)

    (w_hi, w_lo), (x_hi, x_lo) = halves(wT_ref[...]), halves(x)
    logits = nt(w_hi, x_hi) + nt(w_hi, x_lo) + nt(w_lo, x_hi) + b_ref[...]
    eid = lax.broadcasted_iota(I32, (ne, tm), 0)
    cnt = jnp.zeros((ne, tm), I32)
    for e in range(ne):
        row = logits[e:e + 1, :]
        beats = (row > logits) | ((row == logits) & (e < eid))
        cnt = cnt + jnp.where(beats, 1, 0)
    sel = cnt < TOP_K
    lmax = jnp.max(logits, axis=0, keepdims=True)
    ex = jnp.where(sel, jnp.exp(logits - lmax), 0.0)
    gates = ex / jnp.sum(ex, axis=0, keepdims=True)
    self_ = jnp.where(sel, 1.0, 0.0)
    tt = lax.broadcasted_iota(I32, (tm, tm), 0) < lax.broadcasted_iota(I32, (tm, tm), 1)
    earlier = jnp.where(tt, 1.0, 0.0).astype(MXU_DTYPE)
    seen = seen_ref[:, 0:1]
    rank = jnp.dot(self_.astype(MXU_DTYPE), earlier, preferred_element_type=F32) + seen
    seen = seen + jnp.sum(self_, axis=1, keepdims=True)
    seen_ref[...] = jnp.broadcast_to(seen, seen_ref.shape)
    cnt_ref[...] = jnp.broadcast_to(seen, cnt_ref.shape)
    ee = lax.broadcasted_iota(I32, (ne, ne), 1) < lax.broadcasted_iota(I32, (ne, ne), 0)
    below = jnp.dot(jnp.where(ee, 1.0, 0.0).astype(MXU_DTYPE), self_.astype(MXU_DTYPE),
                    preferred_element_type=F32)
    eidf = eid.astype(F32)
    for kk in range(TOP_K):
        hit = sel & (below == float(kk))
        idx_ref[kk:kk + 1, :] = jnp.sum(jnp.where(hit, eidf, 0.0), axis=0, keepdims=True).astype(I32)
        gate_ref[kk:kk + 1, :] = jnp.sum(jnp.where(hit, gates, 0.0), axis=0, keepdims=True)
        rank_ref[kk:kk + 1, :] = jnp.sum(jnp.where(hit, rank, 0.0), axis=0, keepdims=True).astype(I32)


def _route(x1p, x1s, router_wT, router_b_col, *, tm):
    d = x1p.shape[1]
    ntp, nts = x1p.shape[0] // tm, x1s.shape[0] // tm
    t = (ntp + nts) * tm
    ne = router_wT.shape[0]
    col = lambda i: (0, i)
    return pl.pallas_call(
        functools.partial(_router_kernel, ntp=ntp),
        grid=(ntp + nts,),
        in_specs=[pl.BlockSpec((tm, d), lambda i: (jnp.minimum(i, ntp - 1), 0)),
                  pl.BlockSpec((tm, d), lambda i: (jnp.maximum(i - ntp, 0), 0)),
                  pl.BlockSpec((ne, d), lambda i: (0, 0)),
                  pl.BlockSpec((ne, 1), lambda i: (0, 0))],
        out_specs=[pl.BlockSpec((TOP_K, tm), col), pl.BlockSpec((TOP_K, tm), col),
                   pl.BlockSpec((TOP_K, tm), col), pl.BlockSpec((ne, LANES), lambda i: (0, 0))],
        out_shape=[jax.ShapeDtypeStruct((TOP_K, t), I32), jax.ShapeDtypeStruct((TOP_K, t), F32),
                   jax.ShapeDtypeStruct((TOP_K, t), I32), jax.ShapeDtypeStruct((ne, LANES), F32)],
        scratch_shapes=[pltpu.VMEM((ne, LANES), F32)],
        compiler_params=pltpu.CompilerParams(dimension_semantics=("arbitrary",),
                                             vmem_limit_bytes=VMEM_LIMIT),
        name="router",
    )(x1p, x1s, router_wT, router_b_col)


def _dispatch_kernel(dest_ref, pend_ref, xp_ref, xs_ref, xb_ref, zeros_ref, sem, zsem, *, t, tm, tile, ne, ntp):
    i = pl.program_id(0)

    def row_copy(x_ref, r, dst):
        return pltpu.make_async_copy(x_ref.at[pl.ds(r, 1)], xb_ref.at[pl.ds(dst, 1)], sem)

    def clear_copy(e):
        start = pl.multiple_of(pend_ref[e] - tile, tile)
        return pltpu.make_async_copy(zeros_ref, xb_ref.at[pl.ds(start, tile)], zsem)

    @pl.when(i == 0)
    def _():
        zeros_ref[...] = jnp.zeros(zeros_ref.shape, F32)
        for phase in ("start", "wait"):
            for e in range(ne):
                has_rows = pend_ref[e] > (pend_ref[e - 1] if e else 0)

                @pl.when(has_rows)
                def _():
                    if phase == "start":
                        clear_copy(e).start()
                    else:
                        clear_copy(e).wait()

        def tail_copy(j):
            return pltpu.make_async_copy(zeros_ref, xb_ref.at[pl.ds(pl.multiple_of(j * tile, tile), tile)], zsem)

        first_unused = pend_ref[ne - 1] // tile
        n_tiles = xb_ref.shape[0] // tile
        lax.fori_loop(first_unused, n_tiles, lambda j, c: (tail_copy(j).start(), c)[1], 0)
        lax.fori_loop(first_unused, n_tiles, lambda j, c: (tail_copy(j).wait(), c)[1], 0)

    def issue_from(x_ref):
        def issue(r, carry):
            for kk in range(TOP_K):
                row_copy(x_ref, r, dest_ref[kk * t + i * tm + r]).start(priority=kk % 2)
            return carry

        lax.fori_loop(0, tm, issue, 0, unroll=DISPATCH_UNROLL)

    pl.when(i < ntp)(lambda: issue_from(xp_ref))
    pl.when(i >= ntp)(lambda: issue_from(xs_ref))
    for kk in range(TOP_K):
        pltpu.make_async_copy(xp_ref, xb_ref.at[pl.ds(0, tm)], sem).wait()


def _dispatch(dest_flat, pad_end, x1p, x1s, *, rows, tm, tile):
    d = x1p.shape[1]
    ntp, nts = x1p.shape[0] // tm, x1s.shape[0] // tm
    t = (ntp + nts) * tm
    ne = pad_end.shape[0]
    grid_spec = pltpu.PrefetchScalarGridSpec(
        num_scalar_prefetch=2,
        grid=(ntp + nts,),
        in_specs=[pl.BlockSpec((tm, d), lambda i, dest, pend: (jnp.minimum(i, ntp - 1), 0)),
                  pl.BlockSpec((tm, d), lambda i, dest, pend: (jnp.maximum(i - ntp, 0), 0))],
        out_specs=pl.BlockSpec(memory_space=pl.ANY),
        scratch_shapes=[pltpu.VMEM((tile, d), F32), pltpu.SemaphoreType.DMA, pltpu.SemaphoreType.DMA],
    )
    return pl.pallas_call(
        functools.partial(_dispatch_kernel, t=t, tm=tm, tile=tile, ne=ne, ntp=ntp),
        grid_spec=grid_spec,
        out_shape=jax.ShapeDtypeStruct((rows, d), F32),
        compiler_params=pltpu.CompilerParams(dimension_semantics=("arbitrary",),
                                             vmem_limit_bytes=VMEM_LIMIT),
        name="dispatch",
    )(dest_flat, pad_end, x1p, x1s)


def _expert_kernel(te_ref, nu_ref, x_ref, wgu_ref, bg_ref, bu_ref, wd_ref, bd_ref, y_ref, wg_s, wu_s, wd_s):
    i = pl.program_id(0)
    new_expert = jnp.logical_or(i == 0, te_ref[i] != te_ref[jnp.maximum(i - 1, 0)])

    @pl.when(jnp.logical_and(new_expert, i < nu_ref[0]))
    def _():
        half = LANES
        c = lax.broadcasted_iota(I32, (2 * half, 2 * half), 0)
        o = lax.broadcasted_iota(I32, (2 * half, 2 * half), 1)
        src = jnp.where(o < half, 2 * o, 2 * (o - half) + 1)
        perm = jnp.where(c == src, 1.0, 0.0).astype(MXU_DTYPE)
        for j in range(wgu_ref.shape[2] // (2 * half)):
            wb = wgu_ref[0, :, 2 * half * j:2 * half * (j + 1)].astype(MXU_DTYPE)
            y = jnp.dot(wb, perm, preferred_element_type=F32).astype(MXU_DTYPE)
            wg_s[:, half * j:half * (j + 1)] = y[:, :half]
            wu_s[:, half * j:half * (j + 1)] = y[:, half:]
        wd_s[...] = wd_ref[0].astype(MXU_DTYPE)

    @pl.when(i < nu_ref[0])
    def _():
        x = x_ref[...].astype(MXU_DTYPE)
        hg = jnp.dot(x, wg_s[...], preferred_element_type=F32) + bg_ref[0]
        hu = jnp.dot(x, wu_s[...], preferred_element_type=F32) + bu_ref[0]
        g = jnp.minimum(hg, SWIGLU_LIMIT)
        u = jnp.clip(hu, -SWIGLU_LIMIT, SWIGLU_LIMIT)
        a = g * _sigmoid(SWIGLU_ALPHA * g) * (u + 1.0)
        y_ref[...] = jnp.dot(a.astype(MXU_DTYPE), wd_s[...], preferred_element_type=F32) + bd_ref[0]

    @pl.when(i >= nu_ref[0])
    def _():
        y_ref[...] = jnp.zeros(y_ref.shape, F32)


def _experts(tile_e, n_used, xb, w_gate_up, bg, bu, w_down, bd, *, tm):
    rows, d = xb.shape
    dff = w_down.shape[1]
    nt = rows // tm
    ew = lambda i, te, nu: (te[i], 0, 0)
    grid_spec = pltpu.PrefetchScalarGridSpec(
        num_scalar_prefetch=2,
        grid=(nt,),
        in_specs=[pl.BlockSpec((tm, d), lambda i, te, nu: (jnp.minimum(i, nu[0] - 1), 0)),
                  pl.BlockSpec((1, d, 2 * dff), ew),
                  pl.BlockSpec((1, 1, dff), ew), pl.BlockSpec((1, 1, dff), ew),
                  pl.BlockSpec((1, dff, d), ew), pl.BlockSpec((1, 1, d), ew)],
        out_specs=pl.BlockSpec((tm, d), lambda i, te, nu: (i, 0)),
        scratch_shapes=[pltpu.VMEM((d, dff), MXU_DTYPE), pltpu.VMEM((d, dff), MXU_DTYPE),
                        pltpu.VMEM((dff, d), MXU_DTYPE)],
    )
    return pl.pallas_call(
        _expert_kernel,
        grid_spec=grid_spec,
        out_shape=jax.ShapeDtypeStruct((rows, d), F32),
        compiler_params=pltpu.CompilerParams(dimension_semantics=("arbitrary",),
                                             vmem_limit_bytes=VMEM_LIMIT),
        name="experts",
    )(tile_e, n_used, xb, w_gate_up, bg, bu, w_down, bd)


def _norm2_kernel(x1_ref, gate_ref, *refs, alpha):
    y_refs, (g_ref, b_ref, o_ref) = refs[:TOP_K], refs[TOP_K:]
    gate = gate_ref[...]
    ffn = gate[:, 0:1] * y_refs[0][...]
    for kk in range(1, TOP_K):
        ffn = ffn + gate[:, kk:kk + 1] * y_refs[kk][...]
    o_ref[...] = _layer_norm(alpha * x1_ref[...] + ffn, g_ref[...], b_ref[...])


def _norm2(x1, gate_t, y_pairs, g, b, *, row0, tm, alpha):
    nrows, d = x1.shape
    off = row0 // tm
    src = lambda i: (i + off, 0)
    return pl.pallas_call(
        functools.partial(_norm2_kernel, alpha=alpha),
        grid=(nrows // tm,),
        in_specs=[pl.BlockSpec((tm, d), lambda i: (i, 0)), pl.BlockSpec((tm, TOP_K), src)]
        + [pl.BlockSpec((tm, d), src)] * TOP_K
        + [pl.BlockSpec((1, d), lambda i: (0, 0)), pl.BlockSpec((1, d), lambda i: (0, 0))],
        out_specs=pl.BlockSpec((tm, d), lambda i: (i, 0)),
        out_shape=jax.ShapeDtypeStruct((nrows, d), F32),
        compiler_params=pltpu.CompilerParams(dimension_semantics=("arbitrary",),
                                             vmem_limit_bytes=VMEM_LIMIT),
        name="norm2",
    )(x1, gate_t, *y_pairs, g, b)


def _moe(x1p, x1s, router_w, router_b, w_gate_up, b_gate_up, w_down, b_down):
    t = x1p.shape[0] + x1s.shape[0]
    ne = router_w.shape[1]
    idx, gate, rank, counts = _route(x1p, x1s, router_w.T, router_b.reshape(ne, 1), tm=TOKEN_TILE)
    tm = EXPERT_TILE
    counts = counts[:, 0].astype(I32)
    padded = (counts + tm - 1) // tm * tm
    pad_end = jnp.cumsum(padded)
    pad_start = pad_end - padded
    n_tiles = (t * TOP_K + ne * (tm - 1)) // tm
    rows = n_tiles * tm
    eids = jnp.arange(ne, dtype=I32)
    start_of = jnp.sum(jnp.where(idx[:, :, None] == eids, pad_start, 0), axis=-1)
    dest = start_of + rank
    tile_row = jnp.arange(n_tiles, dtype=I32) * tm
    tile_e = jnp.minimum(jnp.sum((pad_end[None, :] <= tile_row[:, None]).astype(I32), axis=1), ne - 1)
    n_used = (pad_end[-1:] // tm).astype(I32)
    xb = _dispatch(dest.reshape(-1), pad_end.astype(I32), x1p, x1s, rows=rows, tm=TOKEN_TILE, tile=tm)
    bg = b_gate_up[:, None, 0::2]
    bu = b_gate_up[:, None, 1::2]
    yb = _experts(tile_e, n_used, xb, w_gate_up, bg, bu, w_down, b_down[:, None, :], tm=tm)
    return gate.T, [yb[dest[kk]] for kk in range(TOP_K)]


def kernel(x_prompt, x_sample, cache_k, cache_v, state_conv, page_table, rel_bias, w_in, conv_w, w_att_o,
           w_conv_o, w_o, ln1_g, ln1_b, router_w, router_b, w_gate_up, b_gate_up, w_down, b_down, ln2_g, ln2_b):
    depth = w_in.shape[0]
    batch, seq, d = x_prompt.shape
    nseq, tq, _ = x_sample.shape
    n_pool = cache_k.shape[1]
    ppseq = page_table.shape[1]
    past = ppseq * PAGE_SIZE
    assert seq % MOBA_BLOCK == 0 and past % MOBA_BLOCK == 0 and tq % SUBLANES == 0 and tq <= PAGE_SIZE
    nb_past = past // MOBA_BLOCK
    tp, ts = batch * seq, nseq * tq
    alpha = (2 * depth) ** 0.25
    cw = conv_w.shape[2]
    tab_rows = jnp.repeat(rel_bias.T, tq, axis=0)
    ck = cache_k.transpose(0, 1, 3, 4, 2).reshape(depth * n_pool, ATT_W, PAGE_SIZE)
    cv = cache_v.transpose(0, 1, 3, 4, 2).reshape(depth * n_pool, ATT_W, PAGE_SIZE)
    hp = x_prompt.reshape(tp, d)
    hs = x_sample.reshape(ts, d)
    outs = [[] for _ in range(6)]
    for l in range(depth):
        w_in_b = w_in[l].astype(MXU_DTYPE)
        wao, wco, wo = (w_att_o[l].astype(MXU_DTYPE), w_conv_o[l].astype(MXU_DTYPE), w_o[l].astype(MXU_DTYPE))
        g1, b1, g2, b2 = ln1_g[l][None], ln1_b[l][None], ln2_g[l][None], ln2_b[l][None]
        qT, khm, vT, kp, vp, km, up, cbp, gap, gbp = _project(hp, w_in_b, tm=PROJ_TILE, head_major=True, seq=seq)
        nbt = PROJ_TILE // MOBA_BLOCK
        kmean = km[:, :nbt].reshape(batch, seq // MOBA_BLOCK, N_HEADS, HEAD_DIM).transpose(0, 2, 1, 3)
        kmean = kmean.reshape(batch * N_HEADS, seq // MOBA_BLOCK, HEAD_DIM)
        attn_p = _moba_prompt(rel_bias, qT, khm, vT, kmean, batch=batch, seq=seq)
        x1p = _mix(hp, attn_p, up, None, cbp, gap, gbp, conv_w[l], wao, wco, wo, g1, b1,
                         tm=PROJ_TILE, seq_len=seq, attn_transposed=True, alpha=alpha)
        qs, ks, vs, us, cbs, gas, gbs = _project(hs, w_in_b, tm=ts, head_major=False, wq=w_in[l][:, :ATT_W])
        pages = (page_table + l * n_pool).reshape(-1).astype(I32)
        attn_s = _moba_sample(pages, qs, ks, vs, tab_rows, ck, cv, nseq=nseq, tq=tq, nb_past=nb_past)
        state = (jnp.repeat(state_conv[l][:, 0], tq, axis=0), jnp.repeat(state_conv[l][:, 1], tq, axis=0))
        x1s = _mix(hs, attn_s, us, state, cbs, gas, gbs, conv_w[l], wao, wco, wo, g1, b1,
                         tm=ts, seq_len=tq, attn_transposed=False, alpha=alpha)
        gate_t, y_pairs = _moe(x1p, x1s, router_w[l], router_b[l], w_gate_up[l], b_gate_up[l], w_down[l], b_down[l])
        hp = _norm2(x1p, gate_t, y_pairs, g2, b2, row0=0, tm=TOKEN_TILE, alpha=alpha)
        hs = _norm2(x1s, gate_t, y_pairs, g2, b2, row0=tp, tm=TOKEN_TILE, alpha=alpha)
        outs[0].append(kp.reshape(batch, N_HEADS, HEAD_DIM, seq).transpose(0, 3, 1, 2))
        outs[1].append(vp.reshape(batch, N_HEADS, HEAD_DIM, seq).transpose(0, 3, 1, 2))
        outs[2].append(up.reshape(batch, seq, cw)[:, seq - (CONV_K - 1):])
        outs[3].append(ks.reshape(nseq, tq, N_HEADS, HEAD_DIM))
        outs[4].append(vs.reshape(nseq, tq, N_HEADS, HEAD_DIM))
        outs[5].append(us.reshape(nseq, tq, cw)[:, tq - (CONV_K - 1):])
    stacked = [jnp.stack(o) for o in outs]
    return (hp.reshape(batch, seq, d), hs.reshape(nseq, tq, d), *stacked)
```

```python
import functools
import math

import jax
import jax.numpy as jnp
from jax import lax
from jax.experimental import pallas as pl
from jax.experimental.pallas import tpu as pltpu

F32 = jnp.float32
I32 = jnp.int32
MXU_DTYPE = jnp.bfloat16
HIGHEST = lax.Precision.HIGHEST

N_HEADS = 8
HEAD_DIM = 64
ATT_W = N_HEADS * HEAD_DIM
MOBA_BLOCK = 256
MOBA_TOPK = 3
PAGE_SIZE = 128
PAGES_PER_BLOCK = MOBA_BLOCK // PAGE_SIZE
CONV_K = 3
N_EXPERTS = 32
TOP_K = 4
SWIGLU_LIMIT = 7.0
SWIGLU_ALPHA = 1.702
NUM_BUCKETS = 32
NUM_EXACT = NUM_BUCKETS // 2
MAX_DISTANCE = 128
LN_EPS = 1e-5
SCORE_SCALE = HEAD_DIM ** -0.5
NEG = -1e30

SUBLANES = 8
LANES = 128
PACKED_ROWS = 16
V_ROWS = HEAD_DIM + PACKED_ROWS

PROJ_TILE = 512
TOKEN_TILE = 256
EXPERT_TILE = 512
DISPATCH_UNROLL = 8
PROMPT_ROUND_UNROLL = 4
SAMPLE_CHUNK_BLOCKS = 16
VMEM_LIMIT = 56 * 1024 * 1024

assert MOBA_BLOCK >= MAX_DISTANCE


def _nt_dims():
    return (((1,), (1,)), ((), ()))


def _sigmoid(x):
    return 1.0 / (1.0 + jnp.exp(-x))


def _bucket(dist):
    n = jnp.maximum(dist, 0)
    nf = jnp.maximum(n, 1).astype(F32)
    large = NUM_EXACT + (jnp.log(nf / NUM_EXACT) / math.log(MAX_DISTANCE / NUM_EXACT)
                         * (NUM_BUCKETS - NUM_EXACT)).astype(I32)
    large = jnp.minimum(large, NUM_BUCKETS - 1)
    return jnp.where(n < NUM_EXACT, n, large)


def _layer_norm(z, g, b):
    mu = jnp.mean(z, axis=-1, keepdims=True)
    zc = z - mu
    var = jnp.mean(zc * zc, axis=-1, keepdims=True)
    return zc * lax.rsqrt(var + LN_EPS) * g + b


def _proj_kernel(x_ref, w_ref, *refs, tm, head_major, blocks_per_seq):
    xb = x_ref[...].astype(MXU_DTYPE)

    def mm(c0, width):
        return jnp.dot(xb, w_ref[:, c0:c0 + width], preferred_element_type=F32)

    if head_major:
        qT_ref, khm_ref, vT_ref, k_ref, v_ref, km_ref, u_ref, cb_ref, ga_ref, gb_ref = refs
        q = mm(0, ATT_W) * SCORE_SCALE
    else:
        wq_ref, q_ref, k_ref, v_ref, u_ref, cb_ref, ga_ref, gb_ref = refs
        q = jnp.dot(x_ref[...], wq_ref[...], preferred_element_type=F32, precision=HIGHEST) * SCORE_SCALE
    d_model = x_ref.shape[1]
    k = mm(ATT_W, ATT_W)
    v = mm(2 * ATT_W, ATT_W)
    if head_major:
        qT_ref[...] = q.T
        k_ref[0] = k.T
        vt = v.T
        v_ref[0] = vt
        km_ref[...] = jnp.zeros(km_ref.shape, F32)
        lane = lax.broadcasted_iota(I32, (MOBA_BLOCK, 2 * HEAD_DIM), 1)
        ones_rows = jnp.ones((V_ROWS - HEAD_DIM, MOBA_BLOCK), MXU_DTYPE)
        for r in range(tm // MOBA_BLOCK):
            kr = k[r * MOBA_BLOCK:(r + 1) * MOBA_BLOCK]
            km_ref[0, r:r + 1, :] = jnp.sum(kr, axis=0, keepdims=True) * (1.0 / MOBA_BLOCK)
            n_blk = (pl.program_id(0) * (tm // MOBA_BLOCK) + r) % blocks_per_seq
            tag = jnp.where(lane - HEAD_DIM == n_blk, 1.0, 0.0)
            vtr = vt[:, r * MOBA_BLOCK:(r + 1) * MOBA_BLOCK]
            for h in range(N_HEADS):
                pair = kr[:, (h // 2) * 2 * HEAD_DIM:(h // 2 + 1) * 2 * HEAD_DIM]
                if h % 2:
                    pair = pltpu.roll(pair, HEAD_DIM, axis=1)
                khm_ref[h, r] = jnp.where(lane < HEAD_DIM, pair, tag).astype(MXU_DTYPE)
                vT_ref[r, h * V_ROWS:h * V_ROWS + HEAD_DIM] = vtr[h * HEAD_DIM:(h + 1) * HEAD_DIM].astype(MXU_DTYPE)
                vT_ref[r, h * V_ROWS + HEAD_DIM:(h + 1) * V_ROWS] = ones_rows
    else:
        q_ref[...] = q
        k_ref[...] = k
        v_ref[...] = v
    c0 = 3 * ATT_W
    cw = u_ref.shape[1]
    cb_ref[...] = mm(c0, cw).astype(cb_ref.dtype)
    u_ref[...] = mm(c0 + cw, cw) * mm(c0 + 2 * cw, cw)
    ga_ref[...] = _sigmoid(mm(c0 + 3 * cw, d_model)).astype(ga_ref.dtype)
    gb_ref[...] = _sigmoid(mm(c0 + 3 * cw + d_model, d_model)).astype(gb_ref.dtype)


def _project(x2d, w_in_b, *, tm, head_major, seq=None, wq=None):
    t, d = x2d.shape
    cw = (w_in_b.shape[1] - 3 * ATT_W - 2 * d) // 3
    nt = t // tm
    row = lambda i: (i, 0)
    f32s = lambda shape: jax.ShapeDtypeStruct(shape, F32)
    mxs = lambda shape: jax.ShapeDtypeStruct(shape, MXU_DTYPE)
    tail_shapes = [f32s((t, cw)), mxs((t, cw)), mxs((t, d)), mxs((t, d))]
    tail_specs = [pl.BlockSpec((tm, cw), row), pl.BlockSpec((tm, cw), row),
                  pl.BlockSpec((tm, d), row), pl.BlockSpec((tm, d), row)]
    if head_major:
        nbt = tm // MOBA_BLOCK
        tps = seq // tm
        seq_t = pl.BlockSpec((1, ATT_W, tm), lambda i: (i // tps, 0, i % tps))
        out_shape = [f32s((ATT_W, t)), mxs((N_HEADS, t // MOBA_BLOCK, MOBA_BLOCK, 2 * HEAD_DIM)),
                     mxs((t // MOBA_BLOCK, N_HEADS * V_ROWS, MOBA_BLOCK)), f32s((t // seq, ATT_W, seq)),
                     f32s((t // seq, ATT_W, seq)), f32s((nt, SUBLANES, ATT_W))] + tail_shapes
        out_specs = [pl.BlockSpec((ATT_W, tm), lambda i: (0, i)),
                     pl.BlockSpec((N_HEADS, nbt, MOBA_BLOCK, 2 * HEAD_DIM), lambda i: (0, i, 0, 0)),
                     pl.BlockSpec((nbt, N_HEADS * V_ROWS, MOBA_BLOCK), lambda i: (i, 0, 0)),
                     seq_t, seq_t,
                     pl.BlockSpec((1, SUBLANES, ATT_W), lambda i: (i, 0, 0))] + tail_specs
        args, extra_specs = (x2d, w_in_b), []
    else:
        out_shape = [f32s((t, ATT_W))] * 3 + tail_shapes
        out_specs = [pl.BlockSpec((tm, ATT_W), row)] * 3 + tail_specs
        args, extra_specs = (x2d, w_in_b, wq), [pl.BlockSpec(wq.shape, lambda i: (0, 0))]
    return pl.pallas_call(
        functools.partial(_proj_kernel, tm=tm, head_major=head_major,
                          blocks_per_seq=seq // MOBA_BLOCK if head_major else None),
        grid=(nt,),
        in_specs=[pl.BlockSpec((tm, d), row), pl.BlockSpec(w_in_b.shape, lambda i: (0, 0))] + extra_specs,
        out_specs=out_specs,
        out_shape=out_shape,
        compiler_params=pltpu.CompilerParams(dimension_semantics=("arbitrary",),
                                             vmem_limit_bytes=VMEM_LIMIT),
        name="proj",
    )(*args)


def _moba_prompt_kernel(tab_ref, qT_ref, k_ref, vT_ref, km_ref, o_ref, bias_ref, qb_ref, *state):
    h = pl.program_id(0)
    first_sequence = pl.program_id(1) == 0
    nb = km_ref.shape[1]
    blk = MOBA_BLOCK
    n_streams = nb // 2
    m_refs, acc_refs = state[:n_streams], state[n_streams:]

    @pl.when(first_sequence)
    def _():
        kk = lax.broadcasted_iota(I32, (blk, blk), 0)
        qq = lax.broadcasted_iota(I32, (blk, blk), 1)
        for age in range(2):
            dist = qq - kk + age * blk
            bucket = _bucket(dist)
            tile = jnp.zeros((blk, blk), F32)
            for b in range(NUM_BUCKETS):
                tile = jnp.where(bucket == b, tab_ref[b, h], tile)
            bias_ref[age] = jnp.where(dist >= 0, tile, NEG)
        bias_ref[2] = jnp.full((blk, blk), tab_ref[NUM_BUCKETS - 1, h], F32)

    km = km_ref[0]
    nid = lax.broadcasted_iota(I32, (nb, blk), 0)
    spare = jnp.zeros((HEAD_DIM - nb, blk), F32)
    for i in range(nb):
        qT = qT_ref[:, i * blk:(i + 1) * blk]
        gT = jnp.dot(km, qT, preferred_element_type=F32, precision=HIGHEST)
        cnt = jnp.zeros((nb, blk), I32)
        for m in range(i):
            row = gT[m:m + 1, :]
            beats = (row > gT) | ((row == gT) & (m < nid))
            cnt = cnt + jnp.where(beats, 1, 0)
        keep = ((cnt < MOBA_TOPK) & (nid < i)) | (nid == i)
        qb_ref[i] = jnp.concatenate([qT, jnp.where(keep, 0.0, NEG), spare], axis=0).astype(MXU_DTYPE)

    for a in range(n_streams):
        m_refs[a][...] = jnp.full(m_refs[a].shape, NEG, F32)
        acc_refs[a][...] = jnp.zeros(acc_refs[a].shape, F32)

    def sweep_round(t, carry):
        staged = []
        for a in range(n_streams):
            first = t <= a
            slot = jnp.where(first, 0, 1)
            i = jnp.where(first, a, nb - 1 - a)
            n = jnp.where(first, a - t, nb - t)
            age = jnp.minimum(i - n, 2)
            s = jnp.dot(k_ref[0, n], qb_ref[i], preferred_element_type=F32) + bias_ref[age]
            staged.append((slot, n, s))
        updates = []
        for a, (slot, n, s) in enumerate(staged):
            m_old = m_refs[a][slot]
            m_new = jnp.maximum(m_old, jnp.max(s, axis=0, keepdims=True))
            alpha = jnp.exp(m_old - m_new)
            p = jnp.exp(s - m_new)
            acc_new = alpha * acc_refs[a][slot] + jnp.dot(
                vT_ref[n], p.astype(MXU_DTYPE), preferred_element_type=F32)
            updates.append((slot, m_new, acc_new))
        for a, (slot, m_new, acc_new) in enumerate(updates):
            m_refs[a][slot] = m_new
            acc_refs[a][slot] = acc_new
        return carry

    lax.fori_loop(0, nb + 1, sweep_round, 0, unroll=PROMPT_ROUND_UNROLL)
    for a in range(n_streams):
        for slot, i in ((0, a), (1, nb - 1 - a)):
            acc = acc_refs[a][slot]
            o_ref[:, i * blk:(i + 1) * blk] = (acc[:HEAD_DIM] / acc[HEAD_DIM:HEAD_DIM + 1]).astype(o_ref.dtype)


def _moba_prompt(rel_bias, qT, khm, vT, kmean, *, batch, seq):
    nb = seq // MOBA_BLOCK
    assert nb % 2 == 0 and nb <= HEAD_DIM
    t = batch * seq
    stream = lambda shape: [pltpu.VMEM((2,) + shape, F32)] * (nb // 2)
    return pl.pallas_call(
        _moba_prompt_kernel,
        grid=(N_HEADS, batch),
        in_specs=[
            pl.BlockSpec(memory_space=pltpu.SMEM),
            pl.BlockSpec((HEAD_DIM, seq), lambda h, b: (h, b)),
            pl.BlockSpec((1, nb, MOBA_BLOCK, 2 * HEAD_DIM), lambda h, b: (h, b, 0, 0)),
            pl.BlockSpec((nb, V_ROWS, MOBA_BLOCK), lambda h, b: (b, h, 0)),
            pl.BlockSpec((1, nb, HEAD_DIM), lambda h, b: (b * N_HEADS + h, 0, 0)),
        ],
        out_specs=pl.BlockSpec((HEAD_DIM, seq), lambda h, b: (h, b)),
        out_shape=jax.ShapeDtypeStruct((ATT_W, t), MXU_DTYPE),
        scratch_shapes=[pltpu.VMEM((3, MOBA_BLOCK, MOBA_BLOCK), F32),
                        pltpu.VMEM((nb, 2 * HEAD_DIM, MOBA_BLOCK), MXU_DTYPE)]
        + stream((1, MOBA_BLOCK)) + stream((V_ROWS, MOBA_BLOCK)),
        compiler_params=pltpu.CompilerParams(dimension_semantics=("arbitrary",) * 2,
                                             vmem_limit_bytes=VMEM_LIMIT),
        name="moba_prompt",
    )(rel_bias, qT, khm, vT, kmean)


def _moba_sample_kernel(pt_ref, q_ref, kn_ref, vn_ref, tab_ref, *refs, tq, nb_past, cb):
    del pt_ref
    npg = cb * PAGES_PER_BLOCK
    k_pages = refs[:npg]
    v_pages = refs[npg:2 * npg]
    o_ref, km_ref, m_ref, l_ref, acc_ref = refs[2 * npg:]
    c = pl.program_id(1)
    n_chunks = nb_past // cb
    rows = N_HEADS * tq
    width = q_ref.shape[1]
    r_id = lax.broadcasted_iota(I32, (rows, width), 0)
    lane = lax.broadcasted_iota(I32, (rows, width), 1)
    head_mask = (lane // HEAD_DIM) == (r_id // tq)
    q_rows = jnp.concatenate([q_ref[...]] * N_HEADS, axis=0)
    qbd = jnp.where(head_mask, q_rows, 0.0)
    qbd_b = qbd.astype(MXU_DTYPE)
    tab = tab_ref[...]

    def bias_rows(dist):
        bucket = _bucket(dist)
        out = jnp.zeros(dist.shape, F32)
        for b in range(NUM_BUCKETS):
            out = jnp.where(bucket == b, tab[:, b:b + 1], out)
        return out

    c_far = tab[:, NUM_BUCKETS - 1:NUM_BUCKETS]

    @pl.when(c == 0)
    def _():
        km_ref[...] = jnp.zeros(km_ref.shape, F32)
        m_ref[...] = jnp.full(m_ref.shape, NEG, F32)
        l_ref[...] = jnp.zeros(l_ref.shape, F32)

    km_lane = lax.broadcasted_iota(I32, km_ref.shape, 1)
    st_lane = lax.broadcasted_iota(I32, m_ref.shape, 1)
    qi = lax.broadcasted_iota(I32, (rows, MOBA_BLOCK), 0) % tq
    kj = lax.broadcasted_iota(I32, (rows, MOBA_BLOCK), 1)
    last_bias = lax.cond(c == n_chunks - 1, lambda: bias_rows(MOBA_BLOCK + qi - kj),
                         lambda: jnp.broadcast_to(c_far, (rows, MOBA_BLOCK)))
    scores = []
    km_new = jnp.zeros(km_ref.shape, F32)
    for j in range(cb):
        kts = [k_pages[PAGES_PER_BLOCK * j + g][0] for g in range(PAGES_PER_BLOCK)]
        ksum = kts[0]
        for kt in kts[1:]:
            ksum = ksum + kt
        kmean = jnp.sum(ksum, axis=1, keepdims=True) * (1.0 / MOBA_BLOCK)
        km_new = jnp.where(km_lane == c * cb + j, kmean, km_new)
        scores.append(jnp.concatenate(
            [jnp.dot(qbd_b, kt.astype(MXU_DTYPE), preferred_element_type=F32) for kt in kts], axis=1))
    km_ref[...] = km_ref[...] + km_new
    probs = []
    m_new = jnp.full(m_ref.shape, NEG, F32)
    l_new = jnp.zeros(l_ref.shape, F32)
    for j, s in enumerate(scores):
        s = s + (last_bias if j == cb - 1 else c_far)
        m_n = jnp.max(s, axis=1, keepdims=True)
        p = jnp.exp(s - m_n)
        m_new = jnp.where(st_lane == c * cb + j, m_n, m_new)
        l_new = jnp.where(st_lane == c * cb + j, jnp.sum(p, axis=1, keepdims=True), l_new)
        probs.append(p.astype(MXU_DTYPE))
    m_ref[...] = jnp.maximum(m_ref[...], m_new)
    l_ref[...] = l_ref[...] + l_new
    for j, pb in enumerate(probs):
        acc = None
        for g in range(PAGES_PER_BLOCK):
            vt = v_pages[PAGES_PER_BLOCK * j + g][0]
            part = lax.dot_general(pb[:, g * PAGE_SIZE:(g + 1) * PAGE_SIZE], vt.astype(MXU_DTYPE), _nt_dims(),
                                   preferred_element_type=F32)
            acc = part if acc is None else acc + part
        acc_ref[c * cb + j] = acc

    @pl.when(c == n_chunks - 1)
    def _():
        lanes = LANES
        nbp = -(-nb_past // SUBLANES) * SUBLANES
        q_pad = jnp.concatenate([qbd, jnp.zeros((lanes - rows, width), F32)], axis=0)
        gT = lax.dot_general(km_ref[...].T, q_pad, _nt_dims(), preferred_element_type=F32,
                             precision=HIGHEST)[:nbp]
        nid = lax.broadcasted_iota(I32, gT.shape, 0)
        cnt = jnp.zeros(gT.shape, I32)
        for m in range(nb_past):
            row = gT[m:m + 1, :]
            beats = (row > gT) | ((row == gT) & (m < nid))
            cnt = cnt + jnp.where(beats, 1, 0)
        selT = jnp.where((cnt < MOBA_TOPK) & (nid < nb_past), 1.0, 0.0)
        selT = jnp.concatenate([selT, jnp.zeros((lanes - nbp, lanes), F32)], axis=0)
        sel = selT.T[:rows] > 0.5
        pad = jnp.zeros((PAGE_SIZE - tq, width), F32)
        kn = jnp.concatenate([kn_ref[...], pad], axis=0)
        vn = jnp.concatenate([vn_ref[...], pad], axis=0)
        qi = lax.broadcasted_iota(I32, (rows, PAGE_SIZE), 0) % tq
        kj = lax.broadcasted_iota(I32, (rows, PAGE_SIZE), 1)
        s = lax.dot_general(qbd_b, kn.astype(MXU_DTYPE), _nt_dims(), preferred_element_type=F32)
        s = jnp.where(kj <= qi, s + bias_rows(qi - kj), NEG)
        m_o = jnp.max(s, axis=1, keepdims=True)
        p = jnp.exp(s - m_o)
        l_o = jnp.sum(p, axis=1, keepdims=True)
        acc_o = jnp.dot(p.astype(MXU_DTYPE), vn.astype(MXU_DTYPE), preferred_element_type=F32)
        m_blk = m_ref[...]
        m_all = jnp.maximum(m_o, jnp.max(jnp.where(sel, m_blk, NEG), axis=1, keepdims=True))
        w_blk = jnp.where(sel, jnp.exp(m_blk - m_all), 0.0)
        w_o = jnp.exp(m_o - m_all)
        l_all = w_o * l_o + jnp.sum(w_blk * l_ref[...], axis=1, keepdims=True)
        acc_all = w_o * acc_o
        for n in range(nb_past):
            acc_all = acc_all + w_blk[:, n:n + 1] * acc_ref[n]
        out_bd = jnp.where(head_mask, acc_all / l_all, 0.0)
        out = out_bd[0:tq]
        for h in range(1, N_HEADS):
            out = out + out_bd[h * tq:(h + 1) * tq]
        o_ref[...] = out


def _moba_sample(page_rows, q, k_new, v_new, tab_rows, cache_k, cache_v, *, nseq, tq, nb_past):
    assert nb_past <= LANES and N_HEADS * tq <= LANES
    cb = math.gcd(SAMPLE_CHUNK_BLOCKS, nb_past)
    npg = cb * PAGES_PER_BLOCK
    n_chunks = nb_past // cb
    ppseq = nb_past * PAGES_PER_BLOCK
    rows = N_HEADS * tq
    width = q.shape[1]
    tok = lambda b, c, pt: (b, 0)

    def page_spec(g):
        return pl.BlockSpec((1, width, PAGE_SIZE), lambda b, c, pt: (pt[b * ppseq + c * npg + g], 0, 0))

    grid_spec = pltpu.PrefetchScalarGridSpec(
        num_scalar_prefetch=1,
        grid=(nseq, n_chunks),
        in_specs=[pl.BlockSpec((tq, width), tok)] * 3
        + [pl.BlockSpec(tab_rows.shape, lambda b, c, pt: (0, 0))]
        + [page_spec(g) for g in range(npg)] * 2,
        out_specs=pl.BlockSpec((tq, width), tok),
        scratch_shapes=[pltpu.VMEM((width, LANES), F32),
                        pltpu.VMEM((rows, LANES), F32),
                        pltpu.VMEM((rows, LANES), F32),
                        pltpu.VMEM((nb_past, rows, width), F32)],
    )
    return pl.pallas_call(
        functools.partial(_moba_sample_kernel, tq=tq, nb_past=nb_past, cb=cb),
        grid_spec=grid_spec,
        out_shape=jax.ShapeDtypeStruct((nseq * tq, width), F32),
        compiler_params=pltpu.CompilerParams(dimension_semantics=("arbitrary",) * 2,
                                             vmem_limit_bytes=VMEM_LIMIT),
        name="moba_sample",
    )(page_rows, q, k_new, v_new, tab_rows, *([cache_k] * npg), *([cache_v] * npg))


def _mix_kernel(*refs, tm, seq_len, attn_transposed, has_state, alpha):
    if has_state:
        (x_ref, attn_ref, u_ref, halo_ref, st0_ref, st1_ref, cb_ref, ga_ref, gb_ref, cw_ref,
         wao_ref, wco_ref, wo_ref, g_ref, b_ref, x1_ref, ubuf) = refs
    else:
        (x_ref, attn_ref, u_ref, halo_ref, cb_ref, ga_ref, gb_ref, cw_ref,
         wao_ref, wco_ref, wo_ref, g_ref, b_ref, x1_ref, ubuf) = refs
    i = pl.program_id(0)
    if attn_transposed:
        attn = attn_ref[...].astype(F32).T.astype(MXU_DTYPE)
    else:
        attn = attn_ref[...].astype(MXU_DTYPE)
    y_att = jnp.dot(attn, wao_ref[...], preferred_element_type=F32)
    u = u_ref[...]
    ubuf[0:SUBLANES, :] = halo_ref[...]
    ubuf[SUBLANES:SUBLANES + tm, :] = u
    p1 = ubuf[SUBLANES - 1:SUBLANES - 1 + tm, :]
    p2 = ubuf[SUBLANES - 2:SUBLANES - 2 + tm, :]
    row = lax.broadcasted_iota(I32, (tm, 1), 0)
    if seq_len % tm == 0:
        pos = (i * tm) % seq_len + row
    else:
        pos = row % seq_len
    s0 = st0_ref[...] if has_state else 0.0
    s1 = st1_ref[...] if has_state else 0.0
    prev1 = jnp.where(pos >= 1, p1, s1)
    prev2 = jnp.where(pos >= 2, p2, jnp.where(pos == 1, s1, s0))
    conv = cw_ref[0:1, :] * prev2 + cw_ref[1:2, :] * prev1 + cw_ref[2:3, :] * u
    y_conv = jnp.dot((cb_ref[...].astype(F32) * conv).astype(MXU_DTYPE), wco_ref[...],
                     preferred_element_type=F32)
    merged = ga_ref[...].astype(F32) * y_att + gb_ref[...].astype(F32) * y_conv
    z = alpha * x_ref[...] + jnp.dot(merged.astype(MXU_DTYPE), wo_ref[...], preferred_element_type=F32)
    x1 = _layer_norm(z, g_ref[...], b_ref[...])
    x1_ref[...] = x1


def _mix(x2d, attn, u, state, cb, ga, gb, conv_w, wao, wco, wo, g, b, *, tm, seq_len, attn_transposed, alpha):
    t, d = x2d.shape
    cw = u.shape[1]
    assert seq_len % tm == 0 or tm % seq_len == 0
    nt = t // tm
    row = lambda i: (i, 0)
    full = lambda a: pl.BlockSpec(a.shape, lambda i: (0,) * a.ndim)
    has_state = state is not None
    hb = tm // SUBLANES
    attn_spec = (pl.BlockSpec((ATT_W, tm), lambda i: (0, i)) if attn_transposed
                 else pl.BlockSpec((tm, ATT_W), row))
    args = [x2d, attn, u, u]
    in_specs = [pl.BlockSpec((tm, d), row), attn_spec, pl.BlockSpec((tm, cw), row),
                pl.BlockSpec((SUBLANES, cw), lambda i: (jnp.maximum(i * hb - 1, 0), 0))]
    if has_state:
        args += list(state)
        in_specs += [pl.BlockSpec((tm, cw), row)] * 2
    args += [cb, ga, gb, conv_w, wao, wco, wo, g, b]
    in_specs += [pl.BlockSpec((tm, cw), row), pl.BlockSpec((tm, d), row), pl.BlockSpec((tm, d), row),
                 full(conv_w), full(wao), full(wco), full(wo), full(g), full(b)]
    return pl.pallas_call(
        functools.partial(_mix_kernel, tm=tm, seq_len=seq_len, attn_transposed=attn_transposed,
                          has_state=has_state, alpha=alpha),
        grid=(nt,),
        in_specs=in_specs,
        out_specs=pl.BlockSpec((tm, d), row),
        out_shape=jax.ShapeDtypeStruct((t, d), F32),
        scratch_shapes=[pltpu.VMEM((SUBLANES + tm, cw), F32)],
        compiler_params=pltpu.CompilerParams(dimension_semantics=("arbitrary",),
                                             vmem_limit_bytes=VMEM_LIMIT),
        name="mix",
    )(*args)


def _router_kernel(xp_ref, xs_ref, wT_ref, b_ref, idx_ref, gate_ref, rank_ref, cnt_ref, seen_ref, *, ntp):
    i = pl.program_id(0)
    ne, tm = wT_ref.shape[0], xp_ref.shape[0]
    x = jnp.where(i < ntp, xp_ref[...], xs_ref[...])

    @pl.when(i == 0)
    def _():
        seen_ref[...] = jnp.zeros(seen_ref.shape, F32)

    def halves(a):
        hi = a.astype(MXU_DTYPE)
        return hi, (a - hi.astype(F32)).astype(MXU_DTYPE)

    def nt(a, b):
        return lax.dot_general(a, b, _nt_dims(), preferred_element_type=F32)

    (w_hi, w_lo), (x_hi, x_lo) = halves(wT_ref[...]), halves(x)
    logits = nt(w_hi, x_hi) + nt(w_hi, x_lo) + nt(w_lo, x_hi) + b_ref[...]
    eid = lax.broadcasted_iota(I32, (ne, tm), 0)
    cnt = jnp.zeros((ne, tm), I32)
    for e in range(ne):
        row = logits[e:e + 1, :]
        beats = (row > logits) | ((row == logits) & (e < eid))
        cnt = cnt + jnp.where(beats, 1, 0)
    sel = cnt < TOP_K
    lmax = jnp.max(logits, axis=0, keepdims=True)
    ex = jnp.where(sel, jnp.exp(logits - lmax), 0.0)
    gates = ex / jnp.sum(ex, axis=0, keepdims=True)
    self_ = jnp.where(sel, 1.0, 0.0)
    tt = lax.broadcasted_iota(I32, (tm, tm), 0) < lax.broadcasted_iota(I32, (tm, tm), 1)
    earlier = jnp.where(tt, 1.0, 0.0).astype(MXU_DTYPE)
    seen = seen_ref[:, 0:1]
    rank = jnp.dot(self_.astype(MXU_DTYPE), earlier, preferred_element_type=F32) + seen
    seen = seen + jnp.sum(self_, axis=1, keepdims=True)
    seen_ref[...] = jnp.broadcast_to(seen, seen_ref.shape)
    cnt_ref[...] = jnp.broadcast_to(seen, cnt_ref.shape)
    ee = lax.broadcasted_iota(I32, (ne, ne), 1) < lax.broadcasted_iota(I32, (ne, ne), 0)
    below = jnp.dot(jnp.where(ee, 1.0, 0.0).astype(MXU_DTYPE), self_.astype(MXU_DTYPE),
                    preferred_element_type=F32)
    eidf = eid.astype(F32)
    for kk in range(TOP_K):
        hit = sel & (below == float(kk))
        idx_ref[kk:kk + 1, :] = jnp.sum(jnp.where(hit, eidf, 0.0), axis=0, keepdims=True).astype(I32)
        gate_ref[kk:kk + 1, :] = jnp.sum(jnp.where(hit, gates, 0.0), axis=0, keepdims=True)
        rank_ref[kk:kk + 1, :] = jnp.sum(jnp.where(hit, rank, 0.0), axis=0, keepdims=True).astype(I32)


def _route(x1p, x1s, router_wT, router_b_col, *, tm):
    d = x1p.shape[1]
    ntp, nts = x1p.shape[0] // tm, x1s.shape[0] // tm
    t = (ntp + nts) * tm
    ne = router_wT.shape[0]
    col = lambda i: (0, i)
    return pl.pallas_call(
        functools.partial(_router_kernel, ntp=ntp),
        grid=(ntp + nts,),
        in_specs=[pl.BlockSpec((tm, d), lambda i: (jnp.minimum(i, ntp - 1), 0)),
                  pl.BlockSpec((tm, d), lambda i: (jnp.maximum(i - ntp, 0), 0)),
                  pl.BlockSpec((ne, d), lambda i: (0, 0)),
                  pl.BlockSpec((ne, 1), lambda i: (0, 0))],
        out_specs=[pl.BlockSpec((TOP_K, tm), col), pl.BlockSpec((TOP_K, tm), col),
                   pl.BlockSpec((TOP_K, tm), col), pl.BlockSpec((ne, LANES), lambda i: (0, 0))],
        out_shape=[jax.ShapeDtypeStruct((TOP_K, t), I32), jax.ShapeDtypeStruct((TOP_K, t), F32),
                   jax.ShapeDtypeStruct((TOP_K, t), I32), jax.ShapeDtypeStruct((ne, LANES), F32)],
        scratch_shapes=[pltpu.VMEM((ne, LANES), F32)],
        compiler_params=pltpu.CompilerParams(dimension_semantics=("arbitrary",),
                                             vmem_limit_bytes=VMEM_LIMIT),
        name="router",
    )(x1p, x1s, router_wT, router_b_col)


def _dispatch_kernel(dest_ref, pend_ref, xp_ref, xs_ref, xb_ref, zeros_ref, sem, zsem, *, t, tm, tile, ne, ntp):
    i = pl.program_id(0)

    def row_copy(x_ref, r, dst):
        return pltpu.make_async_copy(x_ref.at[pl.ds(r, 1)], xb_ref.at[pl.ds(dst, 1)], sem)

    def clear_copy(e):
        start = pl.multiple_of(pend_ref[e] - tile, tile)
        return pltpu.make_async_copy(zeros_ref, xb_ref.at[pl.ds(start, tile)], zsem)

    @pl.when(i == 0)
    def _():
        zeros_ref[...] = jnp.zeros(zeros_ref.shape, F32)
        for phase in ("start", "wait"):
            for e in range(ne):
                has_rows = pend_ref[e] > (pend_ref[e - 1] if e else 0)

                @pl.when(has_rows)
                def _():
                    if phase == "start":
                        clear_copy(e).start()
                    else:
                        clear_copy(e).wait()

        def tail_copy(j):
            return pltpu.make_async_copy(zeros_ref, xb_ref.at[pl.ds(pl.multiple_of(j * tile, tile), tile)], zsem)

        first_unused = pend_ref[ne - 1] // tile
        n_tiles = xb_ref.shape[0] // tile
        lax.fori_loop(first_unused, n_tiles, lambda j, c: (tail_copy(j).start(), c)[1], 0)
        lax.fori_loop(first_unused, n_tiles, lambda j, c: (tail_copy(j).wait(), c)[1], 0)

    def issue_from(x_ref):
        def issue(r, carry):
            for kk in range(TOP_K):
                row_copy(x_ref, r, dest_ref[kk * t + i * tm + r]).start(priority=kk % 2)
            return carry

        lax.fori_loop(0, tm, issue, 0, unroll=DISPATCH_UNROLL)

    pl.when(i < ntp)(lambda: issue_from(xp_ref))
    pl.when(i >= ntp)(lambda: issue_from(xs_ref))
    for kk in range(TOP_K):
        pltpu.make_async_copy(xp_ref, xb_ref.at[pl.ds(0, tm)], sem).wait()


def _dispatch(dest_flat, pad_end, x1p, x1s, *, rows, tm, tile):
    d = x1p.shape[1]
    ntp, nts = x1p.shape[0] // tm, x1s.shape[0] // tm
    t = (ntp + nts) * tm
    ne = pad_end.shape[0]
    grid_spec = pltpu.PrefetchScalarGridSpec(
        num_scalar_prefetch=2,
        grid=(ntp + nts,),
        in_specs=[pl.BlockSpec((tm, d), lambda i, dest, pend: (jnp.minimum(i, ntp - 1), 0)),
                  pl.BlockSpec((tm, d), lambda i, dest, pend: (jnp.maximum(i - ntp, 0), 0))],
        out_specs=pl.BlockSpec(memory_space=pl.ANY),
        scratch_shapes=[pltpu.VMEM((tile, d), F32), pltpu.SemaphoreType.DMA, pltpu.SemaphoreType.DMA],
    )
    return pl.pallas_call(
        functools.partial(_dispatch_kernel, t=t, tm=tm, tile=tile, ne=ne, ntp=ntp),
        grid_spec=grid_spec,
        out_shape=jax.ShapeDtypeStruct((rows, d), F32),
        compiler_params=pltpu.CompilerParams(dimension_semantics=("arbitrary",),
                                             vmem_limit_bytes=VMEM_LIMIT),
        name="dispatch",
    )(dest_flat, pad_end, x1p, x1s)


def _expert_kernel(te_ref, nu_ref, x_ref, wgu_ref, bg_ref, bu_ref, wd_ref, bd_ref, y_ref, wg_s, wu_s, wd_s):
    i = pl.program_id(0)
    new_expert = jnp.logical_or(i == 0, te_ref[i] != te_ref[jnp.maximum(i - 1, 0)])

    @pl.when(jnp.logical_and(new_expert, i < nu_ref[0]))
    def _():
        half = LANES
        c = lax.broadcasted_iota(I32, (2 * half, 2 * half), 0)
        o = lax.broadcasted_iota(I32, (2 * half, 2 * half), 1)
        src = jnp.where(o < half, 2 * o, 2 * (o - half) + 1)
        perm = jnp.where(c == src, 1.0, 0.0).astype(MXU_DTYPE)
        for j in range(wgu_ref.shape[2] // (2 * half)):
            wb = wgu_ref[0, :, 2 * half * j:2 * half * (j + 1)].astype(MXU_DTYPE)
            y = jnp.dot(wb, perm, preferred_element_type=F32).astype(MXU_DTYPE)
            wg_s[:, half * j:half * (j + 1)] = y[:, :half]
            wu_s[:, half * j:half * (j + 1)] = y[:, half:]
        wd_s[...] = wd_ref[0].astype(MXU_DTYPE)

    @pl.when(i < nu_ref[0])
    def _():
        x = x_ref[...].astype(MXU_DTYPE)
        hg = jnp.dot(x, wg_s[...], preferred_element_type=F32) + bg_ref[0]
        hu = jnp.dot(x, wu_s[...], preferred_element_type=F32) + bu_ref[0]
        g = jnp.minimum(hg, SWIGLU_LIMIT)
        u = jnp.clip(hu, -SWIGLU_LIMIT, SWIGLU_LIMIT)
        a = g * _sigmoid(SWIGLU_ALPHA * g) * (u + 1.0)
        y_ref[...] = jnp.dot(a.astype(MXU_DTYPE), wd_s[...], preferred_element_type=F32) + bd_ref[0]

    @pl.when(i >= nu_ref[0])
    def _():
        y_ref[...] = jnp.zeros(y_ref.shape, F32)


def _experts(tile_e, n_used, xb, w_gate_up, bg, bu, w_down, bd, *, tm):
    rows, d = xb.shape
    dff = w_down.shape[1]
    nt = rows // tm
    ew = lambda i, te, nu: (te[i], 0, 0)
    grid_spec = pltpu.PrefetchScalarGridSpec(
        num_scalar_prefetch=2,
        grid=(nt,),
        in_specs=[pl.BlockSpec((tm, d), lambda i, te, nu: (jnp.minimum(i, nu[0] - 1), 0)),
                  pl.BlockSpec((1, d, 2 * dff), ew),
                  pl.BlockSpec((1, 1, dff), ew), pl.BlockSpec((1, 1, dff), ew),
                  pl.BlockSpec((1, dff, d), ew), pl.BlockSpec((1, 1, d), ew)],
        out_specs=pl.BlockSpec((tm, d), lambda i, te, nu: (i, 0)),
        scratch_shapes=[pltpu.VMEM((d, dff), MXU_DTYPE), pltpu.VMEM((d, dff), MXU_DTYPE),
                        pltpu.VMEM((dff, d), MXU_DTYPE)],
    )
    return pl.pallas_call(
        _expert_kernel,
        grid_spec=grid_spec,
        out_shape=jax.ShapeDtypeStruct((rows, d), F32),
        compiler_params=pltpu.CompilerParams(dimension_semantics=("arbitrary",),
                                             vmem_limit_bytes=VMEM_LIMIT),
        name="experts",
    )(tile_e, n_used, xb, w_gate_up, bg, bu, w_down, bd)


def _norm2_kernel(x1_ref, gate_ref, *refs, alpha):
    y_refs, (g_ref, b_ref, o_ref) = refs[:TOP_K], refs[TOP_K:]
    gate = gate_ref[...]
    ffn = gate[:, 0:1] * y_refs[0][...]
    for kk in range(1, TOP_K):
        ffn = ffn + gate[:, kk:kk + 1] * y_refs[kk][...]
    o_ref[...] = _layer_norm(alpha * x1_ref[...] + ffn, g_ref[...], b_ref[...])


def _norm2(x1, gate_t, y_pairs, g, b, *, row0, tm, alpha):
    nrows, d = x1.shape
    off = row0 // tm
    src = lambda i: (i + off, 0)
    return pl.pallas_call(
        functools.partial(_norm2_kernel, alpha=alpha),
        grid=(nrows // tm,),
        in_specs=[pl.BlockSpec((tm, d), lambda i: (i, 0)), pl.BlockSpec((tm, TOP_K), src)]
        + [pl.BlockSpec((tm, d), src)] * TOP_K
        + [pl.BlockSpec((1, d), lambda i: (0, 0)), pl.BlockSpec((1, d), lambda i: (0, 0))],
        out_specs=pl.BlockSpec((tm, d), lambda i: (i, 0)),
        out_shape=jax.ShapeDtypeStruct((nrows, d), F32),
        compiler_params=pltpu.CompilerParams(dimension_semantics=("arbitrary",),
                                             vmem_limit_bytes=VMEM_LIMIT),
        name="norm2",
    )(x1, gate_t, *y_pairs, g, b)


def _moe(x1p, x1s, router_w, router_b, w_gate_up, b_gate_up, w_down, b_down):
    t = x1p.shape[0] + x1s.shape[0]
    ne = router_w.shape[1]
    idx, gate, rank, counts = _route(x1p, x1s, router_w.T, router_b.reshape(ne, 1), tm=TOKEN_TILE)
    tm = EXPERT_TILE
    counts = counts[:, 0].astype(I32)
    padded = (counts + tm - 1) // tm * tm
    pad_end = jnp.cumsum(padded)
    pad_start = pad_end - padded
    n_tiles = (t * TOP_K + ne * (tm - 1)) // tm
    rows = n_tiles * tm
    eids = jnp.arange(ne, dtype=I32)
    start_of = jnp.sum(jnp.where(idx[:, :, None] == eids, pad_start, 0), axis=-1)
    dest = start_of + rank
    tile_row = jnp.arange(n_tiles, dtype=I32) * tm
    tile_e = jnp.minimum(jnp.sum((pad_end[None, :] <= tile_row[:, None]).astype(I32), axis=1), ne - 1)
    n_used = (pad_end[-1:] // tm).astype(I32)
    xb = _dispatch(dest.reshape(-1), pad_end.astype(I32), x1p, x1s, rows=rows, tm=TOKEN_TILE, tile=tm)
    bg = b_gate_up[:, None, 0::2]
    bu = b_gate_up[:, None, 1::2]
    yb = _experts(tile_e, n_used, xb, w_gate_up, bg, bu, w_down, b_down[:, None, :], tm=tm)
    return gate.T, [yb[dest[kk]] for kk in range(TOP_K)]


def kernel(x_prompt, x_sample, cache_k, cache_v, state_conv, page_table, rel_bias, w_in, conv_w, w_att_o,
           w_conv_o, w_o, ln1_g, ln1_b, router_w, router_b, w_gate_up, b_gate_up, w_down, b_down, ln2_g, ln2_b):
    depth = w_in.shape[0]
    batch, seq, d = x_prompt.shape
    nseq, tq, _ = x_sample.shape
    n_pool = cache_k.shape[1]
    ppseq = page_table.shape[1]
    past = ppseq * PAGE_SIZE
    assert seq % MOBA_BLOCK == 0 and past % MOBA_BLOCK == 0 and tq % SUBLANES == 0 and tq <= PAGE_SIZE
    nb_past = past // MOBA_BLOCK
    tp, ts = batch * seq, nseq * tq
    alpha = (2 * depth) ** 0.25
    cw = conv_w.shape[2]
    tab_rows = jnp.repeat(rel_bias.T, tq, axis=0)
    ck = cache_k.transpose(0, 1, 3, 4, 2).reshape(depth * n_pool, ATT_W, PAGE_SIZE)
    cv = cache_v.transpose(0, 1, 3, 4, 2).reshape(depth * n_pool, ATT_W, PAGE_SIZE)
    hp = x_prompt.reshape(tp, d)
    hs = x_sample.reshape(ts, d)
    outs = [[] for _ in range(6)]
    for l in range(depth):
        w_in_b = w_in[l].astype(MXU_DTYPE)
        wao, wco, wo = (w_att_o[l].astype(MXU_DTYPE), w_conv_o[l].astype(MXU_DTYPE), w_o[l].astype(MXU_DTYPE))
        g1, b1, g2, b2 = ln1_g[l][None], ln1_b[l][None], ln2_g[l][None], ln2_b[l][None]
        qT, khm, vT, kp, vp, km, up, cbp, gap, gbp = _project(hp, w_in_b, tm=PROJ_TILE, head_major=True, seq=seq)
        nbt = PROJ_TILE // MOBA_BLOCK
        kmean = km[:, :nbt].reshape(batch, seq // MOBA_BLOCK, N_HEADS, HEAD_DIM).transpose(0, 2, 1, 3)
        kmean = kmean.reshape(batch * N_HEADS, seq // MOBA_BLOCK, HEAD_DIM)
        attn_p = _moba_prompt(rel_bias, qT, khm, vT, kmean, batch=batch, seq=seq)
        x1p = _mix(hp, attn_p, up, None, cbp, gap, gbp, conv_w[l], wao, wco, wo, g1, b1,
                         tm=PROJ_TILE, seq_len=seq, attn_transposed=True, alpha=alpha)
        qs, ks, vs, us, cbs, gas, gbs = _project(hs, w_in_b, tm=ts, head_major=False, wq=w_in[l][:, :ATT_W])
        pages = (page_table + l * n_pool).reshape(-1).astype(I32)
        attn_s = _moba_sample(pages, qs, ks, vs, tab_rows, ck, cv, nseq=nseq, tq=tq, nb_past=nb_past)
        state = (jnp.repeat(state_conv[l][:, 0], tq, axis=0), jnp.repeat(state_conv[l][:, 1], tq, axis=0))
        x1s = _mix(hs, attn_s, us, state, cbs, gas, gbs, conv_w[l], wao, wco, wo, g1, b1,
                         tm=ts, seq_len=tq, attn_transposed=False, alpha=alpha)
        gate_t, y_pairs = _moe(x1p, x1s, router_w[l], router_b[l], w_gate_up[l], b_gate_up[l], w_down[l], b_down[l])
        hp = _norm2(x1p, gate_t, y_pairs, g2, b2, row0=0, tm=TOKEN_TILE, alpha=alpha)
        hs = _norm2(x1s, gate_t, y_pairs, g2, b2, row0=tp, tm=TOKEN_TILE, alpha=alpha)
        outs[0].append(kp.reshape(batch, N_HEADS, HEAD_DIM, seq).transpose(0, 3, 1, 2))
        outs[1].append(vp.reshape(batch, N_HEADS, HEAD_DIM, seq).transpose(0, 3, 1, 2))
        outs[2].append(up.reshape(batch, seq, cw)[:, seq - (CONV_K - 1):])
        outs[3].append(ks.reshape(nseq, tq, N_HEADS, HEAD_DIM))
        outs[4].append(vs.reshape(nseq, tq, N_HEADS, HEAD_DIM))
        outs[5].append(us.reshape(nseq, tq, cw)[:, tq - (CONV_K - 1):])
    stacked = [jnp.stack(o) for o in outs]
    return (hp.reshape(batch, seq, d), hs.reshape(nseq, tq, d), *stacked)
```

```python
import functools
import math

import jax
import jax.numpy as jnp
from jax import lax
from jax.experimental import pallas as pl
from jax.experimental.pallas import tpu as pltpu

F32 = jnp.float32
I32 = jnp.int32
MXU_DTYPE = jnp.bfloat16
HIGHEST = lax.Precision.HIGHEST

N_HEADS = 8
HEAD_DIM = 64
ATT_W = N_HEADS * HEAD_DIM
MOBA_BLOCK = 256
MOBA_TOPK = 3
PAGE_SIZE = 128
PAGES_PER_BLOCK = MOBA_BLOCK // PAGE_SIZE
CONV_K = 3
N_EXPERTS = 32
TOP_K = 4
SWIGLU_LIMIT = 7.0
SWIGLU_ALPHA = 1.702
NUM_BUCKETS = 32
NUM_EXACT = NUM_BUCKETS // 2
MAX_DISTANCE = 128
LN_EPS = 1e-5
SCORE_SCALE = HEAD_DIM ** -0.5
NEG = -1e30

SUBLANES = 8
LANES = 128
PACKED_ROWS = 16
V_ROWS = HEAD_DIM + PACKED_ROWS

PROJ_TILE = 512
TOKEN_TILE = 256
EXPERT_TILE = 512
DISPATCH_UNROLL = 8
PROMPT_ROUND_UNROLL = 4
SAMPLE_CHUNK_BLOCKS = 16
VMEM_LIMIT = 56 * 1024 * 1024

assert MOBA_BLOCK >= MAX_DISTANCE


def _nt_dims():
    return (((1,), (1,)), ((), ()))


def _sigmoid(x):
    return 1.0 / (1.0 + jnp.exp(-x))


def _bucket(dist):
    n = jnp.maximum(dist, 0)
    nf = jnp.maximum(n, 1).astype(F32)
    large = NUM_EXACT + (jnp.log(nf / NUM_EXACT) / math.log(MAX_DISTANCE / NUM_EXACT)
                         * (NUM_BUCKETS - NUM_EXACT)).astype(I32)
    large = jnp.minimum(large, NUM_BUCKETS - 1)
    return jnp.where(n < NUM_EXACT, n, large)


def _layer_norm(z, g, b):
    mu = jnp.mean(z, axis=-1, keepdims=True)
    zc = z - mu
    var = jnp.mean(zc * zc, axis=-1, keepdims=True)
    return zc * lax.rsqrt(var + LN_EPS) * g + b


def _proj_kernel(x_ref, w_ref, *refs, tm, head_major, blocks_per_seq):
    xb = x_ref[...].astype(MXU_DTYPE)

    def mm(c0, width):
        return jnp.dot(xb, w_ref[:, c0:c0 + width], preferred_element_type=F32)

    if head_major:
        qT_ref, khm_ref, vT_ref, k_ref, v_ref, km_ref, u_ref, cb_ref, ga_ref, gb_ref = refs
        q = mm(0, ATT_W) * SCORE_SCALE
    else:
        wq_ref, q_ref, k_ref, v_ref, u_ref, cb_ref, ga_ref, gb_ref = refs
        q = jnp.dot(x_ref[...], wq_ref[...], preferred_element_type=F32, precision=HIGHEST) * SCORE_SCALE
    d_model = x_ref.shape[1]
    k = mm(ATT_W, ATT_W)
    v = mm(2 * ATT_W, ATT_W)
    if head_major:
        qT_ref[...] = q.T
        k_ref[0] = k.T
        vt = v.T
        v_ref[0] = vt
        km_ref[...] = jnp.zeros(km_ref.shape, F32)
        lane = lax.broadcasted_iota(I32, (MOBA_BLOCK, 2 * HEAD_DIM), 1)
        ones_rows = jnp.ones((V_ROWS - HEAD_DIM, MOBA_BLOCK), MXU_DTYPE)
        for r in range(tm // MOBA_BLOCK):
            kr = k[r * MOBA_BLOCK:(r + 1) * MOBA_BLOCK]
            km_ref[0, r:r + 1, :] = jnp.sum(kr, axis=0, keepdims=True) * (1.0 / MOBA_BLOCK)
            n_blk = (pl.program_id(0) * (tm // MOBA_BLOCK) + r) % blocks_per_seq
            tag = jnp.where(lane - HEAD_DIM == n_blk, 1.0, 0.0)
            vtr = vt[:, r * MOBA_BLOCK:(r + 1) * MOBA_BLOCK]
            for h in range(N_HEADS):
                pair = kr[:, (h // 2) * 2 * HEAD_DIM:(h // 2 + 1) * 2 * HEAD_DIM]
                if h % 2:
                    pair = pltpu.roll(pair, HEAD_DIM, axis=1)
                khm_ref[h, r] = jnp.where(lane < HEAD_DIM, pair, tag).astype(MXU_DTYPE)
                vT_ref[r, h * V_ROWS:h * V_ROWS + HEAD_DIM] = vtr[h * HEAD_DIM:(h + 1) * HEAD_DIM].astype(MXU_DTYPE)
                vT_ref[r, h * V_ROWS + HEAD_DIM:(h + 1) * V_ROWS] = ones_rows
    else:
        q_ref[...] = q
        k_ref[...] = k
        v_ref[...] = v
    c0 = 3 * ATT_W
    cw = u_ref.shape[1]
    cb_ref[...] = mm(c0, cw).astype(cb_ref.dtype)
    u_ref[...] = mm(c0 + cw, cw) * mm(c0 + 2 * cw, cw)
    ga_ref[...] = _sigmoid(mm(c0 + 3 * cw, d_model)).astype(ga_ref.dtype)
    gb_ref[...] = _sigmoid(mm(c0 + 3 * cw + d_model, d_model)).astype(gb_ref.dtype)


def _project(x2d, w_in_b, *, tm, head_major, seq=None, wq=None):
    t, d = x2d.shape
    cw = (w_in_b.shape[1] - 3 * ATT_W - 2 * d) // 3
    nt = t // tm
    row = lambda i: (i, 0)
    f32s = lambda shape: jax.ShapeDtypeStruct(shape, F32)
    mxs = lambda shape: jax.ShapeDtypeStruct(shape, MXU_DTYPE)
    tail_shapes = [f32s((t, cw)), mxs((t, cw)), mxs((t, d)), mxs((t, d))]
    tail_specs = [pl.BlockSpec((tm, cw), row), pl.BlockSpec((tm, cw), row),
                  pl.BlockSpec((tm, d), row), pl.BlockSpec((tm, d), row)]
    if head_major:
        nbt = tm // MOBA_BLOCK
        tps = seq // tm
        seq_t = pl.BlockSpec((1, ATT_W, tm), lambda i: (i // tps, 0, i % tps))
        out_shape = [f32s((ATT_W, t)), mxs((N_HEADS, t // MOBA_BLOCK, MOBA_BLOCK, 2 * HEAD_DIM)),
                     mxs((t // MOBA_BLOCK, N_HEADS * V_ROWS, MOBA_BLOCK)), f32s((t // seq, ATT_W, seq)),
                     f32s((t // seq, ATT_W, seq)), f32s((nt, SUBLANES, ATT_W))] + tail_shapes
        out_specs = [pl.BlockSpec((ATT_W, tm), lambda i: (0, i)),
                     pl.BlockSpec((N_HEADS, nbt, MOBA_BLOCK, 2 * HEAD_DIM), lambda i: (0, i, 0, 0)),
                     pl.BlockSpec((nbt, N_HEADS * V_ROWS, MOBA_BLOCK), lambda i: (i, 0, 0)),
                     seq_t, seq_t,
                     pl.BlockSpec((1, SUBLANES, ATT_W), lambda i: (i, 0, 0))] + tail_specs
        args, extra_specs = (x2d, w_in_b), []
    else:
        out_shape = [f32s((t, ATT_W))] * 3 + tail_shapes
        out_specs = [pl.BlockSpec((tm, ATT_W), row)] * 3 + tail_specs
        args, extra_specs = (x2d, w_in_b, wq), [pl.BlockSpec(wq.shape, lambda i: (0, 0))]
    return pl.pallas_call(
        functools.partial(_proj_kernel, tm=tm, head_major=head_major,
                          blocks_per_seq=seq // MOBA_BLOCK if head_major else None),
        grid=(nt,),
        in_specs=[pl.BlockSpec((tm, d), row), pl.BlockSpec(w_in_b.shape, lambda i: (0, 0))] + extra_specs,
        out_specs=out_specs,
        out_shape=out_shape,
        compiler_params=pltpu.CompilerParams(dimension_semantics=("arbitrary",),
                                             vmem_limit_bytes=VMEM_LIMIT),
        name="proj",
    )(*args)


def _moba_prompt_kernel(tab_ref, qT_ref, k_ref, vT_ref, km_ref, o_ref, bias_ref, qb_ref, *state):
    h = pl.program_id(0)
    first_sequence = pl.program_id(1) == 0
    nb = km_ref.shape[1]
    blk = MOBA_BLOCK
    n_streams = nb // 2
    m_refs, acc_refs = state[:n_streams], state[n_streams:]

    @pl.when(first_sequence)
    def _():
        kk = lax.broadcasted_iota(I32, (blk, blk), 0)
        qq = lax.broadcasted_iota(I32, (blk, blk), 1)
        for age in range(2):
            dist = qq - kk + age * blk
            bucket = _bucket(dist)
            tile = jnp.zeros((blk, blk), F32)
            for b in range(NUM_BUCKETS):
                tile = jnp.where(bucket == b, tab_ref[b, h], tile)
            bias_ref[age] = jnp.where(dist >= 0, tile, NEG)
        bias_ref[2] = jnp.full((blk, blk), tab_ref[NUM_BUCKETS - 1, h], F32)

    km = km_ref[0]
    nid = lax.broadcasted_iota(I32, (nb, blk), 0)
    spare = jnp.zeros((HEAD_DIM - nb, blk), F32)
    for i in range(nb):
        qT = qT_ref[:, i * blk:(i + 1) * blk]
        gT = jnp.dot(km, qT, preferred_element_type=F32, precision=HIGHEST)
        cnt = jnp.zeros((nb, blk), I32)
        for m in range(i):
            row = gT[m:m + 1, :]
            beats = (row > gT) | ((row == gT) & (m < nid))
            cnt = cnt + jnp.where(beats, 1, 0)
        keep = ((cnt < MOBA_TOPK) & (nid < i)) | (nid == i)
        qb_ref[i] = jnp.concatenate([qT, jnp.where(keep, 0.0, NEG), spare], axis=0).astype(MXU_DTYPE)

    for a in range(n_streams):
        m_refs[a][...] = jnp.full(m_refs[a].shape, NEG, F32)
        acc_refs[a][...] = jnp.zeros(acc_refs[a].shape, F32)

    def sweep_round(t, carry):
        staged = []
        for a in range(n_streams):
            first = t <= a
            slot = jnp.where(first, 0, 1)
            i = jnp.where(first, a, nb - 1 - a)
            n = jnp.where(first, a - t, nb - t)
            age = jnp.minimum(i - n, 2)
            s = jnp.dot(k_ref[0, n], qb_ref[i], preferred_element_type=F32) + bias_ref[age]
            staged.append((slot, n, s))
        updates = []
        for a, (slot, n, s) in enumerate(staged):
            m_old = m_refs[a][slot]
            m_new = jnp.maximum(m_old, jnp.max(s, axis=0, keepdims=True))
            alpha = jnp.exp(m_old - m_new)
            p = jnp.exp(s - m_new)
            acc_new = alpha * acc_refs[a][slot] + jnp.dot(
                vT_ref[n], p.astype(MXU_DTYPE), preferred_element_type=F32)
            updates.append((slot, m_new, acc_new))
        for a, (slot, m_new, acc_new) in enumerate(updates):
            m_refs[a][slot] = m_new
            acc_refs[a][slot] = acc_new
        return carry

    lax.fori_loop(0, nb + 1, sweep_round, 0, unroll=PROMPT_ROUND_UNROLL)
    for a in range(n_streams):
        for slot, i in ((0, a), (1, nb - 1 - a)):
            acc = acc_refs[a][slot]
            o_ref[:, i * blk:(i + 1) * blk] = (acc[:HEAD_DIM] / acc[HEAD_DIM:HEAD_DIM + 1]).astype(o_ref.dtype)


def _moba_prompt(rel_bias, qT, khm, vT, kmean, *, batch, seq):
    nb = seq // MOBA_BLOCK
    assert nb % 2 == 0 and nb <= HEAD_DIM
    t = batch * seq
    stream = lambda shape: [pltpu.VMEM((2,) + shape, F32)] * (nb // 2)
    return pl.pallas_call(
        _moba_prompt_kernel,
        grid=(N_HEADS, batch),
        in_specs=[
            pl.BlockSpec(memory_space=pltpu.SMEM),
            pl.BlockSpec((HEAD_DIM, seq), lambda h, b: (h, b)),
            pl.BlockSpec((1, nb, MOBA_BLOCK, 2 * HEAD_DIM), lambda h, b: (h, b, 0, 0)),
            pl.BlockSpec((nb, V_ROWS, MOBA_BLOCK), lambda h, b: (b, h, 0)),
            pl.BlockSpec((1, nb, HEAD_DIM), lambda h, b: (b * N_HEADS + h, 0, 0)),
        ],
        out_specs=pl.BlockSpec((HEAD_DIM, seq), lambda h, b: (h, b)),
        out_shape=jax.ShapeDtypeStruct((ATT_W, t), MXU_DTYPE),
        scratch_shapes=[pltpu.VMEM((3, MOBA_BLOCK, MOBA_BLOCK), F32),
                        pltpu.VMEM((nb, 2 * HEAD_DIM, MOBA_BLOCK), MXU_DTYPE)]
        + stream((1, MOBA_BLOCK)) + stream((V_ROWS, MOBA_BLOCK)),
        compiler_params=pltpu.CompilerParams(dimension_semantics=("arbitrary",) * 2,
                                             vmem_limit_bytes=VMEM_LIMIT),
        name="moba_prompt",
    )(rel_bias, qT, khm, vT, kmean)


def _moba_sample_kernel(pt_ref, q_ref, kn_ref, vn_ref, tab_ref, *refs, tq, nb_past, cb):
    del pt_ref
    npg = cb * PAGES_PER_BLOCK
    k_pages = refs[:npg]
    v_pages = refs[npg:2 * npg]
    o_ref, km_ref, m_ref, l_ref, acc_ref = refs[2 * npg:]
    c = pl.program_id(1)
    n_chunks = nb_past // cb
    rows = N_HEADS * tq
    width = q_ref.shape[1]
    r_id = lax.broadcasted_iota(I32, (rows, width), 0)
    lane = lax.broadcasted_iota(I32, (rows, width), 1)
    head_mask = (lane // HEAD_DIM) == (r_id // tq)
    q_rows = jnp.concatenate([q_ref[...]] * N_HEADS, axis=0)
    qbd = jnp.where(head_mask, q_rows, 0.0)
    qbd_b = qbd.astype(MXU_DTYPE)
    tab = tab_ref[...]

    def bias_rows(dist):
        bucket = _bucket(dist)
        out = jnp.zeros(dist.shape, F32)
        for b in range(NUM_BUCKETS):
            out = jnp.where(bucket == b, tab[:, b:b + 1], out)
        return out

    c_far = tab[:, NUM_BUCKETS - 1:NUM_BUCKETS]

    @pl.when(c == 0)
    def _():
        km_ref[...] = jnp.zeros(km_ref.shape, F32)
        m_ref[...] = jnp.full(m_ref.shape, NEG, F32)
        l_ref[...] = jnp.zeros(l_ref.shape, F32)

    km_lane = lax.broadcasted_iota(I32, km_ref.shape, 1)
    st_lane = lax.broadcasted_iota(I32, m_ref.shape, 1)
    qi = lax.broadcasted_iota(I32, (rows, MOBA_BLOCK), 0) % tq
    kj = lax.broadcasted_iota(I32, (rows, MOBA_BLOCK), 1)
    last_bias = lax.cond(c == n_chunks - 1, lambda: bias_rows(MOBA_BLOCK + qi - kj),
                         lambda: jnp.broadcast_to(c_far, (rows, MOBA_BLOCK)))
    scores = []
    km_new = jnp.zeros(km_ref.shape, F32)
    for j in range(cb):
        kts = [k_pages[PAGES_PER_BLOCK * j + g][0] for g in range(PAGES_PER_BLOCK)]
        ksum = kts[0]
        for kt in kts[1:]:
            ksum = ksum + kt
        kmean = jnp.sum(ksum, axis=1, keepdims=True) * (1.0 / MOBA_BLOCK)
        km_new = jnp.where(km_lane == c * cb + j, kmean, km_new)
        scores.append(jnp.concatenate(
            [jnp.dot(qbd_b, kt.astype(MXU_DTYPE), preferred_element_type=F32) for kt in kts], axis=1))
    km_ref[...] = km_ref[...] + km_new
    probs = []
    m_new = jnp.full(m_ref.shape, NEG, F32)
    l_new = jnp.zeros(l_ref.shape, F32)
    for j, s in enumerate(scores):
        s = s + (last_bias if j == cb - 1 else c_far)
        m_n = jnp.max(s, axis=1, keepdims=True)
        p = jnp.exp(s - m_n)
        m_new = jnp.where(st_lane == c * cb + j, m_n, m_new)
        l_new = jnp.where(st_lane == c * cb + j, jnp.sum(p, axis=1, keepdims=True), l_new)
        probs.append(p.astype(MXU_DTYPE))
    m_ref[...] = jnp.maximum(m_ref[...], m_new)
    l_ref[...] = l_ref[...] + l_new
    for j, pb in enumerate(probs):
        acc = None
        for g in range(PAGES_PER_BLOCK):
            vt = v_pages[PAGES_PER_BLOCK * j + g][0]
            part = lax.dot_general(pb[:, g * PAGE_SIZE:(g + 1) * PAGE_SIZE], vt.astype(MXU_DTYPE), _nt_dims(),
                                   preferred_element_type=F32)
            acc = part if acc is None else acc + part
        acc_ref[c * cb + j] = acc

    @pl.when(c == n_chunks - 1)
    def _():
        lanes = LANES
        nbp = -(-nb_past // SUBLANES) * SUBLANES
        q_pad = jnp.concatenate([qbd, jnp.zeros((lanes - rows, width), F32)], axis=0)
        gT = lax.dot_general(km_ref[...].T, q_pad, _nt_dims(), preferred_element_type=F32,
                             precision=HIGHEST)[:nbp]
        nid = lax.broadcasted_iota(I32, gT.shape, 0)
        cnt = jnp.zeros(gT.shape, I32)
        for m in range(nb_past):
            row = gT[m:m + 1, :]
            beats = (row > gT) | ((row == gT) & (m < nid))
            cnt = cnt + jnp.where(beats, 1, 0)
        selT = jnp.where((cnt < MOBA_TOPK) & (nid < nb_past), 1.0, 0.0)
        selT = jnp.concatenate([selT, jnp.zeros((lanes - nbp, lanes), F32)], axis=0)
        sel = selT.T[:rows] > 0.5
        pad = jnp.zeros((PAGE_SIZE - tq, width), F32)
        kn = jnp.concatenate([kn_ref[...], pad], axis=0)
        vn = jnp.concatenate([vn_ref[...], pad], axis=0)
        qi = lax.broadcasted_iota(I32, (rows, PAGE_SIZE), 0) % tq
        kj = lax.broadcasted_iota(I32, (rows, PAGE_SIZE), 1)
        s = lax.dot_general(qbd_b, kn.astype(MXU_DTYPE), _nt_dims(), preferred_element_type=F32)
        s = jnp.where(kj <= qi, s + bias_rows(qi - kj), NEG)
        m_o = jnp.max(s, axis=1, keepdims=True)
        p = jnp.exp(s - m_o)
        l_o = jnp.sum(p, axis=1, keepdims=True)
        acc_o = jnp.dot(p.astype(MXU_DTYPE), vn.astype(MXU_DTYPE), preferred_element_type=F32)
        m_blk = m_ref[...]
        m_all = jnp.maximum(m_o, jnp.max(jnp.where(sel, m_blk, NEG), axis=1, keepdims=True))
        w_blk = jnp.where(sel, jnp.exp(m_blk - m_all), 0.0)
        w_o = jnp.exp(m_o - m_all)
        l_all = w_o * l_o + jnp.sum(w_blk * l_ref[...], axis=1, keepdims=True)
        acc_all = w_o * acc_o
        for n in range(nb_past):
            acc_all = acc_all + w_blk[:, n:n + 1] * acc_ref[n]
        out_bd = jnp.where(head_mask, acc_all / l_all, 0.0)
        out = out_bd[0:tq]
        for h in range(1, N_HEADS):
            out = out + out_bd[h * tq:(h + 1) * tq]
        o_ref[...] = out


def _moba_sample(page_rows, q, k_new, v_new, tab_rows, cache_k, cache_v, *, nseq, tq, nb_past):
    assert nb_past <= LANES and N_HEADS * tq <= LANES
    cb = math.gcd(SAMPLE_CHUNK_BLOCKS, nb_past)
    npg = cb * PAGES_PER_BLOCK
    n_chunks = nb_past // cb
    ppseq = nb_past * PAGES_PER_BLOCK
    rows = N_HEADS * tq
    width = q.shape[1]
    tok = lambda b, c, pt: (b, 0)

    def page_spec(g):
        return pl.BlockSpec((1, width, PAGE_SIZE), lambda b, c, pt: (pt[b * ppseq + c * npg + g], 0, 0))

    grid_spec = pltpu.PrefetchScalarGridSpec(
        num_scalar_prefetch=1,
        grid=(nseq, n_chunks),
        in_specs=[pl.BlockSpec((tq, width), tok)] * 3
        + [pl.BlockSpec(tab_rows.shape, lambda b, c, pt: (0, 0))]
        + [page_spec(g) for g in range(npg)] * 2,
        out_specs=pl.BlockSpec((tq, width), tok),
        scratch_shapes=[pltpu.VMEM((width, LANES), F32),
                        pltpu.VMEM((rows, LANES), F32),
                        pltpu.VMEM((rows, LANES), F32),
                        pltpu.VMEM((nb_past, rows, width), F32)],
    )
    return pl.pallas_call(
        functools.partial(_moba_sample_kernel, tq=tq, nb_past=nb_past, cb=cb),
        grid_spec=grid_spec,
        out_shape=jax.ShapeDtypeStruct((nseq * tq, width), F32),
        compiler_params=pltpu.CompilerParams(dimension_semantics=("arbitrary",) * 2,
                                             vmem_limit_bytes=VMEM_LIMIT),
        name="moba_sample",
    )(page_rows, q, k_new, v_new, tab_rows, *([cache_k] * npg), *([cache_v] * npg))


def _mix_kernel(*refs, tm, seq_len, attn_transposed, has_state, alpha):
    if has_state:
        (x_ref, attn_ref, u_ref, halo_ref, st0_ref, st1_ref, cb_ref, ga_ref, gb_ref, cw_ref,
         wao_ref, wco_ref, wo_ref, g_ref, b_ref, x1_ref, ubuf) = refs
    else:
        (x_ref, attn_ref, u_ref, halo_ref, cb_ref, ga_ref, gb_ref, cw_ref,
         wao_ref, wco_ref, wo_ref, g_ref, b_ref, x1_ref, ubuf) = refs
    i = pl.program_id(0)
    if attn_transposed:
        attn = attn_ref[...].astype(F32).T.astype(MXU_DTYPE)
    else:
        attn = attn_ref[...].astype(MXU_DTYPE)
    y_att = jnp.dot(attn, wao_ref[...], preferred_element_type=F32)
    u = u_ref[...]
    ubuf[0:SUBLANES, :] = halo_ref[...]
    ubuf[SUBLANES:SUBLANES + tm, :] = u
    p1 = ubuf[SUBLANES - 1:SUBLANES - 1 + tm, :]
    p2 = ubuf[SUBLANES - 2:SUBLANES - 2 + tm, :]
    row = lax.broadcasted_iota(I32, (tm, 1), 0)
    if seq_len % tm == 0:
        pos = (i * tm) % seq_len + row
    else:
        pos = row % seq_len
    s0 = st0_ref[...] if has_state else 0.0
    s1 = st1_ref[...] if has_state else 0.0
    prev1 = jnp.where(pos >= 1, p1, s1)
    prev2 = jnp.where(pos >= 2, p2, jnp.where(pos == 1, s1, s0))
    conv = cw_ref[0:1, :] * prev2 + cw_ref[1:2, :] * prev1 + cw_ref[2:3, :] * u
    y_conv = jnp.dot((cb_ref[...].astype(F32) * conv).astype(MXU_DTYPE), wco_ref[...],
                     preferred_element_type=F32)
    merged = ga_ref[...].astype(F32) * y_att + gb_ref[...].astype(F32) * y_conv
    z = alpha * x_ref[...] + jnp.dot(merged.astype(MXU_DTYPE), wo_ref[...], preferred_element_type=F32)
    x1 = _layer_norm(z, g_ref[...], b_ref[...])
    x1_ref[...] = x1


def _mix(x2d, attn, u, state, cb, ga, gb, conv_w, wao, wco, wo, g, b, *, tm, seq_len, attn_transposed, alpha):
    t, d = x2d.shape
    cw = u.shape[1]
    assert seq_len % tm == 0 or tm % seq_len == 0
    nt = t // tm
    row = lambda i: (i, 0)
    full = lambda a: pl.BlockSpec(a.shape, lambda i: (0,) * a.ndim)
    has_state = state is not None
    hb = tm // SUBLANES
    attn_spec = (pl.BlockSpec((ATT_W, tm), lambda i: (0, i)) if attn_transposed
                 else pl.BlockSpec((tm, ATT_W), row))
    args = [x2d, attn, u, u]
    in_specs = [pl.BlockSpec((tm, d), row), attn_spec, pl.BlockSpec((tm, cw), row),
                pl.BlockSpec((SUBLANES, cw), lambda i: (jnp.maximum(i * hb - 1, 0), 0))]
    if has_state:
        args += list(state)
        in_specs += [pl.BlockSpec((tm, cw), row)] * 2
    args += [cb, ga, gb, conv_w, wao, wco, wo, g, b]
    in_specs += [pl.BlockSpec((tm, cw), row), pl.BlockSpec((tm, d), row), pl.BlockSpec((tm, d), row),
                 full(conv_w), full(wao), full(wco), full(wo), full(g), full(b)]
    return pl.pallas_call(
        functools.partial(_mix_kernel, tm=tm, seq_len=seq_len, attn_transposed=attn_transposed,
                          has_state=has_state, alpha=alpha),
        grid=(nt,),
        in_specs=in_specs,
        out_specs=pl.BlockSpec((tm, d), row),
        out_shape=jax.ShapeDtypeStruct((t, d), F32),
        scratch_shapes=[pltpu.VMEM((SUBLANES + tm, cw), F32)],
        compiler_params=pltpu.CompilerParams(dimension_semantics=("arbitrary",),
                                             vmem_limit_bytes=VMEM_LIMIT),
        name="mix",
    )(*args)


def _router_kernel(xp_ref, xs_ref, wT_ref, b_ref, idx_ref, gate_ref, rank_ref, cnt_ref, seen_ref, *, ntp):
    i = pl.program_id(0)
    ne, tm = wT_ref.shape[0], xp_ref.shape[0]
    x = jnp.where(i < ntp, xp_ref[...], xs_ref[...])

    @pl.when(i == 0)
    def _():
        seen_ref[...] = jnp.zeros(seen_ref.shape, F32)

    def halves(a):
        hi = a.astype(MXU_DTYPE)
        return hi, (a - hi.astype(F32)).astype(MXU_DTYPE)

    def nt(a, b):
        return lax.dot_general(a, b, _nt_dims(), preferred_element_type=F32)

    (w_hi, w_lo), (x_hi, x_lo) = halves(wT_ref[...]), halves(x)
    logits = nt(w_hi, x_hi) + nt(w_hi, x_lo) + nt(w_lo, x_hi) + b_ref[...]
    eid = lax.broadcasted_iota(I32, (ne, tm), 0)
    cnt = jnp.zeros((ne, tm), I32)
    for e in range(ne):
        row = logits[e:e + 1, :]
        beats = (row > logits) | ((row == logits) & (e < eid))
        cnt = cnt + jnp.where(beats, 1, 0)
    sel = cnt < TOP_K
    lmax = jnp.max(logits, axis=0, keepdims=True)
    ex = jnp.where(sel, jnp.exp(logits - lmax), 0.0)
    gates = ex / jnp.sum(ex, axis=0, keepdims=True)
    self_ = jnp.where(sel, 1.0, 0.0)
    tt = lax.broadcasted_iota(I32, (tm, tm), 0) < lax.broadcasted_iota(I32, (tm, tm), 1)
    earlier = jnp.where(tt, 1.0, 0.0).astype(MXU_DTYPE)
    seen = seen_ref[:, 0:1]
    rank = jnp.dot(self_.astype(MXU_DTYPE), earlier, preferred_element_type=F32) + seen
    seen = seen + jnp.sum(self_, axis=1, keepdims=True)
    seen_ref[...] = jnp.broadcast_to(seen, seen_ref.shape)
    cnt_ref[...] = jnp.broadcast_to(seen, cnt_ref.shape)
    ee = lax.broadcasted_iota(I32, (ne, ne), 1) < lax.broadcasted_iota(I32, (ne, ne), 0)
    below = jnp.dot(jnp.where(ee, 1.0, 0.0).astype(MXU_DTYPE), self_.astype(MXU_DTYPE),
                    preferred_element_type=F32)
    eidf = eid.astype(F32)
    for kk in range(TOP_K):
        hit = sel & (below == float(kk))
        idx_ref[kk:kk + 1, :] = jnp.sum(jnp.where(hit, eidf, 0.0), axis=0, keepdims=True).astype(I32)
        gate_ref[kk:kk + 1, :] = jnp.sum(jnp.where(hit, gates, 0.0), axis=0, keepdims=True)
        rank_ref[kk:kk + 1, :] = jnp.sum(jnp.where(hit, rank, 0.0), axis=0, keepdims=True).astype(I32)


def _route(x1p, x1s, router_wT, router_b_col, *, tm):
    d = x1p.shape[1]
    ntp, nts = x1p.shape[0] // tm, x1s.shape[0] // tm
    t = (ntp + nts) * tm
    ne = router_wT.shape[0]
    col = lambda i: (0, i)
    return pl.pallas_call(
        functools.partial(_router_kernel, ntp=ntp),
        grid=(ntp + nts,),
        in_specs=[pl.BlockSpec((tm, d), lambda i: (jnp.minimum(i, ntp - 1), 0)),
                  pl.BlockSpec((tm, d), lambda i: (jnp.maximum(i - ntp, 0), 0)),
                  pl.BlockSpec((ne, d), lambda i: (0, 0)),
                  pl.BlockSpec((ne, 1), lambda i: (0, 0))],
        out_specs=[pl.BlockSpec((TOP_K, tm), col), pl.BlockSpec((TOP_K, tm), col),
                   pl.BlockSpec((TOP_K, tm), col), pl.BlockSpec((ne, LANES), lambda i: (0, 0))],
        out_shape=[jax.ShapeDtypeStruct((TOP_K, t), I32), jax.ShapeDtypeStruct((TOP_K, t), F32),
                   jax.ShapeDtypeStruct((TOP_K, t), I32), jax.ShapeDtypeStruct((ne, LANES), F32)],
        scratch_shapes=[pltpu.VMEM((ne, LANES), F32)],
        compiler_params=pltpu.CompilerParams(dimension_semantics=("arbitrary",),
                                             vmem_limit_bytes=VMEM_LIMIT),
        name="router",
    )(x1p, x1s, router_wT, router_b_col)


def _dispatch_kernel(dest_ref, pend_ref, xp_ref, xs_ref, xb_ref, zeros_ref, sem, zsem, *, t, tm, tile, ne, ntp):
    i = pl.program_id(0)

    def row_copy(x_ref, r, dst):
        return pltpu.make_async_copy(x_ref.at[pl.ds(r, 1)], xb_ref.at[pl.ds(dst, 1)], sem)

    def clear_copy(e):
        start = pl.multiple_of(pend_ref[e] - tile, tile)
        return pltpu.make_async_copy(zeros_ref, xb_ref.at[pl.ds(start, tile)], zsem)

    @pl.when(i == 0)
    def _():
        zeros_ref[...] = jnp.zeros(zeros_ref.shape, F32)
        for phase in ("start", "wait"):
            for e in range(ne):
                has_rows = pend_ref[e] > (pend_ref[e - 1] if e else 0)

                @pl.when(has_rows)
                def _():
                    if phase == "start":
                        clear_copy(e).start()
                    else:
                        clear_copy(e).wait()

        def tail_copy(j):
            return pltpu.make_async_copy(zeros_ref, xb_ref.at[pl.ds(pl.multiple_of(j * tile, tile), tile)], zsem)

        first_unused = pend_ref[ne - 1] // tile
        n_tiles = xb_ref.shape[0] // tile
        lax.fori_loop(first_unused, n_tiles, lambda j, c: (tail_copy(j).start(), c)[1], 0)
        lax.fori_loop(first_unused, n_tiles, lambda j, c: (tail_copy(j).wait(), c)[1], 0)

    def issue_from(x_ref):
        def issue(r, carry):
            for kk in range(TOP_K):
                row_copy(x_ref, r, dest_ref[kk * t + i * tm + r]).start(priority=kk % 2)
            return carry

        lax.fori_loop(0, tm, issue, 0, unroll=DISPATCH_UNROLL)

    pl.when(i < ntp)(lambda: issue_from(xp_ref))
    pl.when(i >= ntp)(lambda: issue_from(xs_ref))
    for kk in range(TOP_K):
        pltpu.make_async_copy(xp_ref, xb_ref.at[pl.ds(0, tm)], sem).wait()


def _dispatch(dest_flat, pad_end, x1p, x1s, *, rows, tm, tile):
    d = x1p.shape[1]
    ntp, nts = x1p.shape[0] // tm, x1s.shape[0] // tm
    t = (ntp + nts) * tm
    ne = pad_end.shape[0]
    grid_spec = pltpu.PrefetchScalarGridSpec(
        num_scalar_prefetch=2,
        grid=(ntp + nts,),
        in_specs=[pl.BlockSpec((tm, d), lambda i, dest, pend: (jnp.minimum(i, ntp - 1), 0)),
                  pl.BlockSpec((tm, d), lambda i, dest, pend: (jnp.maximum(i - ntp, 0), 0))],
        out_specs=pl.BlockSpec(memory_space=pl.ANY),
        scratch_shapes=[pltpu.VMEM((tile, d), F32), pltpu.SemaphoreType.DMA, pltpu.SemaphoreType.DMA],
    )
    return pl.pallas_call(
        functools.partial(_dispatch_kernel, t=t, tm=tm, tile=tile, ne=ne, ntp=ntp),
        grid_spec=grid_spec,
        out_shape=jax.ShapeDtypeStruct((rows, d), F32),
        compiler_params=pltpu.CompilerParams(dimension_semantics=("arbitrary",),
                                             vmem_limit_bytes=VMEM_LIMIT),
        name="dispatch",
    )(dest_flat, pad_end, x1p, x1s)


def _expert_kernel(te_ref, nu_ref, x_ref, wgu_ref, bg_ref, bu_ref, wd_ref, bd_ref, y_ref, wg_s, wu_s, wd_s):
    i = pl.program_id(0)
    new_expert = jnp.logical_or(i == 0, te_ref[i] != te_ref[jnp.maximum(i - 1, 0)])

    @pl.when(jnp.logical_and(new_expert, i < nu_ref[0]))
    def _():
        half = LANES
        c = lax.broadcasted_iota(I32, (2 * half, 2 * half), 0)
        o = lax.broadcasted_iota(I32, (2 * half, 2 * half), 1)
        src = jnp.where(o < half, 2 * o, 2 * (o - half) + 1)
        perm = jnp.where(c == src, 1.0, 0.0).astype(MXU_DTYPE)
        for j in range(wgu_ref.shape[2] // (2 * half)):
            wb = wgu_ref[0, :, 2 * half * j:2 * half * (j + 1)].astype(MXU_DTYPE)
            y = jnp.dot(wb, perm, preferred_element_type=F32).astype(MXU_DTYPE)
            wg_s[:, half * j:half * (j + 1)] = y[:, :half]
            wu_s[:, half * j:half * (j + 1)] = y[:, half:]
        wd_s[...] = wd_ref[0].astype(MXU_DTYPE)

    @pl.when(i < nu_ref[0])
    def _():
        x = x_ref[...].astype(MXU_DTYPE)
        hg = jnp.dot(x, wg_s[...], preferred_element_type=F32) + bg_ref[0]
        hu = jnp.dot(x, wu_s[...], preferred_element_type=F32) + bu_ref[0]
        g = jnp.minimum(hg, SWIGLU_LIMIT)
        u = jnp.clip(hu, -SWIGLU_LIMIT, SWIGLU_LIMIT)
        a = g * _sigmoid(SWIGLU_ALPHA * g) * (u + 1.0)
        y_ref[...] = jnp.dot(a.astype(MXU_DTYPE), wd_s[...], preferred_element_type=F32) + bd_ref[0]

    @pl.when(i >= nu_ref[0])
    def _():
        y_ref[...] = jnp.zeros(y_ref.shape, F32)


def _experts(tile_e, n_used, xb, w_gate_up, bg, bu, w_down, bd, *, tm):
    rows, d = xb.shape
    dff = w_down.shape[1]
    nt = rows // tm
    ew = lambda i, te, nu: (te[i], 0, 0)
    grid_spec = pltpu.PrefetchScalarGridSpec(
        num_scalar_prefetch=2,
        grid=(nt,),
        in_specs=[pl.BlockSpec((tm, d), lambda i, te, nu: (jnp.minimum(i, nu[0] - 1), 0)),
                  pl.BlockSpec((1, d, 2 * dff), ew),
                  pl.BlockSpec((1, 1, dff), ew), pl.BlockSpec((1, 1, dff), ew),
                  pl.BlockSpec((1, dff, d), ew), pl.BlockSpec((1, 1, d), ew)],
        out_specs=pl.BlockSpec((tm, d), lambda i, te, nu: (i, 0)),
        scratch_shapes=[pltpu.VMEM((d, dff), MXU_DTYPE), pltpu.VMEM((d, dff), MXU_DTYPE),
                        pltpu.VMEM((dff, d), MXU_DTYPE)],
    )
    return pl.pallas_call(
        _expert_kernel,
        grid_spec=grid_spec,
        out_shape=jax.ShapeDtypeStruct((rows, d), F32),
        compiler_params=pltpu.CompilerParams(dimension_semantics=("arbitrary",),
                                             vmem_limit_bytes=VMEM_LIMIT),
        name="experts",
    )(tile_e, n_used, xb, w_gate_up, bg, bu, w_down, bd)


def _norm2_kernel(x1_ref, gate_ref, *refs, alpha):
    y_refs, (g_ref, b_ref, o_ref) = refs[:TOP_K], refs[TOP_K:]
    gate = gate_ref[...]
    ffn = gate[:, 0:1] * y_refs[0][...]
    for kk in range(1, TOP_K):
        ffn = ffn + gate[:, kk:kk + 1] * y_refs[kk][...]
    o_ref[...] = _layer_norm(alpha * x1_ref[...] + ffn, g_ref[...], b_ref[...])


def _norm2(x1, gate_t, y_slots, g, b, *, row0, tm, alpha):
    nrows, d = x1.shape
    off = row0 // tm
    per_slot = gate_t.shape[0] // tm
    src = lambda i: (i + off, 0)
    slot_src = lambda kk: (lambda i: (i + off + kk * per_slot, 0))
    y_pairs = [y_slots] * TOP_K
    return pl.pallas_call(
        functools.partial(_norm2_kernel, alpha=alpha),
        grid=(nrows // tm,),
        in_specs=[pl.BlockSpec((tm, d), lambda i: (i, 0)), pl.BlockSpec((tm, TOP_K), src)]
        + [pl.BlockSpec((tm, d), slot_src(kk)) for kk in range(TOP_K)]
        + [pl.BlockSpec((1, d), lambda i: (0, 0)), pl.BlockSpec((1, d), lambda i: (0, 0))],
        out_specs=pl.BlockSpec((tm, d), lambda i: (i, 0)),
        out_shape=jax.ShapeDtypeStruct((nrows, d), F32),
        compiler_params=pltpu.CompilerParams(dimension_semantics=("arbitrary",),
                                             vmem_limit_bytes=VMEM_LIMIT),
        name="norm2",
    )(x1, gate_t, *y_pairs, g, b)


def _moe(x1p, x1s, router_w, router_b, w_gate_up, b_gate_up, w_down, b_down):
    t = x1p.shape[0] + x1s.shape[0]
    ne = router_w.shape[1]
    idx, gate, rank, counts = _route(x1p, x1s, router_w.T, router_b.reshape(ne, 1), tm=TOKEN_TILE)
    tm = EXPERT_TILE
    counts = counts[:, 0].astype(I32)
    padded = (counts + tm - 1) // tm * tm
    pad_end = jnp.cumsum(padded)
    pad_start = pad_end - padded
    n_tiles = (t * TOP_K + ne * (tm - 1)) // tm
    rows = n_tiles * tm
    eids = jnp.arange(ne, dtype=I32)
    start_of = jnp.sum(jnp.where(idx[:, :, None] == eids, pad_start, 0), axis=-1)
    dest = start_of + rank
    tile_row = jnp.arange(n_tiles, dtype=I32) * tm
    tile_e = jnp.minimum(jnp.sum((pad_end[None, :] <= tile_row[:, None]).astype(I32), axis=1), ne - 1)
    n_used = (pad_end[-1:] // tm).astype(I32)
    xb = _dispatch(dest.reshape(-1), pad_end.astype(I32), x1p, x1s, rows=rows, tm=TOKEN_TILE, tile=tm)
    bg = b_gate_up[:, None, 0::2]
    bu = b_gate_up[:, None, 1::2]
    yb = _experts(tile_e, n_used, xb, w_gate_up, bg, bu, w_down, b_down[:, None, :], tm=tm)
    return gate.T, yb[dest.reshape(-1)]


def kernel(x_prompt, x_sample, cache_k, cache_v, state_conv, page_table, rel_bias, w_in, conv_w, w_att_o,
           w_conv_o, w_o, ln1_g, ln1_b, router_w, router_b, w_gate_up, b_gate_up, w_down, b_down, ln2_g, ln2_b):
    depth = w_in.shape[0]
    batch, seq, d = x_prompt.shape
    nseq, tq, _ = x_sample.shape
    n_pool = cache_k.shape[1]
    ppseq = page_table.shape[1]
    past = ppseq * PAGE_SIZE
    assert seq % MOBA_BLOCK == 0 and past % MOBA_BLOCK == 0 and tq % SUBLANES == 0 and tq <= PAGE_SIZE
    nb_past = past // MOBA_BLOCK
    tp, ts = batch * seq, nseq * tq
    alpha = (2 * depth) ** 0.25
    cw = conv_w.shape[2]
    tab_rows = jnp.repeat(rel_bias.T, tq, axis=0)
    ck = cache_k.transpose(0, 1, 3, 4, 2).reshape(depth * n_pool, ATT_W, PAGE_SIZE)
    cv = cache_v.transpose(0, 1, 3, 4, 2).reshape(depth * n_pool, ATT_W, PAGE_SIZE)
    hp = x_prompt.reshape(tp, d)
    hs = x_sample.reshape(ts, d)
    outs = [[] for _ in range(6)]
    for l in range(depth):
        w_in_b = w_in[l].astype(MXU_DTYPE)
        wao, wco, wo = (w_att_o[l].astype(MXU_DTYPE), w_conv_o[l].astype(MXU_DTYPE), w_o[l].astype(MXU_DTYPE))
        g1, b1, g2, b2 = ln1_g[l][None], ln1_b[l][None], ln2_g[l][None], ln2_b[l][None]
        qT, khm, vT, kp, vp, km, up, cbp, gap, gbp = _project(hp, w_in_b, tm=PROJ_TILE, head_major=True, seq=seq)
        nbt = PROJ_TILE // MOBA_BLOCK
        kmean = km[:, :nbt].reshape(batch, seq // MOBA_BLOCK, N_HEADS, HEAD_DIM).transpose(0, 2, 1, 3)
        kmean = kmean.reshape(batch * N_HEADS, seq // MOBA_BLOCK, HEAD_DIM)
        attn_p = _moba_prompt(rel_bias, qT, khm, vT, kmean, batch=batch, seq=seq)
        x1p = _mix(hp, attn_p, up, None, cbp, gap, gbp, conv_w[l], wao, wco, wo, g1, b1,
                         tm=PROJ_TILE, seq_len=seq, attn_transposed=True, alpha=alpha)
        qs, ks, vs, us, cbs, gas, gbs = _project(hs, w_in_b, tm=ts, head_major=False, wq=w_in[l][:, :ATT_W])
        pages = (page_table + l * n_pool).reshape(-1).astype(I32)
        attn_s = _moba_sample(pages, qs, ks, vs, tab_rows, ck, cv, nseq=nseq, tq=tq, nb_past=nb_past)
        state = (jnp.repeat(state_conv[l][:, 0], tq, axis=0), jnp.repeat(state_conv[l][:, 1], tq, axis=0))
        x1s = _mix(hs, attn_s, us, state, cbs, gas, gbs, conv_w[l], wao, wco, wo, g1, b1,
                         tm=ts, seq_len=tq, attn_transposed=False, alpha=alpha)
        gate_t, y_pairs = _moe(x1p, x1s, router_w[l], router_b[l], w_gate_up[l], b_gate_up[l], w_down[l], b_down[l])
        hp = _norm2(x1p, gate_t, y_pairs, g2, b2, row0=0, tm=TOKEN_TILE, alpha=alpha)
        hs = _norm2(x1s, gate_t, y_pairs, g2, b2, row0=tp, tm=TOKEN_TILE, alpha=alpha)
        outs[0].append(kp.reshape(batch, N_HEADS, HEAD_DIM, seq).transpose(0, 3, 1, 2))
        outs[1].append(vp.reshape(batch, N_HEADS, HEAD_DIM, seq).transpose(0, 3, 1, 2))
        outs[2].append(up.reshape(batch, seq, cw)[:, seq - (CONV_K - 1):])
        outs[3].append(ks.reshape(nseq, tq, N_HEADS, HEAD_DIM))
        outs[4].append(vs.reshape(nseq, tq, N_HEADS, HEAD_DIM))
        outs[5].append(us.reshape(nseq, tq, cw)[:, tq - (CONV_K - 1):])
    stacked = [jnp.stack(o) for o in outs]
    return (hp.reshape(batch, seq, d), hs.reshape(nseq, tq, d), *stacked)
```

```python
import functools
import math

import jax
import jax.numpy as jnp
from jax import lax
from jax.experimental import pallas as pl
from jax.experimental.pallas import tpu as pltpu

F32 = jnp.float32
I32 = jnp.int32
MXU_DTYPE = jnp.bfloat16
HIGHEST = lax.Precision.HIGHEST

N_HEADS = 8
HEAD_DIM = 64
ATT_W = N_HEADS * HEAD_DIM
MOBA_BLOCK = 256
MOBA_TOPK = 3
PAGE_SIZE = 128
PAGES_PER_BLOCK = MOBA_BLOCK // PAGE_SIZE
CONV_K = 3
N_EXPERTS = 32
TOP_K = 4
SWIGLU_LIMIT = 7.0
SWIGLU_ALPHA = 1.702
NUM_BUCKETS = 32
NUM_EXACT = NUM_BUCKETS // 2
MAX_DISTANCE = 128
LN_EPS = 1e-5
SCORE_SCALE = HEAD_DIM ** -0.5
NEG = -1e30

SUBLANES = 8
LANES = 128
PACKED_ROWS = 16
V_ROWS = HEAD_DIM + PACKED_ROWS

PROJ_TILE = 512
TOKEN_TILE = 256
EXPERT_TILE = 512
DISPATCH_UNROLL = 32
PROMPT_ROUND_UNROLL = 8
SAMPLE_CHUNK_BLOCKS = 16
VMEM_LIMIT = 56 * 1024 * 1024

assert MOBA_BLOCK >= MAX_DISTANCE


def _nt_dims():
    return (((1,), (1,)), ((), ()))


def _sigmoid(x):
    return 1.0 / (1.0 + jnp.exp(-x))


def _bucket(dist):
    n = jnp.maximum(dist, 0)
    nf = jnp.maximum(n, 1).astype(F32)
    large = NUM_EXACT + (jnp.log(nf / NUM_EXACT) / math.log(MAX_DISTANCE / NUM_EXACT)
                         * (NUM_BUCKETS - NUM_EXACT)).astype(I32)
    large = jnp.minimum(large, NUM_BUCKETS - 1)
    return jnp.where(n < NUM_EXACT, n, large)


def _layer_norm(z, g, b):
    mu = jnp.mean(z, axis=-1, keepdims=True)
    zc = z - mu
    var = jnp.mean(zc * zc, axis=-1, keepdims=True)
    return zc * lax.rsqrt(var + LN_EPS) * g + b


def _proj_kernel(x_ref, w_ref, *refs, tm, head_major, blocks_per_seq):
    xb = x_ref[...].astype(MXU_DTYPE)

    def mm(c0, width):
        return jnp.dot(xb, w_ref[:, c0:c0 + width], preferred_element_type=F32)

    if head_major:
        qT_ref, khm_ref, vT_ref, k_ref, v_ref, km_ref, u_ref, cb_ref, ga_ref, gb_ref = refs
        q = mm(0, ATT_W) * SCORE_SCALE
    else:
        wq_ref, q_ref, k_ref, v_ref, u_ref, cb_ref, ga_ref, gb_ref = refs
        q = jnp.dot(x_ref[...], wq_ref[...], preferred_element_type=F32, precision=HIGHEST) * SCORE_SCALE
    d_model = x_ref.shape[1]
    k = mm(ATT_W, ATT_W)
    v = mm(2 * ATT_W, ATT_W)
    if head_major:
        qT_ref[...] = q.T
        k_ref[0] = k.T
        vt = v.T
        v_ref[0] = vt
        km_ref[...] = jnp.zeros(km_ref.shape, F32)
        lane = lax.broadcasted_iota(I32, (MOBA_BLOCK, 2 * HEAD_DIM), 1)
        ones_rows = jnp.ones((V_ROWS - HEAD_DIM, MOBA_BLOCK), MXU_DTYPE)
        for r in range(tm // MOBA_BLOCK):
            kr = k[r * MOBA_BLOCK:(r + 1) * MOBA_BLOCK]
            km_ref[0, r:r + 1, :] = jnp.sum(kr, axis=0, keepdims=True) * (1.0 / MOBA_BLOCK)
            n_blk = (pl.program_id(0) * (tm // MOBA_BLOCK) + r) % blocks_per_seq
            tag = jnp.where(lane - HEAD_DIM == n_blk, 1.0, 0.0)
            vtr = vt[:, r * MOBA_BLOCK:(r + 1) * MOBA_BLOCK]
            for h in range(N_HEADS):
                pair = kr[:, (h // 2) * 2 * HEAD_DIM:(h // 2 + 1) * 2 * HEAD_DIM]
                if h % 2:
                    pair = pltpu.roll(pair, HEAD_DIM, axis=1)
                khm_ref[h, r] = jnp.where(lane < HEAD_DIM, pair, tag).astype(MXU_DTYPE)
                vT_ref[r, h * V_ROWS:h * V_ROWS + HEAD_DIM] = vtr[h * HEAD_DIM:(h + 1) * HEAD_DIM].astype(MXU_DTYPE)
                vT_ref[r, h * V_ROWS + HEAD_DIM:(h + 1) * V_ROWS] = ones_rows
    else:
        q_ref[...] = q
        k_ref[...] = k
        v_ref[...] = v
    c0 = 3 * ATT_W
    cw = u_ref.shape[1]
    cb_ref[...] = mm(c0, cw).astype(cb_ref.dtype)
    u_ref[...] = mm(c0 + cw, cw) * mm(c0 + 2 * cw, cw)
    ga_ref[...] = _sigmoid(mm(c0 + 3 * cw, d_model)).astype(ga_ref.dtype)
    gb_ref[...] = _sigmoid(mm(c0 + 3 * cw + d_model, d_model)).astype(gb_ref.dtype)


def _project(x2d, w_in_b, *, tm, head_major, seq=None, wq=None):
    t, d = x2d.shape
    cw = (w_in_b.shape[1] - 3 * ATT_W - 2 * d) // 3
    nt = t // tm
    row = lambda i: (i, 0)
    f32s = lambda shape: jax.ShapeDtypeStruct(shape, F32)
    mxs = lambda shape: jax.ShapeDtypeStruct(shape, MXU_DTYPE)
    tail_shapes = [f32s((t, cw)), mxs((t, cw)), mxs((t, d)), mxs((t, d))]
    tail_specs = [pl.BlockSpec((tm, cw), row), pl.BlockSpec((tm, cw), row),
                  pl.BlockSpec((tm, d), row), pl.BlockSpec((tm, d), row)]
    if head_major:
        nbt = tm // MOBA_BLOCK
        tps = seq // tm
        seq_t = pl.BlockSpec((1, ATT_W, tm), lambda i: (i // tps, 0, i % tps))
        out_shape = [f32s((ATT_W, t)), mxs((N_HEADS, t // MOBA_BLOCK, MOBA_BLOCK, 2 * HEAD_DIM)),
                     mxs((t // MOBA_BLOCK, N_HEADS * V_ROWS, MOBA_BLOCK)), f32s((t // seq, ATT_W, seq)),
                     f32s((t // seq, ATT_W, seq)), f32s((nt, SUBLANES, ATT_W))] + tail_shapes
        out_specs = [pl.BlockSpec((ATT_W, tm), lambda i: (0, i)),
                     pl.BlockSpec((N_HEADS, nbt, MOBA_BLOCK, 2 * HEAD_DIM), lambda i: (0, i, 0, 0)),
                     pl.BlockSpec((nbt, N_HEADS * V_ROWS, MOBA_BLOCK), lambda i: (i, 0, 0)),
                     seq_t, seq_t,
                     pl.BlockSpec((1, SUBLANES, ATT_W), lambda i: (i, 0, 0))] + tail_specs
        args, extra_specs = (x2d, w_in_b), []
    else:
        out_shape = [f32s((t, ATT_W))] * 3 + tail_shapes
        out_specs = [pl.BlockSpec((tm, ATT_W), row)] * 3 + tail_specs
        args, extra_specs = (x2d, w_in_b, wq), [pl.BlockSpec(wq.shape, lambda i: (0, 0))]
    return pl.pallas_call(
        functools.partial(_proj_kernel, tm=tm, head_major=head_major,
                          blocks_per_seq=seq // MOBA_BLOCK if head_major else None),
        grid=(nt,),
        in_specs=[pl.BlockSpec((tm, d), row), pl.BlockSpec(w_in_b.shape, lambda i: (0, 0))] + extra_specs,
        out_specs=out_specs,
        out_shape=out_shape,
        compiler_params=pltpu.CompilerParams(dimension_semantics=("arbitrary",),
                                             vmem_limit_bytes=VMEM_LIMIT),
        name="proj",
    )(*args)


def _moba_prompt_kernel(tab_ref, qT_ref, k_ref, vT_ref, km_ref, o_ref, bias_ref, qb_ref, *state):
    h = pl.program_id(0)
    first_sequence = pl.program_id(1) == 0
    nb = km_ref.shape[1]
    blk = MOBA_BLOCK
    n_streams = nb // 2
    m_refs, acc_refs = state[:n_streams], state[n_streams:]

    @pl.when(first_sequence)
    def _():
        kk = lax.broadcasted_iota(I32, (blk, blk), 0)
        qq = lax.broadcasted_iota(I32, (blk, blk), 1)
        for age in range(2):
            dist = qq - kk + age * blk
            bucket = _bucket(dist)
            tile = jnp.zeros((blk, blk), F32)
            for b in range(NUM_BUCKETS):
                tile = jnp.where(bucket == b, tab_ref[b, h], tile)
            bias_ref[age] = jnp.where(dist >= 0, tile, NEG)
        bias_ref[2] = jnp.full((blk, blk), tab_ref[NUM_BUCKETS - 1, h], F32)

    km = km_ref[0]
    nid = lax.broadcasted_iota(I32, (nb, blk), 0)
    spare = jnp.zeros((HEAD_DIM - nb, blk), F32)
    for i in range(nb):
        qT = qT_ref[:, i * blk:(i + 1) * blk]
        gT = jnp.dot(km, qT, preferred_element_type=F32, precision=HIGHEST)
        cnt = jnp.zeros((nb, blk), I32)
        for m in range(i):
            row = gT[m:m + 1, :]
            beats = (row > gT) | ((row == gT) & (m < nid))
            cnt = cnt + jnp.where(beats, 1, 0)
        keep = ((cnt < MOBA_TOPK) & (nid < i)) | (nid == i)
        qb_ref[i] = jnp.concatenate([qT, jnp.where(keep, 0.0, NEG), spare], axis=0).astype(MXU_DTYPE)

    for a in range(n_streams):
        m_refs[a][...] = jnp.full(m_refs[a].shape, NEG, F32)
        acc_refs[a][...] = jnp.zeros(acc_refs[a].shape, F32)

    def sweep_round(t, carry):
        staged = []
        for a in range(n_streams):
            first = t <= a
            slot = jnp.where(first, 0, 1)
            i = jnp.where(first, a, nb - 1 - a)
            n = jnp.where(first, a - t, nb - t)
            age = jnp.minimum(i - n, 2)
            s = jnp.dot(k_ref[0, n], qb_ref[i], preferred_element_type=F32) + bias_ref[age]
            staged.append((slot, n, s))
        updates = []
        for a, (slot, n, s) in enumerate(staged):
            m_old = m_refs[a][slot]
            m_new = jnp.maximum(m_old, jnp.max(s, axis=0, keepdims=True))
            alpha = jnp.exp(m_old - m_new)
            p = jnp.exp(s - m_new)
            acc_new = alpha * acc_refs[a][slot] + jnp.dot(
                vT_ref[n], p.astype(MXU_DTYPE), preferred_element_type=F32)
            updates.append((slot, m_new, acc_new))
        for a, (slot, m_new, acc_new) in enumerate(updates):
            m_refs[a][slot] = m_new
            acc_refs[a][slot] = acc_new
        return carry

    lax.fori_loop(0, nb + 1, sweep_round, 0, unroll=PROMPT_ROUND_UNROLL)
    for a in range(n_streams):
        for slot, i in ((0, a), (1, nb - 1 - a)):
            acc = acc_refs[a][slot]
            o_ref[:, i * blk:(i + 1) * blk] = (acc[:HEAD_DIM] / acc[HEAD_DIM:HEAD_DIM + 1]).astype(o_ref.dtype)


def _moba_prompt(rel_bias, qT, khm, vT, kmean, *, batch, seq):
    nb = seq // MOBA_BLOCK
    assert nb % 2 == 0 and nb <= HEAD_DIM
    t = batch * seq
    stream = lambda shape: [pltpu.VMEM((2,) + shape, F32)] * (nb // 2)
    return pl.pallas_call(
        _moba_prompt_kernel,
        grid=(N_HEADS, batch),
        in_specs=[
            pl.BlockSpec(memory_space=pltpu.SMEM),
            pl.BlockSpec((HEAD_DIM, seq), lambda h, b: (h, b)),
            pl.BlockSpec((1, nb, MOBA_BLOCK, 2 * HEAD_DIM), lambda h, b: (h, b, 0, 0)),
            pl.BlockSpec((nb, V_ROWS, MOBA_BLOCK), lambda h, b: (b, h, 0)),
            pl.BlockSpec((1, nb, HEAD_DIM), lambda h, b: (b * N_HEADS + h, 0, 0)),
        ],
        out_specs=pl.BlockSpec((HEAD_DIM, seq), lambda h, b: (h, b)),
        out_shape=jax.ShapeDtypeStruct((ATT_W, t), MXU_DTYPE),
        scratch_shapes=[pltpu.VMEM((3, MOBA_BLOCK, MOBA_BLOCK), F32),
                        pltpu.VMEM((nb, 2 * HEAD_DIM, MOBA_BLOCK), MXU_DTYPE)]
        + stream((1, MOBA_BLOCK)) + stream((V_ROWS, MOBA_BLOCK)),
        compiler_params=pltpu.CompilerParams(dimension_semantics=("arbitrary",) * 2,
                                             vmem_limit_bytes=VMEM_LIMIT),
        name="moba_prompt",
    )(rel_bias, qT, khm, vT, kmean)


def _moba_sample_kernel(pt_ref, q_ref, kn_ref, vn_ref, tab_ref, *refs, tq, nb_past, cb):
    del pt_ref
    npg = cb * PAGES_PER_BLOCK
    k_pages = refs[:npg]
    v_pages = refs[npg:2 * npg]
    o_ref, km_ref, m_ref, l_ref, acc_ref = refs[2 * npg:]
    c = pl.program_id(1)
    n_chunks = nb_past // cb
    rows = N_HEADS * tq
    width = q_ref.shape[1]
    r_id = lax.broadcasted_iota(I32, (rows, width), 0)
    lane = lax.broadcasted_iota(I32, (rows, width), 1)
    head_mask = (lane // HEAD_DIM) == (r_id // tq)
    q_rows = jnp.concatenate([q_ref[...]] * N_HEADS, axis=0)
    qbd = jnp.where(head_mask, q_rows, 0.0)
    qbd_b = qbd.astype(MXU_DTYPE)
    tab = tab_ref[...]

    def bias_rows(dist):
        bucket = _bucket(dist)
        out = jnp.zeros(dist.shape, F32)
        for b in range(NUM_BUCKETS):
            out = jnp.where(bucket == b, tab[:, b:b + 1], out)
        return out

    c_far = tab[:, NUM_BUCKETS - 1:NUM_BUCKETS]

    @pl.when(c == 0)
    def _():
        km_ref[...] = jnp.zeros(km_ref.shape, F32)
        m_ref[...] = jnp.full(m_ref.shape, NEG, F32)
        l_ref[...] = jnp.zeros(l_ref.shape, F32)

    km_lane = lax.broadcasted_iota(I32, km_ref.shape, 1)
    st_lane = lax.broadcasted_iota(I32, m_ref.shape, 1)
    qi = lax.broadcasted_iota(I32, (rows, MOBA_BLOCK), 0) % tq
    kj = lax.broadcasted_iota(I32, (rows, MOBA_BLOCK), 1)
    last_bias = lax.cond(c == n_chunks - 1, lambda: bias_rows(MOBA_BLOCK + qi - kj),
                         lambda: jnp.broadcast_to(c_far, (rows, MOBA_BLOCK)))
    scores = []
    km_new = jnp.zeros(km_ref.shape, F32)
    for j in range(cb):
        kts = [k_pages[PAGES_PER_BLOCK * j + g][0] for g in range(PAGES_PER_BLOCK)]
        ksum = kts[0]
        for kt in kts[1:]:
            ksum = ksum + kt
        kmean = jnp.sum(ksum, axis=1, keepdims=True) * (1.0 / MOBA_BLOCK)
        km_new = jnp.where(km_lane == c * cb + j, kmean, km_new)
        scores.append(jnp.concatenate(
            [jnp.dot(qbd_b, kt.astype(MXU_DTYPE), preferred_element_type=F32) for kt in kts], axis=1))
    km_ref[...] = km_ref[...] + km_new
    probs = []
    m_new = jnp.full(m_ref.shape, NEG, F32)
    l_new = jnp.zeros(l_ref.shape, F32)
    for j, s in enumerate(scores):
        s = s + (last_bias if j == cb - 1 else c_far)
        m_n = jnp.max(s, axis=1, keepdims=True)
        p = jnp.exp(s - m_n)
        m_new = jnp.where(st_lane == c * cb + j, m_n, m_new)
        l_new = jnp.where(st_lane == c * cb + j, jnp.sum(p, axis=1, keepdims=True), l_new)
        probs.append(p.astype(MXU_DTYPE))
    m_ref[...] = jnp.maximum(m_ref[...], m_new)
    l_ref[...] = l_ref[...] + l_new
    for j, pb in enumerate(probs):
        acc = None
        for g in range(PAGES_PER_BLOCK):
            vt = v_pages[PAGES_PER_BLOCK * j + g][0]
            part = lax.dot_general(pb[:, g * PAGE_SIZE:(g + 1) * PAGE_SIZE], vt.astype(MXU_DTYPE), _nt_dims(),
                                   preferred_element_type=F32)
            acc = part if acc is None else acc + part
        acc_ref[c * cb + j] = acc

    @pl.when(c == n_chunks - 1)
    def _():
        lanes = LANES
        nbp = -(-nb_past // SUBLANES) * SUBLANES
        q_pad = jnp.concatenate([qbd, jnp.zeros((lanes - rows, width), F32)], axis=0)
        gT = lax.dot_general(km_ref[...].T, q_pad, _nt_dims(), preferred_element_type=F32,
                             precision=HIGHEST)[:nbp]
        nid = lax.broadcasted_iota(I32, gT.shape, 0)
        cnt = jnp.zeros(gT.shape, I32)
        for m in range(nb_past):
            row = gT[m:m + 1, :]
            beats = (row > gT) | ((row == gT) & (m < nid))
            cnt = cnt + jnp.where(beats, 1, 0)
        selT = jnp.where((cnt < MOBA_TOPK) & (nid < nb_past), 1.0, 0.0)
        selT = jnp.concatenate([selT, jnp.zeros((lanes - nbp, lanes), F32)], axis=0)
        sel = selT.T[:rows] > 0.5
        pad = jnp.zeros((PAGE_SIZE - tq, width), F32)
        kn = jnp.concatenate([kn_ref[...], pad], axis=0)
        vn = jnp.concatenate([vn_ref[...], pad], axis=0)
        qi = lax.broadcasted_iota(I32, (rows, PAGE_SIZE), 0) % tq
        kj = lax.broadcasted_iota(I32, (rows, PAGE_SIZE), 1)
        s = lax.dot_general(qbd_b, kn.astype(MXU_DTYPE), _nt_dims(), preferred_element_type=F32)
        s = jnp.where(kj <= qi, s + bias_rows(qi - kj), NEG)
        m_o = jnp.max(s, axis=1, keepdims=True)
        p = jnp.exp(s - m_o)
        l_o = jnp.sum(p, axis=1, keepdims=True)
        acc_o = jnp.dot(p.astype(MXU_DTYPE), vn.astype(MXU_DTYPE), preferred_element_type=F32)
        m_blk = m_ref[...]
        m_all = jnp.maximum(m_o, jnp.max(jnp.where(sel, m_blk, NEG), axis=1, keepdims=True))
        w_blk = jnp.where(sel, jnp.exp(m_blk - m_all), 0.0)
        w_o = jnp.exp(m_o - m_all)
        l_all = w_o * l_o + jnp.sum(w_blk * l_ref[...], axis=1, keepdims=True)
        acc_all = w_o * acc_o
        for n in range(nb_past):
            acc_all = acc_all + w_blk[:, n:n + 1] * acc_ref[n]
        out_bd = jnp.where(head_mask, acc_all / l_all, 0.0)
        out = out_bd[0:tq]
        for h in range(1, N_HEADS):
            out = out + out_bd[h * tq:(h + 1) * tq]
        o_ref[...] = out


def _moba_sample(page_rows, q, k_new, v_new, tab_rows, cache_k, cache_v, *, nseq, tq, nb_past):
    assert nb_past <= LANES and N_HEADS * tq <= LANES
    cb = math.gcd(SAMPLE_CHUNK_BLOCKS, nb_past)
    npg = cb * PAGES_PER_BLOCK
    n_chunks = nb_past // cb
    ppseq = nb_past * PAGES_PER_BLOCK
    rows = N_HEADS * tq
    width = q.shape[1]
    tok = lambda b, c, pt: (b, 0)

    def page_spec(g):
        return pl.BlockSpec((1, width, PAGE_SIZE), lambda b, c, pt: (pt[b * ppseq + c * npg + g], 0, 0))

    grid_spec = pltpu.PrefetchScalarGridSpec(
        num_scalar_prefetch=1,
        grid=(nseq, n_chunks),
        in_specs=[pl.BlockSpec((tq, width), tok)] * 3
        + [pl.BlockSpec(tab_rows.shape, lambda b, c, pt: (0, 0))]
        + [page_spec(g) for g in range(npg)] * 2,
        out_specs=pl.BlockSpec((tq, width), tok),
        scratch_shapes=[pltpu.VMEM((width, LANES), F32),
                        pltpu.VMEM((rows, LANES), F32),
                        pltpu.VMEM((rows, LANES), F32),
                        pltpu.VMEM((nb_past, rows, width), F32)],
    )
    return pl.pallas_call(
        functools.partial(_moba_sample_kernel, tq=tq, nb_past=nb_past, cb=cb),
        grid_spec=grid_spec,
        out_shape=jax.ShapeDtypeStruct((nseq * tq, width), F32),
        compiler_params=pltpu.CompilerParams(dimension_semantics=("arbitrary",) * 2,
                                             vmem_limit_bytes=VMEM_LIMIT),
        name="moba_sample",
    )(page_rows, q, k_new, v_new, tab_rows, *([cache_k] * npg), *([cache_v] * npg))


def _mix_kernel(*refs, tm, seq_len, attn_transposed, has_state, alpha):
    if has_state:
        (x_ref, attn_ref, u_ref, halo_ref, st0_ref, st1_ref, cb_ref, ga_ref, gb_ref, cw_ref,
         wao_ref, wco_ref, wo_ref, g_ref, b_ref, x1_ref, ubuf) = refs
    else:
        (x_ref, attn_ref, u_ref, halo_ref, cb_ref, ga_ref, gb_ref, cw_ref,
         wao_ref, wco_ref, wo_ref, g_ref, b_ref, x1_ref, ubuf) = refs
    i = pl.program_id(0)
    if attn_transposed:
        attn = attn_ref[...].astype(F32).T.astype(MXU_DTYPE)
    else:
        attn = attn_ref[...].astype(MXU_DTYPE)
    y_att = jnp.dot(attn, wao_ref[...], preferred_element_type=F32)
    u = u_ref[...]
    ubuf[0:SUBLANES, :] = halo_ref[...]
    ubuf[SUBLANES:SUBLANES + tm, :] = u
    p1 = ubuf[SUBLANES - 1:SUBLANES - 1 + tm, :]
    p2 = ubuf[SUBLANES - 2:SUBLANES - 2 + tm, :]
    row = lax.broadcasted_iota(I32, (tm, 1), 0)
    if seq_len % tm == 0:
        pos = (i * tm) % seq_len + row
    else:
        pos = row % seq_len
    s0 = st0_ref[...] if has_state else 0.0
    s1 = st1_ref[...] if has_state else 0.0
    prev1 = jnp.where(pos >= 1, p1, s1)
    prev2 = jnp.where(pos >= 2, p2, jnp.where(pos == 1, s1, s0))
    conv = cw_ref[0:1, :] * prev2 + cw_ref[1:2, :] * prev1 + cw_ref[2:3, :] * u
    y_conv = jnp.dot((cb_ref[...].astype(F32) * conv).astype(MXU_DTYPE), wco_ref[...],
                     preferred_element_type=F32)
    merged = ga_ref[...].astype(F32) * y_att + gb_ref[...].astype(F32) * y_conv
    z = alpha * x_ref[...] + jnp.dot(merged.astype(MXU_DTYPE), wo_ref[...], preferred_element_type=F32)
    x1 = _layer_norm(z, g_ref[...], b_ref[...])
    x1_ref[...] = x1


def _mix(x2d, attn, u, state, cb, ga, gb, conv_w, wao, wco, wo, g, b, *, tm, seq_len, attn_transposed, alpha):
    t, d = x2d.shape
    cw = u.shape[1]
    assert seq_len % tm == 0 or tm % seq_len == 0
    nt = t // tm
    row = lambda i: (i, 0)
    full = lambda a: pl.BlockSpec(a.shape, lambda i: (0,) * a.ndim)
    has_state = state is not None
    hb = tm // SUBLANES
    attn_spec = (pl.BlockSpec((ATT_W, tm), lambda i: (0, i)) if attn_transposed
                 else pl.BlockSpec((tm, ATT_W), row))
    args = [x2d, attn, u, u]
    in_specs = [pl.BlockSpec((tm, d), row), attn_spec, pl.BlockSpec((tm, cw), row),
                pl.BlockSpec((SUBLANES, cw), lambda i: (jnp.maximum(i * hb - 1, 0), 0))]
    if has_state:
        args += list(state)
        in_specs += [pl.BlockSpec((tm, cw), row)] * 2
    args += [cb, ga, gb, conv_w, wao, wco, wo, g, b]
    in_specs += [pl.BlockSpec((tm, cw), row), pl.BlockSpec((tm, d), row), pl.BlockSpec((tm, d), row),
                 full(conv_w), full(wao), full(wco), full(wo), full(g), full(b)]
    return pl.pallas_call(
        functools.partial(_mix_kernel, tm=tm, seq_len=seq_len, attn_transposed=attn_transposed,
                          has_state=has_state, alpha=alpha),
        grid=(nt,),
        in_specs=in_specs,
        out_specs=pl.BlockSpec((tm, d), row),
        out_shape=jax.ShapeDtypeStruct((t, d), F32),
        scratch_shapes=[pltpu.VMEM((SUBLANES + tm, cw), F32)],
        compiler_params=pltpu.CompilerParams(dimension_semantics=("arbitrary",),
                                             vmem_limit_bytes=VMEM_LIMIT),
        name="mix",
    )(*args)


def _router_kernel(xp_ref, xs_ref, wT_ref, b_ref, idx_ref, gate_ref, rank_ref, cnt_ref, seen_ref, *, ntp):
    i = pl.program_id(0)
    ne, tm = wT_ref.shape[0], xp_ref.shape[0]
    x = jnp.where(i < ntp, xp_ref[...], xs_ref[...])

    @pl.when(i == 0)
    def _():
        seen_ref[...] = jnp.zeros(seen_ref.shape, F32)

    def halves(a):
        hi = a.astype(MXU_DTYPE)
        return hi, (a - hi.astype(F32)).astype(MXU_DTYPE)

    def nt(a, b):
        return lax.dot_general(a, b, _nt_dims(), preferred_element_type=F32)

    (w_hi, w_lo), (x_hi, x_lo) = halves(wT_ref[...]), halves(x)
    logits = nt(w_hi, x_hi) + nt(w_hi, x_lo) + nt(w_lo, x_hi) + b_ref[...]
    eid = lax.broadcasted_iota(I32, (ne, tm), 0)
    cnt = jnp.zeros((ne, tm), I32)
    for e in range(ne):
        row = logits[e:e + 1, :]
        beats = (row > logits) | ((row == logits) & (e < eid))
        cnt = cnt + jnp.where(beats, 1, 0)
    sel = cnt < TOP_K
    lmax = jnp.max(logits, axis=0, keepdims=True)
    ex = jnp.where(sel, jnp.exp(logits - lmax), 0.0)
    gates = ex / jnp.sum(ex, axis=0, keepdims=True)
    self_ = jnp.where(sel, 1.0, 0.0)
    tt = lax.broadcasted_iota(I32, (tm, tm), 0) < lax.broadcasted_iota(I32, (tm, tm), 1)
    earlier = jnp.where(tt, 1.0, 0.0).astype(MXU_DTYPE)
    seen = seen_ref[:, 0:1]
    rank = jnp.dot(self_.astype(MXU_DTYPE), earlier, preferred_element_type=F32) + seen
    seen = seen + jnp.sum(self_, axis=1, keepdims=True)
    seen_ref[...] = jnp.broadcast_to(seen, seen_ref.shape)
    cnt_ref[...] = jnp.broadcast_to(seen, cnt_ref.shape)
    ee = lax.broadcasted_iota(I32, (ne, ne), 1) < lax.broadcasted_iota(I32, (ne, ne), 0)
    below = jnp.dot(jnp.where(ee, 1.0, 0.0).astype(MXU_DTYPE), self_.astype(MXU_DTYPE),
                    preferred_element_type=F32)
    eidf = eid.astype(F32)
    for kk in range(TOP_K):
        hit = sel & (below == float(kk))
        idx_ref[kk:kk + 1, :] = jnp.sum(jnp.where(hit, eidf, 0.0), axis=0, keepdims=True).astype(I32)
        gate_ref[kk:kk + 1, :] = jnp.sum(jnp.where(hit, gates, 0.0), axis=0, keepdims=True)
        rank_ref[kk:kk + 1, :] = jnp.sum(jnp.where(hit, rank, 0.0), axis=0, keepdims=True).astype(I32)


def _route(x1p, x1s, router_wT, router_b_col, *, tm):
    d = x1p.shape[1]
    ntp, nts = x1p.shape[0] // tm, x1s.shape[0] // tm
    t = (ntp + nts) * tm
    ne = router_wT.shape[0]
    col = lambda i: (0, i)
    return pl.pallas_call(
        functools.partial(_router_kernel, ntp=ntp),
        grid=(ntp + nts,),
        in_specs=[pl.BlockSpec((tm, d), lambda i: (jnp.minimum(i, ntp - 1), 0)),
                  pl.BlockSpec((tm, d), lambda i: (jnp.maximum(i - ntp, 0), 0)),
                  pl.BlockSpec((ne, d), lambda i: (0, 0)),
                  pl.BlockSpec((ne, 1), lambda i: (0, 0))],
        out_specs=[pl.BlockSpec((TOP_K, tm), col), pl.BlockSpec((TOP_K, tm), col),
                   pl.BlockSpec((TOP_K, tm), col), pl.BlockSpec((ne, LANES), lambda i: (0, 0))],
        out_shape=[jax.ShapeDtypeStruct((TOP_K, t), I32), jax.ShapeDtypeStruct((TOP_K, t), F32),
                   jax.ShapeDtypeStruct((TOP_K, t), I32), jax.ShapeDtypeStruct((ne, LANES), F32)],
        scratch_shapes=[pltpu.VMEM((ne, LANES), F32)],
        compiler_params=pltpu.CompilerParams(dimension_semantics=("arbitrary",),
                                             vmem_limit_bytes=VMEM_LIMIT),
        name="router",
    )(x1p, x1s, router_wT, router_b_col)


def _dispatch_kernel(dest_ref, pend_ref, xp_ref, xs_ref, xb_ref, zeros_ref, sem, zsem, *, t, tm, tile, ne, ntp):
    i = pl.program_id(0)

    def row_copy(x_ref, r, dst):
        return pltpu.make_async_copy(x_ref.at[pl.ds(r, 1)], xb_ref.at[pl.ds(dst, 1)], sem)

    def clear_copy(e):
        start = pl.multiple_of(pend_ref[e] - tile, tile)
        return pltpu.make_async_copy(zeros_ref, xb_ref.at[pl.ds(start, tile)], zsem)

    @pl.when(i == 0)
    def _():
        zeros_ref[...] = jnp.zeros(zeros_ref.shape, F32)
        for phase in ("start", "wait"):
            for e in range(ne):
                has_rows = pend_ref[e] > (pend_ref[e - 1] if e else 0)

                @pl.when(has_rows)
                def _():
                    if phase == "start":
                        clear_copy(e).start()
                    else:
                        clear_copy(e).wait()

        def tail_copy(j):
            return pltpu.make_async_copy(zeros_ref, xb_ref.at[pl.ds(pl.multiple_of(j * tile, tile), tile)], zsem)

        first_unused = pend_ref[ne - 1] // tile
        n_tiles = xb_ref.shape[0] // tile
        lax.fori_loop(first_unused, n_tiles, lambda j, c: (tail_copy(j).start(), c)[1], 0)
        lax.fori_loop(first_unused, n_tiles, lambda j, c: (tail_copy(j).wait(), c)[1], 0)

    def issue_from(x_ref):
        def issue(r, carry):
            for kk in range(TOP_K):
                row_copy(x_ref, r, dest_ref[kk * t + i * tm + r]).start(priority=kk % 2)
            return carry

        lax.fori_loop(0, tm, issue, 0, unroll=DISPATCH_UNROLL)

    pl.when(i < ntp)(lambda: issue_from(xp_ref))
    pl.when(i >= ntp)(lambda: issue_from(xs_ref))
    for kk in range(TOP_K):
        pltpu.make_async_copy(xp_ref, xb_ref.at[pl.ds(0, tm)], sem).wait()


def _dispatch(dest_flat, pad_end, x1p, x1s, *, rows, tm, tile):
    d = x1p.shape[1]
    ntp, nts = x1p.shape[0] // tm, x1s.shape[0] // tm
    t = (ntp + nts) * tm
    ne = pad_end.shape[0]
    grid_spec = pltpu.PrefetchScalarGridSpec(
        num_scalar_prefetch=2,
        grid=(ntp + nts,),
        in_specs=[pl.BlockSpec((tm, d), lambda i, dest, pend: (jnp.minimum(i, ntp - 1), 0)),
                  pl.BlockSpec((tm, d), lambda i, dest, pend: (jnp.maximum(i - ntp, 0), 0))],
        out_specs=pl.BlockSpec(memory_space=pl.ANY),
        scratch_shapes=[pltpu.VMEM((tile, d), F32), pltpu.SemaphoreType.DMA, pltpu.SemaphoreType.DMA],
    )
    return pl.pallas_call(
        functools.partial(_dispatch_kernel, t=t, tm=tm, tile=tile, ne=ne, ntp=ntp),
        grid_spec=grid_spec,
        out_shape=jax.ShapeDtypeStruct((rows, d), F32),
        compiler_params=pltpu.CompilerParams(dimension_semantics=("arbitrary",),
                                             vmem_limit_bytes=VMEM_LIMIT),
        name="dispatch",
    )(dest_flat, pad_end, x1p, x1s)


def _expert_kernel(te_ref, nu_ref, x_ref, wgu_ref, bg_ref, bu_ref, wd_ref, bd_ref, y_ref, wg_s, wu_s, wd_s):
    i = pl.program_id(0)
    new_expert = jnp.logical_or(i == 0, te_ref[i] != te_ref[jnp.maximum(i - 1, 0)])

    @pl.when(jnp.logical_and(new_expert, i < nu_ref[0]))
    def _():
        half = LANES
        c = lax.broadcasted_iota(I32, (2 * half, 2 * half), 0)
        o = lax.broadcasted_iota(I32, (2 * half, 2 * half), 1)
        src = jnp.where(o < half, 2 * o, 2 * (o - half) + 1)
        perm = jnp.where(c == src, 1.0, 0.0).astype(MXU_DTYPE)
        for j in range(wgu_ref.shape[2] // (2 * half)):
            wb = wgu_ref[0, :, 2 * half * j:2 * half * (j + 1)].astype(MXU_DTYPE)
            y = jnp.dot(wb, perm, preferred_element_type=F32).astype(MXU_DTYPE)
            wg_s[:, half * j:half * (j + 1)] = y[:, :half]
            wu_s[:, half * j:half * (j + 1)] = y[:, half:]
        wd_s[...] = wd_ref[0].astype(MXU_DTYPE)

    @pl.when(i < nu_ref[0])
    def _():
        x = x_ref[...].astype(MXU_DTYPE)
        hg = jnp.dot(x, wg_s[...], preferred_element_type=F32) + bg_ref[0]
        hu = jnp.dot(x, wu_s[...], preferred_element_type=F32) + bu_ref[0]
        g = jnp.minimum(hg, SWIGLU_LIMIT)
        u = jnp.clip(hu, -SWIGLU_LIMIT, SWIGLU_LIMIT)
        a = g * _sigmoid(SWIGLU_ALPHA * g) * (u + 1.0)
        y_ref[...] = jnp.dot(a.astype(MXU_DTYPE), wd_s[...], preferred_element_type=F32) + bd_ref[0]

    @pl.when(i >= nu_ref[0])
    def _():
        y_ref[...] = jnp.zeros(y_ref.shape, F32)


def _experts(tile_e, n_used, xb, w_gate_up, bg, bu, w_down, bd, *, tm):
    rows, d = xb.shape
    dff = w_down.shape[1]
    nt = rows // tm
    ew = lambda i, te, nu: (te[i], 0, 0)
    grid_spec = pltpu.PrefetchScalarGridSpec(
        num_scalar_prefetch=2,
        grid=(nt,),
        in_specs=[pl.BlockSpec((tm, d), lambda i, te, nu: (jnp.minimum(i, nu[0] - 1), 0)),
                  pl.BlockSpec((1, d, 2 * dff), ew),
                  pl.BlockSpec((1, 1, dff), ew), pl.BlockSpec((1, 1, dff), ew),
                  pl.BlockSpec((1, dff, d), ew), pl.BlockSpec((1, 1, d), ew)],
        out_specs=pl.BlockSpec((tm, d), lambda i, te, nu: (i, 0)),
        scratch_shapes=[pltpu.VMEM((d, dff), MXU_DTYPE), pltpu.VMEM((d, dff), MXU_DTYPE),
                        pltpu.VMEM((dff, d), MXU_DTYPE)],
    )
    return pl.pallas_call(
        _expert_kernel,
        grid_spec=grid_spec,
        out_shape=jax.ShapeDtypeStruct((rows, d), F32),
        compiler_params=pltpu.CompilerParams(dimension_semantics=("arbitrary",),
                                             vmem_limit_bytes=VMEM_LIMIT),
        name="experts",
    )(tile_e, n_used, xb, w_gate_up, bg, bu, w_down, bd)


def _norm2_kernel(x1_ref, gate_ref, *refs, alpha):
    y_refs, (g_ref, b_ref, o_ref) = refs[:TOP_K], refs[TOP_K:]
    gate = gate_ref[...]
    ffn = gate[:, 0:1] * y_refs[0][...]
    for kk in range(1, TOP_K):
        ffn = ffn + gate[:, kk:kk + 1] * y_refs[kk][...]
    o_ref[...] = _layer_norm(alpha * x1_ref[...] + ffn, g_ref[...], b_ref[...])


def _norm2(x1, gate_t, y_slots, g, b, *, row0, tm, alpha):
    nrows, d = x1.shape
    off = row0 // tm
    per_slot = gate_t.shape[0] // tm
    src = lambda i: (i + off, 0)
    slot_src = lambda kk: (lambda i: (i + off + kk * per_slot, 0))
    y_pairs = [y_slots] * TOP_K
    return pl.pallas_call(
        functools.partial(_norm2_kernel, alpha=alpha),
        grid=(nrows // tm,),
        in_specs=[pl.BlockSpec((tm, d), lambda i: (i, 0)), pl.BlockSpec((tm, TOP_K), src)]
        + [pl.BlockSpec((tm, d), slot_src(kk)) for kk in range(TOP_K)]
        + [pl.BlockSpec((1, d), lambda i: (0, 0)), pl.BlockSpec((1, d), lambda i: (0, 0))],
        out_specs=pl.BlockSpec((tm, d), lambda i: (i, 0)),
        out_shape=jax.ShapeDtypeStruct((nrows, d), F32),
        compiler_params=pltpu.CompilerParams(dimension_semantics=("arbitrary",),
                                             vmem_limit_bytes=VMEM_LIMIT),
        name="norm2",
    )(x1, gate_t, *y_pairs, g, b)


def _moe(x1p, x1s, router_w, router_b, w_gate_up, b_gate_up, w_down, b_down):
    t = x1p.shape[0] + x1s.shape[0]
    ne = router_w.shape[1]
    idx, gate, rank, counts = _route(x1p, x1s, router_w.T, router_b.reshape(ne, 1), tm=TOKEN_TILE)
    tm = EXPERT_TILE
    counts = counts[:, 0].astype(I32)
    padded = (counts + tm - 1) // tm * tm
    pad_end = jnp.cumsum(padded)
    pad_start = pad_end - padded
    n_tiles = (t * TOP_K + ne * (tm - 1)) // tm
    rows = n_tiles * tm
    eids = jnp.arange(ne, dtype=I32)
    start_of = jnp.sum(jnp.where(idx[:, :, None] == eids, pad_start, 0), axis=-1)
    dest = start_of + rank
    tile_row = jnp.arange(n_tiles, dtype=I32) * tm
    tile_e = jnp.minimum(jnp.sum((pad_end[None, :] <= tile_row[:, None]).astype(I32), axis=1), ne - 1)
    n_used = (pad_end[-1:] // tm).astype(I32)
    xb = _dispatch(dest.reshape(-1), pad_end.astype(I32), x1p, x1s, rows=rows, tm=TOKEN_TILE, tile=tm)
    bg = b_gate_up[:, None, 0::2]
    bu = b_gate_up[:, None, 1::2]
    yb = _experts(tile_e, n_used, xb, w_gate_up, bg, bu, w_down, b_down[:, None, :], tm=tm)
    return gate.T, yb[dest.reshape(-1)]


def kernel(x_prompt, x_sample, cache_k, cache_v, state_conv, page_table, rel_bias, w_in, conv_w, w_att_o,
           w_conv_o, w_o, ln1_g, ln1_b, router_w, router_b, w_gate_up, b_gate_up, w_down, b_down, ln2_g, ln2_b):
    depth = w_in.shape[0]
    batch, seq, d = x_prompt.shape
    nseq, tq, _ = x_sample.shape
    n_pool = cache_k.shape[1]
    ppseq = page_table.shape[1]
    past = ppseq * PAGE_SIZE
    assert seq % MOBA_BLOCK == 0 and past % MOBA_BLOCK == 0 and tq % SUBLANES == 0 and tq <= PAGE_SIZE
    nb_past = past // MOBA_BLOCK
    tp, ts = batch * seq, nseq * tq
    alpha = (2 * depth) ** 0.25
    cw = conv_w.shape[2]
    tab_rows = jnp.repeat(rel_bias.T, tq, axis=0)
    ck = cache_k.transpose(0, 1, 3, 4, 2).reshape(depth * n_pool, ATT_W, PAGE_SIZE)
    cv = cache_v.transpose(0, 1, 3, 4, 2).reshape(depth * n_pool, ATT_W, PAGE_SIZE)
    hp = x_prompt.reshape(tp, d)
    hs = x_sample.reshape(ts, d)
    outs = [[] for _ in range(6)]
    for l in range(depth):
        w_in_b = w_in[l].astype(MXU_DTYPE)
        wao, wco, wo = (w_att_o[l].astype(MXU_DTYPE), w_conv_o[l].astype(MXU_DTYPE), w_o[l].astype(MXU_DTYPE))
        g1, b1, g2, b2 = ln1_g[l][None], ln1_b[l][None], ln2_g[l][None], ln2_b[l][None]
        qT, khm, vT, kp, vp, km, up, cbp, gap, gbp = _project(hp, w_in_b, tm=PROJ_TILE, head_major=True, seq=seq)
        nbt = PROJ_TILE // MOBA_BLOCK
        kmean = km[:, :nbt].reshape(batch, seq // MOBA_BLOCK, N_HEADS, HEAD_DIM).transpose(0, 2, 1, 3)
        kmean = kmean.reshape(batch * N_HEADS, seq // MOBA_BLOCK, HEAD_DIM)
        attn_p = _moba_prompt(rel_bias, qT, khm, vT, kmean, batch=batch, seq=seq)
        x1p = _mix(hp, attn_p, up, None, cbp, gap, gbp, conv_w[l], wao, wco, wo, g1, b1,
                         tm=PROJ_TILE, seq_len=seq, attn_transposed=True, alpha=alpha)
        qs, ks, vs, us, cbs, gas, gbs = _project(hs, w_in_b, tm=ts, head_major=False, wq=w_in[l][:, :ATT_W])
        pages = (page_table + l * n_pool).reshape(-1).astype(I32)
        attn_s = _moba_sample(pages, qs, ks, vs, tab_rows, ck, cv, nseq=nseq, tq=tq, nb_past=nb_past)
        state = (jnp.repeat(state_conv[l][:, 0], tq, axis=0), jnp.repeat(state_conv[l][:, 1], tq, axis=0))
        x1s = _mix(hs, attn_s, us, state, cbs, gas, gbs, conv_w[l], wao, wco, wo, g1, b1,
                         tm=ts, seq_len=tq, attn_transposed=False, alpha=alpha)
        gate_t, y_pairs = _moe(x1p, x1s, router_w[l], router_b[l], w_gate_up[l], b_gate_up[l], w_down[l], b_down[l])
        hp = _norm2(x1p, gate_t, y_pairs, g2, b2, row0=0, tm=TOKEN_TILE, alpha=alpha)
        hs = _norm2(x1s, gate_t, y_pairs, g2, b2, row0=tp, tm=TOKEN_TILE, alpha=alpha)
        outs[0].append(kp.reshape(batch, N_HEADS, HEAD_DIM, seq).transpose(0, 3, 1, 2))
        outs[1].append(vp.reshape(batch, N_HEADS, HEAD_DIM, seq).transpose(0, 3, 1, 2))
        outs[2].append(up.reshape(batch, seq, cw)[:, seq - (CONV_K - 1):])
        outs[3].append(ks.reshape(nseq, tq, N_HEADS, HEAD_DIM))
        outs[4].append(vs.reshape(nseq, tq, N_HEADS, HEAD_DIM))
        outs[5].append(us.reshape(nseq, tq, cw)[:, tq - (CONV_K - 1):])
    stacked = [jnp.stack(o) for o in outs]
    return (hp.reshape(batch, seq, d), hs.reshape(nseq, tq, d), *stacked)
```
